```python
import math
import jax, jax.numpy as jnp
from jax import lax
import numpy as np

D_MODEL = 1024
BATCH = 8
SEQ = 8192
DEPTH = 2

HEAD_DIM = 64
N_HEADS_A = 8
WIDTH_A = N_HEADS_A * HEAD_DIM
KV_RANK = 128
N_IDX_HEADS = 4
IDX_DIM = 64
INDEX_TOPK = 256
N_HEADS_B = 8
WIDTH_B = N_HEADS_B * HEAD_DIM
Q_BLOCK = 128
N_BUCKETS = 32
MAX_DISTANCE = 128
N_EXPERTS = 32
TOP_K = 4
D_EXPERT = 1024
SWIGLU_LIMIT = 7.0
SWIGLU_ALPHA = 1.702
MOE_BLOCK = 256
LN_EPS = 1e-5
RMS_EPS = 1e-6
DN_ALPHA = (2 * DEPTH) ** 0.25
DN_BETA = (8 * DEPTH) ** -0.25
COL_SPLITS = (WIDTH_A, KV_RANK, N_IDX_HEADS * IDX_DIM, IDX_DIM, N_IDX_HEADS,
              WIDTH_B, WIDTH_B, WIDTH_B, D_MODEL, D_MODEL)
N_COLS = sum(COL_SPLITS)

kernel_name = "hybrid_dsa_stickbreak_moe_deepnorm"


def layer_norm_f32(x):
    xf = x.astype(jnp.float32)
    mu = jnp.mean(xf, axis=-1, keepdims=True)
    var = jnp.mean(jnp.square(xf - mu), axis=-1, keepdims=True)
    return (xf - mu) * lax.rsqrt(var + LN_EPS)


def modulate(x, shift, scale):
    return (layer_norm_f32(x) * (1.0 + scale.astype(jnp.float32)) + shift.astype(jnp.float32)).astype(x.dtype)


def post_norm(z, g, b):
    return (layer_norm_f32(z) * g.astype(jnp.float32) + b.astype(jnp.float32)).astype(z.dtype)


def rms_norm(x, g):
    xf = x.astype(jnp.float32)
    y = xf * lax.rsqrt(jnp.mean(xf * xf, axis=-1, keepdims=True) + RMS_EPS)
    return (y * g.astype(jnp.float32)).astype(x.dtype)


def split_cols(p):
    out, o = [], 0
    for w in COL_SPLITS:
        out.append(p[..., o:o + w])
        o += w
    return out


def t5_bucket(n):
    max_exact = N_BUCKETS // 2
    nf = jnp.maximum(n, 1).astype(jnp.float32)
    large = max_exact + (jnp.log(nf / max_exact) / math.log(MAX_DISTANCE / max_exact)
                         * (N_BUCKETS - max_exact)).astype(jnp.int32)
    large = jnp.minimum(large, N_BUCKETS - 1)
    return jnp.where(n < max_exact, n, large)


def dsa_attention(q_lat, c_kv, q_idx, k_idx, w_idx, rel_bias):
    B, S = c_kv.shape[:2]
    n_blocks = S // Q_BLOCK
    topk = min(INDEX_TOPK, S // 4)
    key_pos = jnp.arange(S)

    def block(i):
        t0 = i * Q_BLOCK
        qpos = t0 + jnp.arange(Q_BLOCK)
        qi = lax.dynamic_slice_in_dim(q_idx, t0, Q_BLOCK, axis=1)
        wi = lax.dynamic_slice_in_dim(w_idx, t0, Q_BLOCK, axis=1)
        ql = lax.dynamic_slice_in_dim(q_lat, t0, Q_BLOCK, axis=1)
        dots = jax.nn.relu(jnp.einsum('bthd,bsd->bths', qi, k_idx).astype(jnp.float32))
        score = jnp.einsum('bths,bth->bts', dots, wi.astype(jnp.float32))
        causal = key_pos[None, :] <= qpos[:, None]
        score = jnp.where(causal[None], score, -jnp.inf)
        _, sel = lax.top_k(score, topk)
        c_sel = jax.vmap(lambda cb, ib: jnp.take(cb, ib, axis=0))(c_kv, sel)
        logits = jnp.einsum('bthr,btkr->bthk', ql, c_sel).astype(jnp.float32)
        dist = qpos[None, :, None] - sel
        bias = rel_bias[t5_bucket(jnp.maximum(dist, 0))]
        logits = logits + jnp.moveaxis(bias, -1, 2).astype(jnp.float32)
        logits = jnp.where((dist >= 0)[:, :, None, :], logits, -jnp.inf)
        p = jax.nn.softmax(logits, axis=-1).astype(c_kv.dtype)
        return jnp.einsum('bthk,btkr->bthr', p, c_sel)

    o = lax.map(block, jnp.arange(n_blocks))
    return jnp.moveaxis(o, 0, 1).reshape(B, S, N_HEADS_A, KV_RANK)


def stick_breaking_attention(q, k, v):
    B, S = q.shape[:2]
    n_blocks = S // Q_BLOCK
    key_pos = jnp.arange(S)
    scale = HEAD_DIM ** -0.5

    def block(i):
        t0 = i * Q_BLOCK
        qpos = t0 + jnp.arange(Q_BLOCK)
        qb = lax.dynamic_slice_in_dim(q, t0, Q_BLOCK, axis=1)
        z = jnp.einsum('bthd,bshd->bhts', qb, k).astype(jnp.float32) * scale
        strict = (key_pos[None, :] < qpos[:, None])[None, None]
        log_fail = jnp.where(strict, jax.nn.log_sigmoid(-z), 0.0)
        later = lax.cumsum(log_fail, axis=3, reverse=True) - log_fail
        a = jnp.where(strict, jnp.exp(jax.nn.log_sigmoid(z) + later), 0.0)
        return jnp.einsum('bhts,bshd->bthd', a.astype(v.dtype), v)

    o = lax.map(block, jnp.arange(n_blocks))
    return jnp.moveaxis(o, 0, 1).reshape(B, S, WIDTH_B)


def moe_ffn(h, w_router, b_router, w_gu, b_gu, w_dn, b_dn):
    B, S, D = h.shape
    N = B * S
    xt = h.reshape(N, D)
    logits = (xt @ w_router + b_router).astype(jnp.float32)
    top_val, top_idx = lax.top_k(logits, TOP_K)
    gate = jax.nn.softmax(top_val, axis=-1)
    flat_e = top_idx.reshape(-1)
    flat_g = gate.reshape(-1)
    order = jnp.argsort(flat_e)
    sorted_e = flat_e[order]
    token = (order // TOP_K).astype(jnp.int32)
    counts = jnp.zeros((N_EXPERTS,), jnp.int32).at[flat_e].add(1)
    padded = (counts + MOE_BLOCK - 1) // MOE_BLOCK * MOE_BLOCK
    start = jnp.cumsum(counts) - counts
    pad_end = jnp.cumsum(padded)
    pad_start = pad_end - padded
    dest = pad_start[sorted_e] + jnp.arange(N * TOP_K, dtype=jnp.int32) - start[sorted_e]
    n_blocks = -(-(N * TOP_K + N_EXPERTS * (MOE_BLOCK - 1)) // MOE_BLOCK)
    rows = n_blocks * MOE_BLOCK
    row_token = jnp.full((rows,), N, jnp.int32).at[dest].set(token)
    row_gate = jnp.zeros((rows,), jnp.float32).at[dest].set(flat_g[order])
    block_expert = jnp.minimum(
        jnp.searchsorted(pad_end, jnp.arange(n_blocks, dtype=jnp.int32) * MOE_BLOCK, side='right'),
        N_EXPERTS - 1)
    x_pad = jnp.concatenate([xt, jnp.zeros((1, D), xt.dtype)], axis=0)

    def expert_block(args):
        tok, g, e = args
        xb = x_pad[tok]
        gu = xb @ w_gu[e] + b_gu[e]
        a, u = jnp.split(gu, 2, axis=-1)
        a = jnp.minimum(a, SWIGLU_LIMIT)
        u = jnp.clip(u, -SWIGLU_LIMIT, SWIGLU_LIMIT)
        y = ((u + 1.0) * a * jax.nn.sigmoid(SWIGLU_ALPHA * a)) @ w_dn[e] + b_dn[e]
        return y * g[:, None].astype(y.dtype)

    ys = lax.map(expert_block, (row_token.reshape(n_blocks, MOE_BLOCK),
                                row_gate.reshape(n_blocks, MOE_BLOCK), block_expert))
    out = jax.ops.segment_sum(ys.reshape(rows, D), row_token, num_segments=N + 1)[:N]
    return out.reshape(B, S, D)


def setup_inputs(seed: int = 0) -> dict:
    key = jax.random.key(seed)
    ks = jax.random.split(key, 24)
    L, D, E, F = DEPTH, D_MODEL, N_EXPERTS, D_EXPERT

    def nrm(k, shape, s):
        return jax.random.normal(k, shape, jnp.float32) * s

    col_scale = jnp.concatenate([
        jnp.ones((sum(COL_SPLITS[:7]),), jnp.float32),
        jnp.full((WIDTH_B,), DN_BETA, jnp.float32),
        jnp.ones((2 * D,), jnp.float32)])
    return {
        "x": nrm(ks[0], (BATCH, SEQ, D), 1.0),
        "c": nrm(ks[1], (BATCH, D), 1.0),
        "rel_bias": nrm(ks[2], (N_BUCKETS, N_HEADS_A), 0.5),
        "w_ada": nrm(ks[3], (L, D, 6 * D), 0.5 * D ** -0.5),
        "b_ada": nrm(ks[4], (L, 6 * D), 0.01),
        "w_in": nrm(ks[5], (L, D, N_COLS), D ** -0.5) * col_scale,
        "g_kv": 1.0 + nrm(ks[6], (L, KV_RANK), 0.02),
        "w_uk": nrm(ks[7], (L, KV_RANK, N_HEADS_A, HEAD_DIM), KV_RANK ** -0.5),
        "w_uv": nrm(ks[8], (L, KV_RANK, N_HEADS_A, HEAD_DIM), KV_RANK ** -0.5 * DN_BETA),
        "w_a_out": nrm(ks[9], (L, WIDTH_A, D), WIDTH_A ** -0.5 * DN_BETA),
        "w_b_out": nrm(ks[10], (L, WIDTH_B, D), WIDTH_B ** -0.5 * DN_BETA),
        "w_o": nrm(ks[11], (L, D, D), D ** -0.5 * DN_BETA),
        "ln1_g": 1.0 + nrm(ks[12], (L, D), 0.02),
        "ln1_b": nrm(ks[13], (L, D), 0.01),
        "w_router": nrm(ks[14], (L, D, E), D ** -0.5),
        "b_router": nrm(ks[15], (L, E), 0.01),
        "w_gu": nrm(ks[16], (L, E, D, 2 * F), D ** -0.5),
        "b_gu": nrm(ks[17], (L, E, 2 * F), 0.01),
        "w_dn": nrm(ks[18], (L, E, F, D), F ** -0.5 * DN_BETA),
        "b_dn": nrm(ks[19], (L, E, D), 0.01),
        "ln2_g": 1.0 + nrm(ks[20], (L, D), 0.02),
        "ln2_b": nrm(ks[21], (L, D), 0.01),
    }


def reference(x, c, rel_bias, w_ada, b_ada, w_in, g_kv, w_uk, w_uv, w_a_out, w_b_out, w_o,
              ln1_g, ln1_b, w_router, b_router, w_gu, b_gu, w_dn, b_dn, ln2_g, ln2_b):
    B, S, D = x.shape
    cond = jax.nn.silu(c)
    idx_scale = (N_IDX_HEADS ** -0.5) * (IDX_DIM ** -0.5)
    for l in range(DEPTH):
        mod = cond @ w_ada[l] + b_ada[l]
        shift1, scale1, gate1, shift2, scale2, gate2 = [m[:, None, :] for m in jnp.split(mod, 6, axis=-1)]

        h = modulate(x, shift1, scale1)
        q_a, c_kv, q_idx, k_idx, w_idx, q_b, k_b, v_b, g_a, g_b = split_cols(h @ w_in[l])
        c_kv = rms_norm(c_kv, g_kv[l])
        q_a = q_a.reshape(B, S, N_HEADS_A, HEAD_DIM)
        q_lat = jnp.einsum('bshd,rhd->bshr', q_a, w_uk[l]) * (HEAD_DIM ** -0.5)
        o_lat = dsa_attention(q_lat, c_kv,
                              q_idx.reshape(B, S, N_IDX_HEADS, IDX_DIM), k_idx,
                              w_idx * idx_scale, rel_bias)
        o_a = jnp.einsum('bshr,rhd->bshd', o_lat, w_uv[l]).reshape(B, S, WIDTH_A)
        o_b = stick_breaking_attention(q_b.reshape(B, S, N_HEADS_B, HEAD_DIM),
                                       k_b.reshape(B, S, N_HEADS_B, HEAD_DIM),
                                       v_b.reshape(B, S, N_HEADS_B, HEAD_DIM))
        merged = jax.nn.sigmoid(g_a) * (o_a @ w_a_out[l]) + jax.nn.sigmoid(g_b) * (o_b @ w_b_out[l])
        y = merged @ w_o[l]
        x = post_norm(DN_ALPHA * x + gate1 * y, ln1_g[l], ln1_b[l])

        h = modulate(x, shift2, scale2)
        y = moe_ffn(h, w_router[l], b_router[l], w_gu[l], b_gu[l], w_dn[l], b_dn[l])
        x = post_norm(DN_ALPHA * x + gate2 * y, ln2_g[l], ln2_b[l])
    return x
```

```python
import functools
import math

import numpy as np
import jax
import jax.numpy as jnp
from jax import lax
from jax.experimental import pallas as pl
from jax.experimental.pallas import tpu as pltpu

D_MODEL = 1024
HEAD_DIM = 64
N_HEADS_A = 8
WIDTH_A = N_HEADS_A * HEAD_DIM
KV_RANK = 128
N_IDX_HEADS = 4
IDX_DIM = 64
INDEX_TOPK = 256
N_HEADS_B = 8
WIDTH_B = N_HEADS_B * HEAD_DIM
N_BUCKETS = 32
MAX_DISTANCE = 128
N_EXPERTS = 32
TOP_K = 4
D_EXPERT = 1024
SWIGLU_LIMIT = 7.0
SWIGLU_ALPHA = 1.702
LN_EPS = 1e-5
RMS_EPS = 1e-6
DEPTH = 2
DN_ALPHA = (2 * DEPTH) ** 0.25
IDX_SCALE = (N_IDX_HEADS ** -0.5) * (IDX_DIM ** -0.5)

COL_SPLITS = (WIDTH_A, KV_RANK, N_IDX_HEADS * IDX_DIM, IDX_DIM, N_IDX_HEADS,
              WIDTH_B, WIDTH_B, WIDTH_B, D_MODEL, D_MODEL)
N_COLS = sum(COL_SPLITS)

LANES = 128
VMEM_LIMIT = 56 * 1024 * 1024

IDX_PAD = LANES - IDX_DIM - N_IDX_HEADS
C_QA = 0
C_CKV = C_QA + WIDTH_A
C_QIDX = C_CKV + KV_RANK
C_KW = C_QIDX + N_IDX_HEADS * IDX_DIM
C_QB = C_KW + LANES
C_KB = C_QB + WIDTH_B
C_VB = C_KB + WIDTH_B
C_GA = C_VB + WIDTH_B
C_GB = C_GA + D_MODEL
N_COLS_PAD = C_GB + D_MODEL

ROW_TILE = 512
ATT_BLOCK = 256
BISECT_ROWS = 128
MOE_ROWS = 256
NEG_MASK = -1e30
M_INIT = -1e29
SB_SKIP = -110.0
INT_MIN = np.int32(-2 ** 31)

f32 = jnp.float32
bf16 = jnp.bfloat16


def _ln(x):
    mu = jnp.mean(x, axis=-1, keepdims=True)
    xc = x - mu
    var = jnp.mean(xc * xc, axis=-1, keepdims=True)
    return xc * lax.rsqrt(var + LN_EPS)


def _dot(a, b):
    return jnp.dot(a, b, preferred_element_type=f32)


def _dot_nt(a, b):
    return lax.dot_general(a, b, (((1,), (1,)), ((), ())), preferred_element_type=f32)


def _split_bf16(x):
    hi = x.astype(bf16)
    lo = (x - hi.astype(f32)).astype(bf16)
    return hi, lo


def _proj_kernel(x_ref, shift_ref, scale_ref, w_ref, gkv_ref,
                 qa_ref, ckv_ref, qidx_ref, kidx_ref, widx_ref, qb_ref, kb_ref, vb_ref, ga_ref, gb_ref):
    h = _ln(x_ref[...]) * (1.0 + scale_ref[0]) + shift_ref[0]
    hb = h.astype(bf16)

    def mm(c0, width):
        return _dot(hb, w_ref[:, c0:c0 + width])

    qa_ref[...] = mm(C_QA, WIDTH_A).astype(bf16)
    ckv = mm(C_CKV, KV_RANK)
    ckv = ckv * lax.rsqrt(jnp.mean(ckv * ckv, axis=-1, keepdims=True) + RMS_EPS) * gkv_ref[...]
    ckv_ref[...] = ckv.astype(bf16)
    qidx_ref[...] = mm(C_QIDX, N_IDX_HEADS * IDX_DIM).astype(bf16)
    kw = mm(C_KW, LANES)
    kidx_ref[...] = kw[:, :IDX_DIM].astype(bf16)
    widx_ref[...] = kw * IDX_SCALE
    qb_ref[...] = mm(C_QB, WIDTH_B).astype(bf16)
    kb_ref[...] = mm(C_KB, WIDTH_B).astype(bf16)
    vb_ref[...] = mm(C_VB, WIDTH_B).astype(bf16)
    for c in range(0, D_MODEL, 512):
        ga_ref[:, c:c + 512] = mm(C_GA + c, 512).astype(bf16)
        gb_ref[:, c:c + 512] = mm(C_GB + c, 512).astype(bf16)


def _proj(x2, shift, scale, w_pad, g_kv, seq):
    n, d = x2.shape
    tm = min(ROW_TILE, seq)
    per_b = seq // tm
    row = lambda i: (i, 0)
    bat = lambda i: (i // per_b, 0, 0)
    const = lambda i: (0, 0)
    widths = (WIDTH_A, KV_RANK, N_IDX_HEADS * IDX_DIM, IDX_DIM, LANES, WIDTH_B, WIDTH_B, WIDTH_B, D_MODEL, D_MODEL)
    dtypes = (bf16, bf16, bf16, bf16, f32, bf16, bf16, bf16, bf16, bf16)
    return pl.pallas_call(
        _proj_kernel,
        grid=(n // tm,),
        in_specs=[pl.BlockSpec((tm, d), row),
                  pl.BlockSpec((1, 1, d), bat),
                  pl.BlockSpec((1, 1, d), bat),
                  pl.BlockSpec((d, N_COLS_PAD), const),
                  pl.BlockSpec((1, KV_RANK), const)],
        out_specs=[pl.BlockSpec((tm, w), row) for w in widths],
        out_shape=[jax.ShapeDtypeStruct((n, w), dt) for w, dt in zip(widths, dtypes)],
        compiler_params=pltpu.CompilerParams(dimension_semantics=("parallel",), vmem_limit_bytes=VMEM_LIMIT),
        name="proj",
    )(x2, shift, scale, w_pad, g_kv)


def _dsa_kernel(far_ref, qa_ref, qidx_ref, widx_ref, kidx_ref, ckv_ref, wuk_ref, wuv_ref, bias_ref, tri_ref,
                o_ref, key_ref, qlat_ref, thr_ref, need_ref, ceq_ref, m_ref, l_ref, acc_ref, *, topk):
    tq = sk = ATT_BLOCK
    i = pl.program_id(1)
    n_blocks = i + 1

    for h in range(N_HEADS_A):
        ql = _dot(qa_ref[:, h * HEAD_DIM:(h + 1) * HEAD_DIM], wuk_ref[h]) * (HEAD_DIM ** -0.5)
        qlat_ref[h] = ql.astype(bf16)

    rowi = lax.broadcasted_iota(jnp.int32, (tq, sk), 0)
    coli = lax.broadcasted_iota(jnp.int32, (tq, sk), 1)
    w = widx_ref[:, IDX_DIM:IDX_DIM + N_IDX_HEADS]

    def score_block(j, carry):
        kblk = kidx_ref[pl.ds(pl.multiple_of(j * sk, sk), sk), :]
        s = jnp.zeros((tq, sk), f32)
        for h in range(N_IDX_HEADS):
            d = _dot_nt(qidx_ref[:, h * IDX_DIM:(h + 1) * IDX_DIM], kblk)
            s = s + jnp.maximum(d, 0.0) * w[:, h:h + 1]
        s = jnp.where((j - i) * sk + coli <= rowi, s, -jnp.inf)
        bits = pltpu.bitcast(s, jnp.int32)
        key_ref[j] = bits ^ ((bits >> 31) & np.int32(0x7FFFFFFF))
        return carry

    lax.fori_loop(0, n_blocks, score_block, 0)

    for r0 in range(0, tq, BISECT_ROWS):
        def count_ge(cand):
            cb = jnp.broadcast_to(cand, (BISECT_ROWS, LANES))

            def body(j, acc):
                for c in range(0, sk, LANES):
                    kk = key_ref[j, r0:r0 + BISECT_ROWS, c:c + LANES]
                    acc = acc + jnp.where(kk >= cb, 1.0, 0.0)
                return acc

            acc = lax.fori_loop(0, n_blocks, body, jnp.zeros((BISECT_ROWS, LANES), f32))
            return jnp.sum(acc, axis=1, keepdims=True)

        def bit_step(b, ans):
            cand_u = ans | jnp.left_shift(jnp.int32(1), 31 - b)
            cnt = count_ge(cand_u ^ INT_MIN)
            return jnp.where(cnt >= float(topk), cand_u, ans)

        ans = lax.fori_loop(0, 32, bit_step, jnp.zeros((BISECT_ROWS, 1), jnp.int32))
        thr = ans ^ INT_MIN
        n_gt = count_ge(thr + 1)
        thr_ref[r0:r0 + BISECT_ROWS, :] = jnp.broadcast_to(thr, (BISECT_ROWS, LANES))
        need_ref[r0:r0 + BISECT_ROWS, :] = jnp.broadcast_to(float(topk) - n_gt, (BISECT_ROWS, LANES))

    m_ref[...] = jnp.full(m_ref.shape, M_INIT, f32)
    l_ref[...] = jnp.zeros(l_ref.shape, f32)
    acc_ref[...] = jnp.zeros(acc_ref.shape, f32)
    ceq_ref[...] = jnp.zeros(ceq_ref.shape, f32)

    def attend(j, near):
        key = key_ref[j]
        thr = thr_ref[:, :1]
        eq = key == thr
        pref = _dot(jnp.where(eq, 1.0, 0.0).astype(bf16), tri_ref[...]) + ceq_ref[:, :1]
        sel = jnp.logical_or(key > thr, jnp.logical_and(eq, pref <= need_ref[:, :1]))
        if near == 1:
            sel = jnp.logical_and(sel, coli <= rowi)
        ceq_ref[...] = jnp.broadcast_to(pref[:, sk - 1:sk], ceq_ref.shape)
        cblk = ckv_ref[pl.ds(pl.multiple_of(j * sk, sk), sk), :]
        for h in range(N_HEADS_A):
            lg = _dot_nt(qlat_ref[h], cblk)
            if near is None:
                lg = lg + far_ref[h]
            else:
                lg = lg + bias_ref[h, :, near * sk:(near + 1) * sk]
            lg = jnp.where(sel, lg, NEG_MASK)
            m_old = m_ref[h]
            m_new = jnp.maximum(m_old, jnp.max(lg, axis=1, keepdims=True))
            alpha = jnp.exp(m_old - m_new)
            p = jnp.exp(lg - m_new[:, :1])
            l_ref[h] = alpha * l_ref[h] + jnp.sum(p, axis=1, keepdims=True)
            acc_ref[h] = alpha * acc_ref[h] + _dot(p.astype(bf16), cblk)
            m_ref[h] = m_new

    def far_block(j, carry):
        attend(j, None)
        return carry

    lax.fori_loop(0, jnp.maximum(i - 1, 0), far_block, 0)

    @pl.when(i >= 1)
    def _():
        attend(i - 1, 0)

    attend(i, 1)

    outs = []
    for h in range(N_HEADS_A):
        o_lat = acc_ref[h] / l_ref[h]
        outs.append(_dot(o_lat.astype(bf16), wuv_ref[h]))
    o_ref[...] = jnp.concatenate(outs, axis=1).astype(bf16)


def _t5_bucket(n):
    max_exact = N_BUCKETS // 2
    nf = jnp.maximum(n, 1).astype(f32)
    large = max_exact + (jnp.log(nf / max_exact) / math.log(MAX_DISTANCE / max_exact)
                         * (N_BUCKETS - max_exact)).astype(jnp.int32)
    large = jnp.minimum(large, N_BUCKETS - 1)
    return jnp.where(n < max_exact, n, large)


def _dsa(qa, qidx, widx, kidx, ckv, wuk_t, wuv_t, rel_bias, batch, seq):
    n = qa.shape[0]
    tq = ATT_BLOCK
    nq = seq // tq
    topk = min(INDEX_TOPK, seq // 4)
    dist = tq + jnp.arange(tq)[:, None] - jnp.arange(2 * tq)[None, :]
    bias_near = jnp.moveaxis(rel_bias[_t5_bucket(jnp.maximum(dist, 0))], -1, 0).astype(f32)
    far_n = np.float32(tq + 1)
    assert 16 + int(np.log(far_n / 16) / math.log(MAX_DISTANCE / 16) * 16) >= N_BUCKETS - 1
    bias_far = rel_bias[N_BUCKETS - 1].astype(f32)
    tri = (jnp.arange(tq)[:, None] <= jnp.arange(tq)[None, :]).astype(bf16)

    qrow = lambda b, i: (b * nq + i, 0)
    full = lambda b, i: (b, 0)
    c3 = lambda b, i: (0, 0, 0)
    c2 = lambda b, i: (0, 0)
    kern = functools.partial(_dsa_kernel, topk=topk)
    return pl.pallas_call(
        kern,
        grid=(batch, nq),
        in_specs=[pl.BlockSpec(memory_space=pltpu.SMEM),
                  pl.BlockSpec((tq, WIDTH_A), qrow),
                  pl.BlockSpec((tq, N_IDX_HEADS * IDX_DIM), qrow),
                  pl.BlockSpec((tq, LANES), qrow),
                  pl.BlockSpec((seq, IDX_DIM), full),
                  pl.BlockSpec((seq, KV_RANK), full),
                  pl.BlockSpec((N_HEADS_A, HEAD_DIM, KV_RANK), c3),
                  pl.BlockSpec((N_HEADS_A, KV_RANK, HEAD_DIM), c3),
                  pl.BlockSpec((N_HEADS_A, tq, 2 * tq), c3),
                  pl.BlockSpec((tq, tq), c2)],
        out_specs=pl.BlockSpec((tq, WIDTH_A), qrow),
        out_shape=jax.ShapeDtypeStruct((n, WIDTH_A), bf16),
        scratch_shapes=[pltpu.VMEM((nq, tq, tq), jnp.int32),
                        pltpu.VMEM((N_HEADS_A, tq, KV_RANK), bf16),
                        pltpu.VMEM((tq, LANES), jnp.int32),
                        pltpu.VMEM((tq, LANES), f32),
                        pltpu.VMEM((tq, LANES), f32),
                        pltpu.VMEM((N_HEADS_A, tq, LANES), f32),
                        pltpu.VMEM((N_HEADS_A, tq, LANES), f32),
                        pltpu.VMEM((N_HEADS_A, tq, KV_RANK), f32)],
        compiler_params=pltpu.CompilerParams(dimension_semantics=("parallel", "arbitrary"),
                                             vmem_limit_bytes=VMEM_LIMIT),
        name="dsa",
    )(bias_far, qa, qidx, widx, kidx, ckv, wuk_t, wuv_t, bias_near, tri)


def _sb_kernel(q_ref, k_ref, v_ref, tri_ref, o_ref, carry_ref, acc_ref):
    tq = sk = ATT_BLOCK
    i = pl.program_id(1)
    rowi = lax.broadcasted_iota(jnp.int32, (tq, sk), 0)
    coli = lax.broadcasted_iota(jnp.int32, (tq, sk), 1)
    scale = HEAD_DIM ** -0.5

    outs = []
    for h in range(N_HEADS_B):
        lanes = slice(h * HEAD_DIM, (h + 1) * HEAD_DIM)
        qh = q_ref[:, lanes]
        carry_ref[...] = jnp.zeros(carry_ref.shape, f32)
        acc_ref[...] = jnp.zeros(acc_ref.shape, f32)

        def cond(st):
            j, top = st
            return jnp.logical_and(j >= 0, top > SB_SKIP)

        def body(st):
            j, _ = st
            rows = pl.ds(pl.multiple_of(j * sk, sk), sk)
            z = _dot_nt(qh, k_ref[rows, lanes]) * scale
            strict = (j - i) * sk + coli < rowi
            softplus = jnp.maximum(z, 0.0) + jnp.log(1.0 + jnp.exp(-jnp.abs(z)))
            log_fail = jnp.where(strict, -softplus, 0.0)
            hi, lo = _split_bf16(log_fail)
            carry = carry_ref[:, :1]
            later = _dot(hi, tri_ref[...]) + _dot(lo, tri_ref[...]) + carry
            a = jnp.where(strict, jnp.exp(z + log_fail + later), 0.0)
            acc_ref[...] += _dot(a.astype(bf16), v_ref[rows, lanes])
            carry = carry + jnp.sum(log_fail, axis=1, keepdims=True)
            carry_ref[...] = jnp.broadcast_to(carry, carry_ref.shape)
            return j - 1, jnp.max(carry)

        lax.while_loop(cond, body, (i, jnp.float32(0.0)))
        outs.append(acc_ref[...])
    o_ref[...] = jnp.concatenate(outs, axis=1).astype(bf16)


def _sb(qb, kb, vb, batch, seq):
    n = qb.shape[0]
    tq = ATT_BLOCK
    nq = seq // tq
    tri = (jnp.arange(tq)[:, None] > jnp.arange(tq)[None, :]).astype(bf16)
    qrow = lambda b, i: (b * nq + i, 0)
    full = lambda b, i: (b, 0)
    return pl.pallas_call(
        _sb_kernel,
        grid=(batch, nq),
        in_specs=[pl.BlockSpec((tq, WIDTH_B), qrow),
                  pl.BlockSpec((seq, WIDTH_B), full),
                  pl.BlockSpec((seq, WIDTH_B), full),
                  pl.BlockSpec((tq, tq), lambda b, i: (0, 0))],
        out_specs=pl.BlockSpec((tq, WIDTH_B), qrow),
        out_shape=jax.ShapeDtypeStruct((n, WIDTH_B), bf16),
        scratch_shapes=[pltpu.VMEM((tq, LANES), f32),
                        pltpu.VMEM((tq, HEAD_DIM), f32)],
        compiler_params=pltpu.CompilerParams(dimension_semantics=("parallel", "arbitrary"),
                                             vmem_limit_bytes=VMEM_LIMIT),
        name="sb",
    )(qb, kb, vb, tri)


def _merge_kernel(oa_ref, ob_ref, ga_ref, gb_ref, x_ref, gate_ref, shift_ref, scale_ref,
                  wa_ref, wb_ref, wo_ref, g_ref, b_ref, wr_hi_ref, wr_lo_ref, br_ref,
                  x1_ref, h2_ref, logit_ref):
    ya = _dot(oa_ref[...], wa_ref[...])
    yb = _dot(ob_ref[...], wb_ref[...])
    merged = jax.nn.sigmoid(ga_ref[...].astype(f32)) * ya + jax.nn.sigmoid(gb_ref[...].astype(f32)) * yb
    y = _dot(merged.astype(bf16), wo_ref[...])
    x1 = _ln(DN_ALPHA * x_ref[...] + gate_ref[0] * y) * g_ref[...] + b_ref[...]
    x1_ref[...] = x1
    h2 = _ln(x1) * (1.0 + scale_ref[0]) + shift_ref[0]
    hi, lo = _split_bf16(h2)
    h2_ref[...] = hi
    logit_ref[...] = (_dot(hi, wr_hi_ref[...]) + _dot(lo, wr_hi_ref[...]) + _dot(hi, wr_lo_ref[...])
                      + br_ref[...])


def _merge(oa, ob, ga, gb, x2, gate1, shift2, scale2, wa, wb, wo, g, b, wr_hi, wr_lo, br, seq):
    n, d = x2.shape
    tm = min(ROW_TILE, seq)
    per_b = seq // tm
    row = lambda i: (i, 0)
    bat = lambda i: (i // per_b, 0, 0)
    const = lambda i: (0, 0)
    return pl.pallas_call(
        _merge_kernel,
        grid=(n // tm,),
        in_specs=[pl.BlockSpec((tm, WIDTH_A), row), pl.BlockSpec((tm, WIDTH_B), row),
                  pl.BlockSpec((tm, d), row), pl.BlockSpec((tm, d), row), pl.BlockSpec((tm, d), row),
                  pl.BlockSpec((1, 1, d), bat), pl.BlockSpec((1, 1, d), bat), pl.BlockSpec((1, 1, d), bat),
                  pl.BlockSpec((WIDTH_A, d), const), pl.BlockSpec((WIDTH_B, d), const), pl.BlockSpec((d, d), const),
                  pl.BlockSpec((1, d), const), pl.BlockSpec((1, d), const),
                  pl.BlockSpec((d, LANES), const), pl.BlockSpec((d, LANES), const), pl.BlockSpec((1, LANES), const)],
        out_specs=[pl.BlockSpec((tm, d), row), pl.BlockSpec((tm, d), row), pl.BlockSpec((tm, LANES), row)],
        out_shape=[jax.ShapeDtypeStruct((n, d), f32), jax.ShapeDtypeStruct((n, d), bf16),
                   jax.ShapeDtypeStruct((n, LANES), f32)],
        compiler_params=pltpu.CompilerParams(dimension_semantics=("parallel",), vmem_limit_bytes=VMEM_LIMIT),
        name="merge",
    )(oa, ob, ga, gb, x2, gate1, shift2, scale2, wa, wb, wo, g, b, wr_hi, wr_lo, br)


def _expert_kernel(be_ref, nb_ref, x_ref, wgu_ref, bgu_ref, wdn_ref, bdn_ref, y_ref):
    i = pl.program_id(0)

    @pl.when(i < nb_ref[0])
    def _():
        gu = _dot(x_ref[...], wgu_ref[0]) + bgu_ref[0]
        a = jnp.minimum(gu[:, :D_EXPERT], SWIGLU_LIMIT)
        u = jnp.clip(gu[:, D_EXPERT:], -SWIGLU_LIMIT, SWIGLU_LIMIT)
        act = (u + 1.0) * a * jax.nn.sigmoid(SWIGLU_ALPHA * a)
        y_ref[...] = _dot(act.astype(bf16), wdn_ref[0]) + bdn_ref[0]

    @pl.when(i >= nb_ref[0])
    def _():
        y_ref[...] = jnp.zeros(y_ref.shape, f32)


def _experts(xs, block_expert, n_used, wgu, bgu, wdn, bdn):
    rows, d = xs.shape
    n_blocks = rows // MOE_ROWS
    e_idx = lambda i, be, nb: (be[i], 0, 0)
    grid_spec = pltpu.PrefetchScalarGridSpec(
        num_scalar_prefetch=2,
        grid=(n_blocks,),
        in_specs=[pl.BlockSpec((MOE_ROWS, d), lambda i, be, nb: (i, 0)),
                  pl.BlockSpec((1, d, 2 * D_EXPERT), e_idx),
                  pl.BlockSpec((1, 1, 2 * D_EXPERT), e_idx),
                  pl.BlockSpec((1, D_EXPERT, d), e_idx),
                  pl.BlockSpec((1, 1, d), e_idx)],
        out_specs=pl.BlockSpec((MOE_ROWS, d), lambda i, be, nb: (i, 0)),
    )
    return pl.pallas_call(
        _expert_kernel,
        grid_spec=grid_spec,
        out_shape=jax.ShapeDtypeStruct((rows, d), f32),
        compiler_params=pltpu.CompilerParams(dimension_semantics=("arbitrary",), vmem_limit_bytes=VMEM_LIMIT),
        name="experts",
    )(block_expert, n_used, xs, wgu, bgu, wdn, bdn)


def _combine_kernel(x_ref, y_ref, gk_ref, gate_ref, g_ref, b_ref, o_ref):
    gk = gk_ref[...]
    y = y_ref[0] * gk[:, 0:1]
    for k in range(1, TOP_K):
        y = y + y_ref[k] * gk[:, k:k + 1]
    o_ref[...] = _ln(DN_ALPHA * x_ref[...] + gate_ref[0] * y) * g_ref[...] + b_ref[...]


def _combine(x1, yk, gates, gate2, g, b, seq):
    n, d = x1.shape
    tm = min(ROW_TILE, seq)
    per_b = seq // tm
    row = lambda i: (i, 0)
    const = lambda i: (0, 0)
    return pl.pallas_call(
        _combine_kernel,
        grid=(n // tm,),
        in_specs=[pl.BlockSpec((tm, d), row),
                  pl.BlockSpec((TOP_K, tm, d), lambda i: (0, i, 0)),
                  pl.BlockSpec((tm, TOP_K), row),
                  pl.BlockSpec((1, 1, d), lambda i: (i // per_b, 0, 0)),
                  pl.BlockSpec((1, d), const), pl.BlockSpec((1, d), const)],
        out_specs=pl.BlockSpec((tm, d), row),
        out_shape=jax.ShapeDtypeStruct((n, d), f32),
        compiler_params=pltpu.CompilerParams(dimension_semantics=("parallel",), vmem_limit_bytes=VMEM_LIMIT),
        name="combine",
    )(x1, yk, gates, gate2, g, b)


def _route(logits, n):
    top_val, top_idx = lax.top_k(logits, TOP_K)
    gates = jax.nn.softmax(top_val, axis=-1)
    onehot = jnp.sum(jax.nn.one_hot(top_idx, N_EXPERTS, dtype=jnp.int32), axis=1)
    before = jnp.cumsum(onehot, axis=0) - onehot
    counts = before[-1] + onehot[-1]
    padded = (counts + MOE_ROWS - 1) // MOE_ROWS * MOE_ROWS
    pad_end = jnp.cumsum(padded)
    pad_start = pad_end - padded
    dest = pad_start[top_idx] + jnp.take_along_axis(before, top_idx, axis=1)
    n_blocks = -(-(n * TOP_K + N_EXPERTS * (MOE_ROWS - 1)) // MOE_ROWS)
    block_expert = jnp.minimum(
        jnp.searchsorted(pad_end, jnp.arange(n_blocks, dtype=jnp.int32) * MOE_ROWS, side='right'),
        N_EXPERTS - 1).astype(jnp.int32)
    n_used = (pad_end[-1:] // MOE_ROWS).astype(jnp.int32)
    return gates, dest.astype(jnp.int32), block_expert, n_used, n_blocks


def _pad_w_in(w):
    z = jnp.zeros((w.shape[0], IDX_PAD), w.dtype)
    split = sum(COL_SPLITS[:5])
    return jnp.concatenate([w[:, :split], z, w[:, split:]], axis=1).astype(bf16)


def kernel(x, c, rel_bias, w_ada, b_ada, w_in, g_kv, w_uk, w_uv, w_a_out, w_b_out, w_o, ln1_g, ln1_b,
           w_router, b_router, w_gu, b_gu, w_dn, b_dn, ln2_g, ln2_b):
    B, S, D = x.shape
    N = B * S
    assert D == D_MODEL and S % ATT_BLOCK == 0
    cond = jax.nn.silu(c)
    x2 = x.reshape(N, D)
    for l in range(DEPTH):
        mod = jnp.dot(cond, w_ada[l], precision=lax.Precision.HIGHEST) + b_ada[l]
        shift1, scale1, gate1, shift2, scale2, gate2 = [m[:, None, :] for m in jnp.split(mod, 6, axis=-1)]

        qa, ckv, qidx, kidx, widx, qb, kb, vb, ga, gb = _proj(
            x2, shift1, scale1, _pad_w_in(w_in[l]), g_kv[l][None, :], S)
        wuk_t = jnp.transpose(w_uk[l], (1, 2, 0)).astype(bf16)
        wuv_t = jnp.transpose(w_uv[l], (1, 0, 2)).astype(bf16)
        oa = _dsa(qa, qidx, widx, kidx, ckv, wuk_t, wuv_t, rel_bias, B, S)
        ob = _sb(qb, kb, vb, B, S)

        wr = jnp.pad(w_router[l], ((0, 0), (0, LANES - N_EXPERTS)))
        wr_hi, wr_lo = _split_bf16(wr)
        br = jnp.pad(b_router[l], (0, LANES - N_EXPERTS))[None, :]
        x1, h2, logits = _merge(oa, ob, ga, gb, x2, gate1, shift2, scale2,
                                w_a_out[l].astype(bf16), w_b_out[l].astype(bf16), w_o[l].astype(bf16),
                                ln1_g[l][None, :], ln1_b[l][None, :], wr_hi, wr_lo, br, S)

        gates, dest, block_expert, n_used, n_blocks = _route(logits[:, :N_EXPERTS], N)
        rows = n_blocks * MOE_ROWS
        row_token = jnp.full((rows,), N, jnp.int32).at[dest.reshape(-1)].set(
            jnp.repeat(jnp.arange(N, dtype=jnp.int32), TOP_K))
        h2_pad = jnp.concatenate([h2, jnp.zeros((1, D), h2.dtype)], axis=0)
        xs = h2_pad[row_token]
        ys = _experts(xs, block_expert, n_used, w_gu[l].astype(bf16), b_gu[l][:, None, :],
                      w_dn[l].astype(bf16), b_dn[l][:, None, :])
        yk = ys[dest.T]
        x2 = _combine(x1, yk, gates, gate2, ln2_g[l][None, :], ln2_b[l][None, :], S)
    return x2.reshape(B, S, D)
```

```python
import functools
import math

import numpy as np
import jax
import jax.numpy as jnp
from jax import lax
from jax.experimental import pallas as pl
from jax.experimental.pallas import tpu as pltpu

D_MODEL = 1024
HEAD_DIM = 64
N_HEADS_A = 8
WIDTH_A = N_HEADS_A * HEAD_DIM
KV_RANK = 128
N_IDX_HEADS = 4
IDX_DIM = 64
INDEX_TOPK = 256
N_HEADS_B = 8
WIDTH_B = N_HEADS_B * HEAD_DIM
N_BUCKETS = 32
MAX_DISTANCE = 128
N_EXPERTS = 32
TOP_K = 4
D_EXPERT = 1024
SWIGLU_LIMIT = 7.0
SWIGLU_ALPHA = 1.702
LN_EPS = 1e-5
RMS_EPS = 1e-6
DEPTH = 2
DN_ALPHA = (2 * DEPTH) ** 0.25
IDX_SCALE = (N_IDX_HEADS ** -0.5) * (IDX_DIM ** -0.5)
LOG2E = math.log2(math.e)

COL_SPLITS = (WIDTH_A, KV_RANK, N_IDX_HEADS * IDX_DIM, IDX_DIM, N_IDX_HEADS,
              WIDTH_B, WIDTH_B, WIDTH_B, D_MODEL, D_MODEL)
N_COLS = sum(COL_SPLITS)

LANES = 128
VMEM_LIMIT = 56 * 1024 * 1024

IDX_PAD = LANES - IDX_DIM - N_IDX_HEADS
C_QA = 0
C_CKV = C_QA + WIDTH_A
C_QIDX = C_CKV + KV_RANK
C_KW = C_QIDX + N_IDX_HEADS * IDX_DIM
C_QB = C_KW + LANES
C_KB = C_QB + WIDTH_B
C_VB = C_KB + WIDTH_B
C_GA = C_VB + WIDTH_B
C_GB = C_GA + D_MODEL
N_COLS_PAD = C_GB + D_MODEL

ROW_TILE = 512
ATT_BLOCK = 256
BISECT_ROWS = 128
MOE_ROWS = 256
NEG_MASK = -1e30
M_INIT = -1e29
SB_SKIP = -110.0
INT_MIN = np.int32(-2 ** 31)

f32 = jnp.float32
bf16 = jnp.bfloat16


def _ln(x):
    mu = jnp.mean(x, axis=-1, keepdims=True)
    xc = x - mu
    var = jnp.mean(xc * xc, axis=-1, keepdims=True)
    return xc * lax.rsqrt(var + LN_EPS)


def _dot(a, b):
    return jnp.dot(a, b, preferred_element_type=f32)


def _dot_nt(a, b):
    return lax.dot_general(a, b, (((1,), (1,)), ((), ())), preferred_element_type=f32)


def _split_bf16(x):
    hi = x.astype(bf16)
    lo = (x - hi.astype(f32)).astype(bf16)
    return hi, lo


def _proj_kernel(x_ref, shift_ref, scale_ref, w_ref, gkv_ref,
                 qa_ref, ckv_ref, qidx_ref, kidx_ref, widx_ref, qb_ref, kb_ref, vb_ref, ga_ref, gb_ref):
    h = _ln(x_ref[...]) * (1.0 + scale_ref[0]) + shift_ref[0]
    hb = h.astype(bf16)

    def mm(c0, width):
        return _dot(hb, w_ref[:, c0:c0 + width])

    qa_ref[...] = mm(C_QA, WIDTH_A).astype(bf16)
    ckv = mm(C_CKV, KV_RANK)
    ckv = ckv * lax.rsqrt(jnp.mean(ckv * ckv, axis=-1, keepdims=True) + RMS_EPS) * gkv_ref[...]
    ckv_ref[...] = jnp.concatenate([ckv, jnp.ones_like(ckv)], axis=1).astype(bf16)
    qidx_ref[...] = mm(C_QIDX, N_IDX_HEADS * IDX_DIM).astype(bf16)
    kw = mm(C_KW, LANES)
    kidx_ref[...] = kw[:, :IDX_DIM].astype(bf16)
    widx_ref[...] = kw * IDX_SCALE
    qb_ref[...] = mm(C_QB, WIDTH_B).astype(bf16)
    kb_ref[...] = mm(C_KB, WIDTH_B).astype(bf16)
    vb_ref[...] = mm(C_VB, WIDTH_B).astype(bf16)
    for c in range(0, D_MODEL, 512):
        ga_ref[:, c:c + 512] = mm(C_GA + c, 512).astype(bf16)
        gb_ref[:, c:c + 512] = mm(C_GB + c, 512).astype(bf16)


def _proj(x2, shift, scale, w_pad, g_kv, seq):
    n, d = x2.shape
    tm = min(ROW_TILE, seq)
    per_b = seq // tm
    row = lambda i: (i, 0)
    bat = lambda i: (i // per_b, 0, 0)
    const = lambda i: (0, 0)
    widths = (WIDTH_A, 2 * KV_RANK, N_IDX_HEADS * IDX_DIM, IDX_DIM, LANES, WIDTH_B, WIDTH_B, WIDTH_B, D_MODEL, D_MODEL)
    dtypes = (bf16, bf16, bf16, bf16, f32, bf16, bf16, bf16, bf16, bf16)
    return pl.pallas_call(
        _proj_kernel,
        grid=(n // tm,),
        in_specs=[pl.BlockSpec((tm, d), row),
                  pl.BlockSpec((1, 1, d), bat),
                  pl.BlockSpec((1, 1, d), bat),
                  pl.BlockSpec((d, N_COLS_PAD), const),
                  pl.BlockSpec((1, KV_RANK), const)],
        out_specs=[pl.BlockSpec((tm, w), row) for w in widths],
        out_shape=[jax.ShapeDtypeStruct((n, w), dt) for w, dt in zip(widths, dtypes)],
        compiler_params=pltpu.CompilerParams(dimension_semantics=("parallel",), vmem_limit_bytes=VMEM_LIMIT),
        name="proj",
    )(x2, shift, scale, w_pad, g_kv)


def _dsa_kernel(qa_ref, qidx_ref, widx_ref, kidx_ref, ckv_ref, wuk_ref, wuv_ref, bias_ref, tri_ref,
                o_ref, key_ref, qlat_ref, wrep_ref, thr_ref, need_ref, ceq_ref, mb_ref, m_ref, acc_ref, *, topk):
    tq = sk = ATT_BLOCK
    i = pl.program_id(1)
    n_blocks = i + 1

    for h in range(N_HEADS_A):
        ql = _dot(qa_ref[:, h * HEAD_DIM:(h + 1) * HEAD_DIM], wuk_ref[h]) * (HEAD_DIM ** -0.5 * LOG2E)
        qlat_ref[h] = ql.astype(bf16)

    rowi = lax.broadcasted_iota(jnp.int32, (tq, sk), 0)
    coli = lax.broadcasted_iota(jnp.int32, (tq, sk), 1)
    w = widx_ref[:, IDX_DIM:IDX_DIM + N_IDX_HEADS]
    for h in range(N_IDX_HEADS):
        wrep_ref[h] = jnp.broadcast_to(w[:, h:h + 1], (tq, LANES))

    def score_block(j, carry):
        kblk = kidx_ref[pl.ds(pl.multiple_of(j * sk, sk), sk), :]
        s = None
        for h in range(N_IDX_HEADS):
            d = _dot_nt(qidx_ref[:, h * IDX_DIM:(h + 1) * IDX_DIM], kblk)
            t = jnp.maximum(d, 0.0) * jnp.concatenate([wrep_ref[h]] * (sk // LANES), axis=1)
            s = t if s is None else s + t
        s = jnp.where((j - i) * sk + coli <= rowi, s, -jnp.inf)
        bits = pltpu.bitcast(s, jnp.int32)
        key_ref[j] = bits ^ ((bits >> 31) & np.int32(0x7FFFFFFF))
        return carry

    lax.fori_loop(0, n_blocks, score_block, 0)

    for r0 in range(0, tq, BISECT_ROWS):
        def count_ge(cand):
            cb = jnp.broadcast_to(cand, (BISECT_ROWS, LANES))

            def body(j, acc):
                for c in range(0, sk, LANES):
                    kk = key_ref[j, r0:r0 + BISECT_ROWS, c:c + LANES]
                    acc = acc + jnp.where(kk >= cb, 1.0, 0.0)
                return acc

            acc = lax.fori_loop(0, n_blocks, body, jnp.zeros((BISECT_ROWS, LANES), f32))
            return jnp.sum(acc, axis=1, keepdims=True)

        def bit_step(b, ans):
            cand_u = ans | jnp.left_shift(jnp.int32(1), 31 - b)
            cnt = count_ge(cand_u ^ INT_MIN)
            return jnp.where(cnt >= float(topk), cand_u, ans)

        ans = lax.fori_loop(0, 32, bit_step, jnp.zeros((BISECT_ROWS, 1), jnp.int32))
        thr = ans ^ INT_MIN
        n_gt = count_ge(thr + 1)
        thr_ref[r0:r0 + BISECT_ROWS, :] = jnp.broadcast_to(thr, (BISECT_ROWS, LANES))
        need_ref[r0:r0 + BISECT_ROWS, :] = jnp.broadcast_to(float(topk) - n_gt, (BISECT_ROWS, LANES))

    m_ref[...] = jnp.full(m_ref.shape, M_INIT, f32)
    acc_ref[...] = jnp.zeros(acc_ref.shape, f32)
    ceq_ref[...] = jnp.zeros(ceq_ref.shape, f32)

    def key_rows(j):
        return pl.ds(pl.multiple_of(j * sk, sk), sk)

    def mask_block(j, slot, diag):
        thr, need, ceq = thr_ref[...], need_ref[...], ceq_ref[...]
        keys = [key_ref[j, :, c:c + LANES] for c in range(0, sk, LANES)]
        eqs = [k == thr for k in keys]
        eqf = jnp.concatenate([jnp.where(e, 1.0, 0.0) for e in eqs], axis=1).astype(bf16)
        pref = _dot(eqf, tri_ref[...])
        total = jnp.broadcast_to(pref[:, sk - 1:sk], (tq, LANES))
        for n, (k, e) in enumerate(zip(keys, eqs)):
            c = n * LANES
            sel = jnp.logical_or(k > thr, jnp.logical_and(e, pref[:, c:c + LANES] + ceq <= need))
            if diag:
                ri = lax.broadcasted_iota(jnp.int32, (tq, LANES), 0)
                ci = lax.broadcasted_iota(jnp.int32, (tq, LANES), 1)
                sel = jnp.logical_and(sel, ci + c <= ri)
            mb_ref[:, slot * sk + c:slot * sk + c + LANES] = jnp.where(sel, 0.0, NEG_MASK)
        ceq_ref[...] = ceq + total

    def attend(j0, n_blk, near_col):
        for b in range(n_blk):
            mask_block(j0 + b, b, near_col is not None and b == n_blk - 1)
        for h in range(N_HEADS_A):
            q = qlat_ref[h]
            parts = []
            for b in range(n_blk):
                lg = _dot_nt(q, ckv_ref[key_rows(j0 + b), :KV_RANK]) + mb_ref[:, b * sk:(b + 1) * sk]
                if near_col is not None:
                    lg = lg + bias_ref[h, :, near_col + b * sk:near_col + (b + 1) * sk]
                parts.append(lg)
            mx = functools.reduce(jnp.maximum, [p[:, c:c + LANES] for p in parts for c in range(0, sk, LANES)])
            m_old = m_ref[h]
            m_new = jnp.maximum(m_old, jnp.max(mx, axis=1, keepdims=True))
            m_wide = jnp.concatenate([m_new] * (sk // LANES), axis=1)
            pv = None
            for b in range(n_blk):
                d = _dot(jnp.exp2(parts[b] - m_wide).astype(bf16), ckv_ref[key_rows(j0 + b), :])
                pv = d if pv is None else pv + d
            alpha = jnp.exp2(m_old - m_new)
            acc_ref[h] = jnp.concatenate([alpha, alpha], axis=1) * acc_ref[h] + pv
            m_ref[h] = m_new

    lone = jnp.logical_and(i >= 2, i % 2 == 0)

    @pl.when(lone)
    def _():
        attend(0, 1, None)

    first = jnp.where(lone, 1, 0)

    def far_pair(p, carry):
        attend(first + 2 * p, 2, None)
        return carry

    lax.fori_loop(0, (i - 1 - first) // 2, far_pair, 0)

    @pl.when(i >= 1)
    def _():
        attend(i - 1, 2, 0)

    @pl.when(i == 0)
    def _():
        attend(0, 1, sk)

    outs = []
    for h in range(N_HEADS_A):
        acc = acc_ref[h]
        o_lat = acc[:, :KV_RANK] / acc[:, KV_RANK:]
        outs.append(_dot(o_lat.astype(bf16), wuv_ref[h]))
    o_ref[...] = jnp.concatenate(outs, axis=1).astype(bf16)


def _t5_bucket(n):
    max_exact = N_BUCKETS // 2
    nf = jnp.maximum(n, 1).astype(f32)
    large = max_exact + (jnp.log(nf / max_exact) / math.log(MAX_DISTANCE / max_exact)
                         * (N_BUCKETS - max_exact)).astype(jnp.int32)
    large = jnp.minimum(large, N_BUCKETS - 1)
    return jnp.where(n < max_exact, n, large)


def _dsa(qa, qidx, widx, kidx, ckv, wuk_t, wuv_t, rel_bias, batch, seq):
    n = qa.shape[0]
    tq = ATT_BLOCK
    nq = seq // tq
    topk = min(INDEX_TOPK, seq // 4)
    dist = tq + jnp.arange(tq)[:, None] - jnp.arange(2 * tq)[None, :]
    bias_near = jnp.moveaxis(rel_bias[_t5_bucket(jnp.maximum(dist, 0))], -1, 0).astype(f32)
    far_n = np.float32(tq + 1)
    assert 16 + int(np.log(far_n / 16) / math.log(MAX_DISTANCE / 16) * 16) >= N_BUCKETS - 1
    bias_near = (bias_near - rel_bias[N_BUCKETS - 1].astype(f32)[:, None, None]) * LOG2E
    tri = (jnp.arange(tq)[:, None] <= jnp.arange(tq)[None, :]).astype(bf16)

    qrow = lambda b, i: (b * nq + i, 0)
    full = lambda b, i: (b, 0)
    c3 = lambda b, i: (0, 0, 0)
    c2 = lambda b, i: (0, 0)
    kern = functools.partial(_dsa_kernel, topk=topk)
    return pl.pallas_call(
        kern,
        grid=(batch, nq),
        in_specs=[pl.BlockSpec((tq, WIDTH_A), qrow),
                  pl.BlockSpec((tq, N_IDX_HEADS * IDX_DIM), qrow),
                  pl.BlockSpec((tq, LANES), qrow),
                  pl.BlockSpec((seq, IDX_DIM), full),
                  pl.BlockSpec((seq, 2 * KV_RANK), full),
                  pl.BlockSpec((N_HEADS_A, HEAD_DIM, KV_RANK), c3),
                  pl.BlockSpec((N_HEADS_A, KV_RANK, HEAD_DIM), c3),
                  pl.BlockSpec((N_HEADS_A, tq, 2 * tq), c3),
                  pl.BlockSpec((tq, tq), c2)],
        out_specs=pl.BlockSpec((tq, WIDTH_A), qrow),
        out_shape=jax.ShapeDtypeStruct((n, WIDTH_A), bf16),
        scratch_shapes=[pltpu.VMEM((nq, tq, tq), jnp.int32),
                        pltpu.VMEM((N_HEADS_A, tq, KV_RANK), bf16),
                        pltpu.VMEM((N_IDX_HEADS, tq, LANES), f32),
                        pltpu.VMEM((tq, LANES), jnp.int32),
                        pltpu.VMEM((tq, LANES), f32),
                        pltpu.VMEM((tq, LANES), f32),
                        pltpu.VMEM((tq, 2 * tq), f32),
                        pltpu.VMEM((N_HEADS_A, tq, LANES), f32),
                        pltpu.VMEM((N_HEADS_A, tq, 2 * KV_RANK), f32)],
        compiler_params=pltpu.CompilerParams(dimension_semantics=("parallel", "arbitrary"),
                                             vmem_limit_bytes=VMEM_LIMIT),
        name="dsa",
    )(qa, qidx, widx, kidx, ckv, wuk_t, wuv_t, bias_near, tri)


def _sb_kernel(q_ref, k_ref, v_ref, tri_ref, o_ref, carry_ref, acc_ref):
    tq = sk = ATT_BLOCK
    i = pl.program_id(1)
    rowi = lax.broadcasted_iota(jnp.int32, (tq, sk), 0)
    coli = lax.broadcasted_iota(jnp.int32, (tq, sk), 1)
    scale = HEAD_DIM ** -0.5

    outs = []
    for h in range(N_HEADS_B):
        lanes = slice(h * HEAD_DIM, (h + 1) * HEAD_DIM)
        qh = q_ref[:, lanes]
        carry_ref[...] = jnp.zeros(carry_ref.shape, f32)
        acc_ref[...] = jnp.zeros(acc_ref.shape, f32)

        def cond(st):
            j, top = st
            return jnp.logical_and(j >= 0, top > SB_SKIP)

        def body(st):
            j, _ = st
            rows = pl.ds(pl.multiple_of(j * sk, sk), sk)
            z = _dot_nt(qh, k_ref[rows, lanes]) * scale
            strict = (j - i) * sk + coli < rowi
            softplus = jnp.maximum(z, 0.0) + jnp.log(1.0 + jnp.exp(-jnp.abs(z)))
            log_fail = jnp.where(strict, -softplus, 0.0)
            hi, lo = _split_bf16(log_fail)
            carry = carry_ref[:, :1]
            later = _dot(hi, tri_ref[...]) + _dot(lo, tri_ref[...]) + carry
            a = jnp.where(strict, jnp.exp(z + log_fail + later), 0.0)
            acc_ref[...] += _dot(a.astype(bf16), v_ref[rows, lanes])
            carry = carry + jnp.sum(log_fail, axis=1, keepdims=True)
            carry_ref[...] = jnp.broadcast_to(carry, carry_ref.shape)
            return j - 1, jnp.max(carry)

        lax.while_loop(cond, body, (i, jnp.float32(0.0)))
        outs.append(acc_ref[...])
    o_ref[...] = jnp.concatenate(outs, axis=1).astype(bf16)


def _sb(qb, kb, vb, batch, seq):
    n = qb.shape[0]
    tq = ATT_BLOCK
    nq = seq // tq
    tri = (jnp.arange(tq)[:, None] > jnp.arange(tq)[None, :]).astype(bf16)
    qrow = lambda b, i: (b * nq + i, 0)
    full = lambda b, i: (b, 0)
    return pl.pallas_call(
        _sb_kernel,
        grid=(batch, nq),
        in_specs=[pl.BlockSpec((tq, WIDTH_B), qrow),
                  pl.BlockSpec((seq, WIDTH_B), full),
                  pl.BlockSpec((seq, WIDTH_B), full),
                  pl.BlockSpec((tq, tq), lambda b, i: (0, 0))],
        out_specs=pl.BlockSpec((tq, WIDTH_B), qrow),
        out_shape=jax.ShapeDtypeStruct((n, WIDTH_B), bf16),
        scratch_shapes=[pltpu.VMEM((tq, LANES), f32),
                        pltpu.VMEM((tq, HEAD_DIM), f32)],
        compiler_params=pltpu.CompilerParams(dimension_semantics=("parallel", "arbitrary"),
                                             vmem_limit_bytes=VMEM_LIMIT),
        name="sb",
    )(qb, kb, vb, tri)


def _merge_kernel(oa_ref, ob_ref, ga_ref, gb_ref, x_ref, gate_ref, shift_ref, scale_ref,
                  wa_ref, wb_ref, wo_ref, g_ref, b_ref, wr_hi_ref, wr_lo_ref, br_ref,
                  x1_ref, h2_ref, logit_ref):
    ya = _dot(oa_ref[...], wa_ref[...])
    yb = _dot(ob_ref[...], wb_ref[...])
    merged = jax.nn.sigmoid(ga_ref[...].astype(f32)) * ya + jax.nn.sigmoid(gb_ref[...].astype(f32)) * yb
    y = _dot(merged.astype(bf16), wo_ref[...])
    x1 = _ln(DN_ALPHA * x_ref[...] + gate_ref[0] * y) * g_ref[...] + b_ref[...]
    x1_ref[...] = x1
    h2 = _ln(x1) * (1.0 + scale_ref[0]) + shift_ref[0]
    hi, lo = _split_bf16(h2)
    h2_ref[...] = hi
    logit_ref[...] = (_dot(hi, wr_hi_ref[...]) + _dot(lo, wr_hi_ref[...]) + _dot(hi, wr_lo_ref[...])
                      + br_ref[...])


def _merge(oa, ob, ga, gb, x2, gate1, shift2, scale2, wa, wb, wo, g, b, wr_hi, wr_lo, br, seq):
    n, d = x2.shape
    tm = min(ROW_TILE, seq)
    per_b = seq // tm
    row = lambda i: (i, 0)
    bat = lambda i: (i // per_b, 0, 0)
    const = lambda i: (0, 0)
    return pl.pallas_call(
        _merge_kernel,
        grid=(n // tm,),
        in_specs=[pl.BlockSpec((tm, WIDTH_A), row), pl.BlockSpec((tm, WIDTH_B), row),
                  pl.BlockSpec((tm, d), row), pl.BlockSpec((tm, d), row), pl.BlockSpec((tm, d), row),
                  pl.BlockSpec((1, 1, d), bat), pl.BlockSpec((1, 1, d), bat), pl.BlockSpec((1, 1, d), bat),
                  pl.BlockSpec((WIDTH_A, d), const), pl.BlockSpec((WIDTH_B, d), const), pl.BlockSpec((d, d), const),
                  pl.BlockSpec((1, d), const), pl.BlockSpec((1, d), const),
                  pl.BlockSpec((d, LANES), const), pl.BlockSpec((d, LANES), const), pl.BlockSpec((1, LANES), const)],
        out_specs=[pl.BlockSpec((tm, d), row), pl.BlockSpec((tm, d), row), pl.BlockSpec((tm, LANES), row)],
        out_shape=[jax.ShapeDtypeStruct((n, d), f32), jax.ShapeDtypeStruct((n, d), bf16),
                   jax.ShapeDtypeStruct((n, LANES), f32)],
        compiler_params=pltpu.CompilerParams(dimension_semantics=("parallel",), vmem_limit_bytes=VMEM_LIMIT),
        name="merge",
    )(oa, ob, ga, gb, x2, gate1, shift2, scale2, wa, wb, wo, g, b, wr_hi, wr_lo, br)


def _expert_kernel(be_ref, nb_ref, x_ref, wgu_ref, bgu_ref, wdn_ref, bdn_ref, y_ref):
    i = pl.program_id(0)

    @pl.when(i < nb_ref[0])
    def _():
        gu = _dot(x_ref[...], wgu_ref[0]) + bgu_ref[0]
        a = jnp.minimum(gu[:, :D_EXPERT], SWIGLU_LIMIT)
        u = jnp.clip(gu[:, D_EXPERT:], -SWIGLU_LIMIT, SWIGLU_LIMIT)
        act = (u + 1.0) * a * jax.nn.sigmoid(SWIGLU_ALPHA * a)
        y_ref[...] = _dot(act.astype(bf16), wdn_ref[0]) + bdn_ref[0]

    @pl.when(i >= nb_ref[0])
    def _():
        y_ref[...] = jnp.zeros(y_ref.shape, f32)


def _experts(xs, block_expert, n_used, wgu, bgu, wdn, bdn):
    rows, d = xs.shape
    n_blocks = rows // MOE_ROWS
    e_idx = lambda i, be, nb: (be[i], 0, 0)
    grid_spec = pltpu.PrefetchScalarGridSpec(
        num_scalar_prefetch=2,
        grid=(n_blocks,),
        in_specs=[pl.BlockSpec((MOE_ROWS, d), lambda i, be, nb: (i, 0)),
                  pl.BlockSpec((1, d, 2 * D_EXPERT), e_idx),
                  pl.BlockSpec((1, 1, 2 * D_EXPERT), e_idx),
                  pl.BlockSpec((1, D_EXPERT, d), e_idx),
                  pl.BlockSpec((1, 1, d), e_idx)],
        out_specs=pl.BlockSpec((MOE_ROWS, d), lambda i, be, nb: (i, 0)),
    )
    return pl.pallas_call(
        _expert_kernel,
        grid_spec=grid_spec,
        out_shape=jax.ShapeDtypeStruct((rows, d), f32),
        compiler_params=pltpu.CompilerParams(dimension_semantics=("arbitrary",), vmem_limit_bytes=VMEM_LIMIT),
        name="experts",
    )(block_expert, n_used, xs, wgu, bgu, wdn, bdn)


def _combine_kernel(x_ref, y_ref, gk_ref, gate_ref, g_ref, b_ref, o_ref):
    gk = gk_ref[...]
    y = y_ref[0] * gk[:, 0:1]
    for k in range(1, TOP_K):
        y = y + y_ref[k] * gk[:, k:k + 1]
    o_ref[...] = _ln(DN_ALPHA * x_ref[...] + gate_ref[0] * y) * g_ref[...] + b_ref[...]


def _combine(x1, yk, gates, gate2, g, b, seq):
    n, d = x1.shape
    tm = min(ROW_TILE, seq)
    per_b = seq // tm
    row = lambda i: (i, 0)
    const = lambda i: (0, 0)
    return pl.pallas_call(
        _combine_kernel,
        grid=(n // tm,),
        in_specs=[pl.BlockSpec((tm, d), row),
                  pl.BlockSpec((TOP_K, tm, d), lambda i: (0, i, 0)),
                  pl.BlockSpec((tm, TOP_K), row),
                  pl.BlockSpec((1, 1, d), lambda i: (i // per_b, 0, 0)),
                  pl.BlockSpec((1, d), const), pl.BlockSpec((1, d), const)],
        out_specs=pl.BlockSpec((tm, d), row),
        out_shape=jax.ShapeDtypeStruct((n, d), f32),
        compiler_params=pltpu.CompilerParams(dimension_semantics=("parallel",), vmem_limit_bytes=VMEM_LIMIT),
        name="combine",
    )(x1, yk, gates, gate2, g, b)


def _route(logits, n):
    top_val, top_idx = lax.top_k(logits, TOP_K)
    gates = jax.nn.softmax(top_val, axis=-1)
    onehot = jnp.sum(jax.nn.one_hot(top_idx, N_EXPERTS, dtype=jnp.int32), axis=1)
    before = jnp.cumsum(onehot, axis=0) - onehot
    counts = before[-1] + onehot[-1]
    padded = (counts + MOE_ROWS - 1) // MOE_ROWS * MOE_ROWS
    pad_end = jnp.cumsum(padded)
    pad_start = pad_end - padded
    dest = pad_start[top_idx] + jnp.take_along_axis(before, top_idx, axis=1)
    n_blocks = -(-(n * TOP_K + N_EXPERTS * (MOE_ROWS - 1)) // MOE_ROWS)
    block_expert = jnp.minimum(
        jnp.searchsorted(pad_end, jnp.arange(n_blocks, dtype=jnp.int32) * MOE_ROWS, side='right'),
        N_EXPERTS - 1).astype(jnp.int32)
    n_used = (pad_end[-1:] // MOE_ROWS).astype(jnp.int32)
    return gates, dest.astype(jnp.int32), block_expert, n_used, n_blocks


def _pad_w_in(w):
    z = jnp.zeros((w.shape[0], IDX_PAD), w.dtype)
    split = sum(COL_SPLITS[:5])
    return jnp.concatenate([w[:, :split], z, w[:, split:]], axis=1).astype(bf16)


def kernel(x, c, rel_bias, w_ada, b_ada, w_in, g_kv, w_uk, w_uv, w_a_out, w_b_out, w_o, ln1_g, ln1_b,
           w_router, b_router, w_gu, b_gu, w_dn, b_dn, ln2_g, ln2_b):
    B, S, D = x.shape
    N = B * S
    assert D == D_MODEL and S % ATT_BLOCK == 0
    cond = jax.nn.silu(c)
    x2 = x.reshape(N, D)
    for l in range(DEPTH):
        mod = jnp.dot(cond, w_ada[l], precision=lax.Precision.HIGHEST) + b_ada[l]
        shift1, scale1, gate1, shift2, scale2, gate2 = [m[:, None, :] for m in jnp.split(mod, 6, axis=-1)]

        qa, ckv, qidx, kidx, widx, qb, kb, vb, ga, gb = _proj(
            x2, shift1, scale1, _pad_w_in(w_in[l]), g_kv[l][None, :], S)
        wuk_t = jnp.transpose(w_uk[l], (1, 2, 0)).astype(bf16)
        wuv_t = jnp.transpose(w_uv[l], (1, 0, 2)).astype(bf16)
        oa = _dsa(qa, qidx, widx, kidx, ckv, wuk_t, wuv_t, rel_bias, B, S)
        ob = _sb(qb, kb, vb, B, S)

        wr = jnp.pad(w_router[l], ((0, 0), (0, LANES - N_EXPERTS)))
        wr_hi, wr_lo = _split_bf16(wr)
        br = jnp.pad(b_router[l], (0, LANES - N_EXPERTS))[None, :]
        x1, h2, logits = _merge(oa, ob, ga, gb, x2, gate1, shift2, scale2,
                                w_a_out[l].astype(bf16), w_b_out[l].astype(bf16), w_o[l].astype(bf16),
                                ln1_g[l][None, :], ln1_b[l][None, :], wr_hi, wr_lo, br, S)

        gates, dest, block_expert, n_used, n_blocks = _route(logits[:, :N_EXPERTS], N)
        rows = n_blocks * MOE_ROWS
        row_token = jnp.full((rows,), N, jnp.int32).at[dest.reshape(-1)].set(
            jnp.repeat(jnp.arange(N, dtype=jnp.int32), TOP_K))
        h2_pad = jnp.concatenate([h2, jnp.zeros((1, D), h2.dtype)], axis=0)
        xs = h2_pad[row_token]
        ys = _experts(xs, block_expert, n_used, w_gu[l].astype(bf16), b_gu[l][:, None, :],
                      w_dn[l].astype(bf16), b_dn[l][:, None, :])
        yk = ys[dest.T]
        x2 = _combine(x1, yk, gates, gate2, ln2_g[l][None, :], ln2_b[l][None, :], S)
    return x2.reshape(B, S, D)
```

```python
import functools
import math

import numpy as np
import jax
import jax.numpy as jnp
from jax import lax
from jax.experimental import pallas as pl
from jax.experimental.pallas import tpu as pltpu

D_MODEL = 1024
HEAD_DIM = 64
N_HEADS_A = 8
WIDTH_A = N_HEADS_A * HEAD_DIM
KV_RANK = 128
N_IDX_HEADS = 4
IDX_DIM = 64
INDEX_TOPK = 256
N_HEADS_B = 8
WIDTH_B = N_HEADS_B * HEAD_DIM
N_BUCKETS = 32
MAX_DISTANCE = 128
N_EXPERTS = 32
TOP_K = 4
D_EXPERT = 1024
SWIGLU_LIMIT = 7.0
SWIGLU_ALPHA = 1.702
LN_EPS = 1e-5
RMS_EPS = 1e-6
DEPTH = 2
DN_ALPHA = (2 * DEPTH) ** 0.25
IDX_SCALE = (N_IDX_HEADS ** -0.5) * (IDX_DIM ** -0.5)
LOG2E = math.log2(math.e)

COL_SPLITS = (WIDTH_A, KV_RANK, N_IDX_HEADS * IDX_DIM, IDX_DIM, N_IDX_HEADS,
              WIDTH_B, WIDTH_B, WIDTH_B, D_MODEL, D_MODEL)
N_COLS = sum(COL_SPLITS)

LANES = 128
VMEM_LIMIT = 56 * 1024 * 1024

IDX_PAD = LANES - IDX_DIM - N_IDX_HEADS
C_QA = 0
C_CKV = C_QA + WIDTH_A
C_QIDX = C_CKV + KV_RANK
C_KW = C_QIDX + N_IDX_HEADS * IDX_DIM
C_QB = C_KW + LANES
C_KB = C_QB + WIDTH_B
C_VB = C_KB + WIDTH_B
C_GA = C_VB + WIDTH_B
C_GB = C_GA + D_MODEL
N_COLS_PAD = C_GB + D_MODEL

ROW_TILE = 512
ATT_BLOCK = 256
BISECT_ROWS = 128
MOE_ROWS = 256
NEG_MASK = -1e30
M_INIT = -1e29
SB_SKIP = -110.0
INT_MIN = np.int32(-2 ** 31)

f32 = jnp.float32
bf16 = jnp.bfloat16


def _ln(x):
    mu = jnp.mean(x, axis=-1, keepdims=True)
    xc = x - mu
    var = jnp.mean(xc * xc, axis=-1, keepdims=True)
    return xc * lax.rsqrt(var + LN_EPS)


def _dot(a, b):
    return jnp.dot(a, b, preferred_element_type=f32)


def _dot_nt(a, b):
    return lax.dot_general(a, b, (((1,), (1,)), ((), ())), preferred_element_type=f32)


def _split_bf16(x):
    hi = x.astype(bf16)
    lo = (x - hi.astype(f32)).astype(bf16)
    return hi, lo


def _proj_kernel(x_ref, shift_ref, scale_ref, w_ref, gkv_ref,
                 qa_ref, ckv_ref, qidx_ref, kidx_ref, widx_ref, qb_ref, kb_ref, vb_ref, ga_ref, gb_ref):
    h = _ln(x_ref[...]) * (1.0 + scale_ref[0]) + shift_ref[0]
    hb = h.astype(bf16)

    def mm(c0, width):
        return _dot(hb, w_ref[:, c0:c0 + width])

    qa_ref[...] = mm(C_QA, WIDTH_A).astype(bf16)
    ckv = mm(C_CKV, KV_RANK)
    ckv = ckv * lax.rsqrt(jnp.mean(ckv * ckv, axis=-1, keepdims=True) + RMS_EPS) * gkv_ref[...]
    ckv_ref[...] = jnp.concatenate([ckv, jnp.ones_like(ckv)], axis=1).astype(bf16)
    qidx_ref[...] = mm(C_QIDX, N_IDX_HEADS * IDX_DIM).astype(bf16)
    kw = mm(C_KW, LANES)
    kidx_ref[...] = kw[:, :IDX_DIM].astype(bf16)
    widx_ref[...] = kw * IDX_SCALE
    qb_ref[...] = mm(C_QB, WIDTH_B).astype(bf16)
    kb_ref[...] = mm(C_KB, WIDTH_B).astype(bf16)
    vb_ref[...] = mm(C_VB, WIDTH_B).astype(bf16)
    for c in range(0, D_MODEL, 512):
        ga_ref[:, c:c + 512] = mm(C_GA + c, 512).astype(bf16)
        gb_ref[:, c:c + 512] = mm(C_GB + c, 512).astype(bf16)


def _proj(x2, shift, scale, w_pad, g_kv, seq):
    n, d = x2.shape
    tm = min(ROW_TILE, seq)
    per_b = seq // tm
    row = lambda i: (i, 0)
    bat = lambda i: (i // per_b, 0, 0)
    const = lambda i: (0, 0)
    widths = (WIDTH_A, 2 * KV_RANK, N_IDX_HEADS * IDX_DIM, IDX_DIM, LANES, WIDTH_B, WIDTH_B, WIDTH_B, D_MODEL, D_MODEL)
    dtypes = (bf16, bf16, bf16, bf16, f32, bf16, bf16, bf16, bf16, bf16)
    return pl.pallas_call(
        _proj_kernel,
        grid=(n // tm,),
        in_specs=[pl.BlockSpec((tm, d), row),
                  pl.BlockSpec((1, 1, d), bat),
                  pl.BlockSpec((1, 1, d), bat),
                  pl.BlockSpec((d, N_COLS_PAD), const),
                  pl.BlockSpec((1, KV_RANK), const)],
        out_specs=[pl.BlockSpec((tm, w), row) for w in widths],
        out_shape=[jax.ShapeDtypeStruct((n, w), dt) for w, dt in zip(widths, dtypes)],
        compiler_params=pltpu.CompilerParams(dimension_semantics=("parallel",), vmem_limit_bytes=VMEM_LIMIT),
        name="proj",
    )(x2, shift, scale, w_pad, g_kv)


def _dsa_kernel(qa_ref, qidx_ref, widx_ref, kidx_ref, ckv_ref, wuk_ref, wuv_ref, bias_ref, tri_ref,
                o_ref, key_ref, qlat_ref, wrep_ref, thr_ref, need_ref, ceq_ref, mb_ref, m_ref, acc_ref, *, topk):
    tq = sk = ATT_BLOCK
    i = pl.program_id(1)
    n_blocks = i + 1

    for h in range(N_HEADS_A):
        ql = _dot(qa_ref[:, h * HEAD_DIM:(h + 1) * HEAD_DIM], wuk_ref[h]) * (HEAD_DIM ** -0.5 * LOG2E)
        qlat_ref[h] = ql.astype(bf16)

    rowi = lax.broadcasted_iota(jnp.int32, (tq, sk), 0)
    coli = lax.broadcasted_iota(jnp.int32, (tq, sk), 1)
    w = widx_ref[:, IDX_DIM:IDX_DIM + N_IDX_HEADS]
    for h in range(N_IDX_HEADS):
        wrep_ref[h] = jnp.broadcast_to(w[:, h:h + 1], (tq, LANES))

    def score_block(j, carry):
        kblk = kidx_ref[pl.ds(pl.multiple_of(j * sk, sk), sk), :]
        s = None
        for h in range(N_IDX_HEADS):
            d = _dot_nt(qidx_ref[:, h * IDX_DIM:(h + 1) * IDX_DIM], kblk)
            t = jnp.maximum(d, 0.0) * jnp.concatenate([wrep_ref[h]] * (sk // LANES), axis=1)
            s = t if s is None else s + t
        s = jnp.where((j - i) * sk + coli <= rowi, s, -jnp.inf)
        bits = pltpu.bitcast(s, jnp.int32)
        key_ref[j] = bits ^ ((bits >> 31) & np.int32(0x7FFFFFFF))
        return carry

    lax.fori_loop(0, n_blocks, score_block, 0)

    key_ref[n_blocks] = jnp.full((tq, sk), INT_MIN, jnp.int32)

    for r0 in range(0, tq, BISECT_ROWS):
        def count_ge(cand):
            cb = jnp.broadcast_to(cand, (BISECT_ROWS, LANES))

            def body(p, acc):
                for u in range(2):
                    for c in range(0, sk, LANES):
                        kk = key_ref[2 * p + u, r0:r0 + BISECT_ROWS, c:c + LANES]
                        acc = acc + jnp.where(kk >= cb, 1.0, 0.0)
                return acc

            acc = lax.fori_loop(0, (n_blocks + 1) // 2, body, jnp.zeros((BISECT_ROWS, LANES), f32))
            return jnp.sum(acc, axis=1, keepdims=True)

        def bit_step(b, ans):
            cand = ans + jnp.left_shift(jnp.int32(1), 31 - b)
            return jnp.where(count_ge(cand) >= float(topk), cand, ans)

        thr = lax.fori_loop(0, 32, bit_step, jnp.full((BISECT_ROWS, 1), INT_MIN, jnp.int32))
        n_gt = count_ge(thr + 1)
        thr_ref[r0:r0 + BISECT_ROWS, :] = jnp.broadcast_to(thr, (BISECT_ROWS, LANES))
        need_ref[r0:r0 + BISECT_ROWS, :] = jnp.broadcast_to(float(topk) - n_gt, (BISECT_ROWS, LANES))

    m_ref[...] = jnp.full(m_ref.shape, M_INIT, f32)
    acc_ref[...] = jnp.zeros(acc_ref.shape, f32)
    ceq_ref[...] = jnp.zeros(ceq_ref.shape, f32)

    def key_rows(j):
        return pl.ds(pl.multiple_of(j * sk, sk), sk)

    def mask_block(j, slot, diag):
        thr, need, ceq = thr_ref[...], need_ref[...], ceq_ref[...]
        keys = [key_ref[j, :, c:c + LANES] for c in range(0, sk, LANES)]
        eqs = [k == thr for k in keys]
        eqf = jnp.concatenate([jnp.where(e, 1.0, 0.0) for e in eqs], axis=1).astype(bf16)
        pref = _dot(eqf, tri_ref[...])
        total = jnp.broadcast_to(pref[:, sk - 1:sk], (tq, LANES))
        for n, (k, e) in enumerate(zip(keys, eqs)):
            c = n * LANES
            sel = jnp.logical_or(k > thr, jnp.logical_and(e, pref[:, c:c + LANES] + ceq <= need))
            if diag:
                ri = lax.broadcasted_iota(jnp.int32, (tq, LANES), 0)
                ci = lax.broadcasted_iota(jnp.int32, (tq, LANES), 1)
                sel = jnp.logical_and(sel, ci + c <= ri)
            mb_ref[:, slot * sk + c:slot * sk + c + LANES] = jnp.where(sel, 0.0, NEG_MASK)
        ceq_ref[...] = ceq + total

    def attend(j0, n_blk, near_col):
        for b in range(n_blk):
            mask_block(j0 + b, b, near_col is not None and b == n_blk - 1)
        heads, blks = range(N_HEADS_A), range(n_blk)
        logits = []
        for h in heads:
            q = qlat_ref[h]
            parts = []
            for b in blks:
                lg = _dot_nt(q, ckv_ref[key_rows(j0 + b), :KV_RANK]) + mb_ref[:, b * sk:(b + 1) * sk]
                if near_col is not None:
                    lg = lg + bias_ref[h, :, near_col + b * sk:near_col + (b + 1) * sk]
                parts.append(lg)
            logits.append(parts)
        m_olds = [m_ref[h] for h in heads]
        m_news = []
        for h in heads:
            mx = functools.reduce(jnp.maximum, [p[:, c:c + LANES] for p in logits[h] for c in range(0, sk, LANES)])
            m_news.append(jnp.maximum(m_olds[h], jnp.max(mx, axis=1, keepdims=True)))
        probs = []
        for h in heads:
            m_wide = jnp.concatenate([m_news[h]] * (sk // LANES), axis=1)
            probs.append([jnp.exp2(logits[h][b] - m_wide).astype(bf16) for b in blks])
        for h in heads:
            pv = None
            for b in blks:
                d = _dot(probs[h][b], ckv_ref[key_rows(j0 + b), :])
                pv = d if pv is None else pv + d
            alpha = jnp.exp2(m_olds[h] - m_news[h])
            acc_ref[h] = jnp.concatenate([alpha, alpha], axis=1) * acc_ref[h] + pv
            m_ref[h] = m_news[h]

    lone = jnp.logical_and(i >= 2, i % 2 == 0)

    @pl.when(lone)
    def _():
        attend(0, 1, None)

    first = jnp.where(lone, 1, 0)

    def far_pair(p, carry):
        attend(first + 2 * p, 2, None)
        return carry

    lax.fori_loop(0, (i - 1 - first) // 2, far_pair, 0)

    @pl.when(i >= 1)
    def _():
        attend(i - 1, 2, 0)

    @pl.when(i == 0)
    def _():
        attend(0, 1, sk)

    outs = []
    for h in range(N_HEADS_A):
        acc = acc_ref[h]
        o_lat = acc[:, :KV_RANK] / acc[:, KV_RANK:]
        outs.append(_dot(o_lat.astype(bf16), wuv_ref[h]))
    o_ref[...] = jnp.concatenate(outs, axis=1).astype(bf16)


def _t5_bucket(n):
    max_exact = N_BUCKETS // 2
    nf = jnp.maximum(n, 1).astype(f32)
    large = max_exact + (jnp.log(nf / max_exact) / math.log(MAX_DISTANCE / max_exact)
                         * (N_BUCKETS - max_exact)).astype(jnp.int32)
    large = jnp.minimum(large, N_BUCKETS - 1)
    return jnp.where(n < max_exact, n, large)


def _dsa(qa, qidx, widx, kidx, ckv, wuk_t, wuv_t, rel_bias, batch, seq):
    n = qa.shape[0]
    tq = ATT_BLOCK
    nq = seq // tq
    topk = min(INDEX_TOPK, seq // 4)
    dist = tq + jnp.arange(tq)[:, None] - jnp.arange(2 * tq)[None, :]
    bias_near = jnp.moveaxis(rel_bias[_t5_bucket(jnp.maximum(dist, 0))], -1, 0).astype(f32)
    far_n = np.float32(tq + 1)
    assert 16 + int(np.log(far_n / 16) / math.log(MAX_DISTANCE / 16) * 16) >= N_BUCKETS - 1
    bias_near = (bias_near - rel_bias[N_BUCKETS - 1].astype(f32)[:, None, None]) * LOG2E
    tri = (jnp.arange(tq)[:, None] <= jnp.arange(tq)[None, :]).astype(bf16)

    qrow = lambda b, i: (b * nq + i, 0)
    full = lambda b, i: (b, 0)
    c3 = lambda b, i: (0, 0, 0)
    c2 = lambda b, i: (0, 0)
    kern = functools.partial(_dsa_kernel, topk=topk)
    return pl.pallas_call(
        kern,
        grid=(batch, nq),
        in_specs=[pl.BlockSpec((tq, WIDTH_A), qrow),
                  pl.BlockSpec((tq, N_IDX_HEADS * IDX_DIM), qrow),
                  pl.BlockSpec((tq, LANES), qrow),
                  pl.BlockSpec((seq, IDX_DIM), full),
                  pl.BlockSpec((seq, 2 * KV_RANK), full),
                  pl.BlockSpec((N_HEADS_A, HEAD_DIM, KV_RANK), c3),
                  pl.BlockSpec((N_HEADS_A, KV_RANK, HEAD_DIM), c3),
                  pl.BlockSpec((N_HEADS_A, tq, 2 * tq), c3),
                  pl.BlockSpec((tq, tq), c2)],
        out_specs=pl.BlockSpec((tq, WIDTH_A), qrow),
        out_shape=jax.ShapeDtypeStruct((n, WIDTH_A), bf16),
        scratch_shapes=[pltpu.VMEM((nq + 1, tq, tq), jnp.int32),
                        pltpu.VMEM((N_HEADS_A, tq, KV_RANK), bf16),
                        pltpu.VMEM((N_IDX_HEADS, tq, LANES), f32),
                        pltpu.VMEM((tq, LANES), jnp.int32),
                        pltpu.VMEM((tq, LANES), f32),
                        pltpu.VMEM((tq, LANES), f32),
                        pltpu.VMEM((tq, 2 * tq), f32),
                        pltpu.VMEM((N_HEADS_A, tq, LANES), f32),
                        pltpu.VMEM((N_HEADS_A, tq, 2 * KV_RANK), f32)],
        compiler_params=pltpu.CompilerParams(dimension_semantics=("parallel", "arbitrary"),
                                             vmem_limit_bytes=VMEM_LIMIT),
        name="dsa",
    )(qa, qidx, widx, kidx, ckv, wuk_t, wuv_t, bias_near, tri)


def _sb_kernel(q_ref, k_ref, v_ref, tri_ref, o_ref, carry_ref, acc_ref):
    tq = sk = ATT_BLOCK
    i = pl.program_id(1)
    carry_ref[...] = jnp.zeros(carry_ref.shape, f32)
    acc_ref[...] = jnp.zeros(acc_ref.shape, f32)

    def step(j, diag):
        rows = pl.ds(pl.multiple_of(j * sk, sk), sk)
        if diag:
            strict = lax.broadcasted_iota(jnp.int32, (tq, sk), 1) < lax.broadcasted_iota(jnp.int32, (tq, sk), 0)
        heads = range(N_HEADS_B)
        lanes = [slice(h * HEAD_DIM, (h + 1) * HEAD_DIM) for h in heads]
        zs = [_dot_nt(q_ref[:, lanes[h]] * (HEAD_DIM ** -0.5), k_ref[rows, lanes[h]]) for h in heads]
        lfs = [-(jnp.maximum(z, 0.0) + jnp.log(1.0 + jnp.exp(-jnp.abs(z)))) for z in zs]
        if diag:
            lfs = [jnp.where(strict, lf, 0.0) for lf in lfs]
        splits = [_split_bf16(lf) for lf in lfs]
        laters = [_dot(hi, tri_ref[...]) + _dot(lo, tri_ref[...]) for hi, lo in splits]
        carries = [carry_ref[h] for h in heads]
        probs = [jnp.exp(zs[h] + lfs[h] + laters[h] + jnp.concatenate([carries[h]] * (sk // LANES), axis=1))
                 for h in heads]
        if diag:
            probs = [jnp.where(strict, a, 0.0) for a in probs]
        top = None
        for h in heads:
            acc_ref[h] += _dot(probs[h].astype(bf16), v_ref[rows, lanes[h]])
            carry = carries[h] + jnp.sum(lfs[h], axis=1, keepdims=True)
            carry_ref[h] = carry
            top = carry if top is None else jnp.maximum(top, carry)
        return jnp.max(top)

    def cond(st):
        j, top = st
        return jnp.logical_and(j >= 0, top > SB_SKIP)

    def body(st):
        j, _ = st
        return j - 1, step(j, False)

    lax.while_loop(cond, body, (i - 1, step(i, True)))
    o_ref[...] = jnp.concatenate([acc_ref[h] for h in range(N_HEADS_B)], axis=1).astype(bf16)


def _sb(qb, kb, vb, batch, seq):
    n = qb.shape[0]
    tq = ATT_BLOCK
    nq = seq // tq
    tri = (jnp.arange(tq)[:, None] > jnp.arange(tq)[None, :]).astype(bf16)
    qrow = lambda b, i: (b * nq + i, 0)
    full = lambda b, i: (b, 0)
    return pl.pallas_call(
        _sb_kernel,
        grid=(batch, nq),
        in_specs=[pl.BlockSpec((tq, WIDTH_B), qrow),
                  pl.BlockSpec((seq, WIDTH_B), full),
                  pl.BlockSpec((seq, WIDTH_B), full),
                  pl.BlockSpec((tq, tq), lambda b, i: (0, 0))],
        out_specs=pl.BlockSpec((tq, WIDTH_B), qrow),
        out_shape=jax.ShapeDtypeStruct((n, WIDTH_B), bf16),
        scratch_shapes=[pltpu.VMEM((N_HEADS_B, tq, LANES), f32),
                        pltpu.VMEM((N_HEADS_B, tq, HEAD_DIM), f32)],
        compiler_params=pltpu.CompilerParams(dimension_semantics=("parallel", "arbitrary"),
                                             vmem_limit_bytes=VMEM_LIMIT),
        name="sb",
    )(qb, kb, vb, tri)


def _merge_kernel(oa_ref, ob_ref, ga_ref, gb_ref, x_ref, gate_ref, shift_ref, scale_ref,
                  wa_ref, wb_ref, wo_ref, g_ref, b_ref, wr_hi_ref, wr_lo_ref, br_ref,
                  x1_ref, h2_ref, logit_ref):
    ya = _dot(oa_ref[...], wa_ref[...])
    yb = _dot(ob_ref[...], wb_ref[...])
    merged = jax.nn.sigmoid(ga_ref[...].astype(f32)) * ya + jax.nn.sigmoid(gb_ref[...].astype(f32)) * yb
    y = _dot(merged.astype(bf16), wo_ref[...])
    x1 = _ln(DN_ALPHA * x_ref[...] + gate_ref[0] * y) * g_ref[...] + b_ref[...]
    x1_ref[...] = x1
    h2 = _ln(x1) * (1.0 + scale_ref[0]) + shift_ref[0]
    hi, lo = _split_bf16(h2)
    h2_ref[...] = hi
    logit_ref[...] = (_dot(hi, wr_hi_ref[...]) + _dot(lo, wr_hi_ref[...]) + _dot(hi, wr_lo_ref[...])
                      + br_ref[...])


def _merge(oa, ob, ga, gb, x2, gate1, shift2, scale2, wa, wb, wo, g, b, wr_hi, wr_lo, br, seq):
    n, d = x2.shape
    tm = min(ROW_TILE, seq)
    per_b = seq // tm
    row = lambda i: (i, 0)
    bat = lambda i: (i // per_b, 0, 0)
    const = lambda i: (0, 0)
    return pl.pallas_call(
        _merge_kernel,
        grid=(n // tm,),
        in_specs=[pl.BlockSpec((tm, WIDTH_A), row), pl.BlockSpec((tm, WIDTH_B), row),
                  pl.BlockSpec((tm, d), row), pl.BlockSpec((tm, d), row), pl.BlockSpec((tm, d), row),
                  pl.BlockSpec((1, 1, d), bat), pl.BlockSpec((1, 1, d), bat), pl.BlockSpec((1, 1, d), bat),
                  pl.BlockSpec((WIDTH_A, d), const), pl.BlockSpec((WIDTH_B, d), const), pl.BlockSpec((d, d), const),
                  pl.BlockSpec((1, d), const), pl.BlockSpec((1, d), const),
                  pl.BlockSpec((d, LANES), const), pl.BlockSpec((d, LANES), const), pl.BlockSpec((1, LANES), const)],
        out_specs=[pl.BlockSpec((tm, d), row), pl.BlockSpec((tm, d), row), pl.BlockSpec((tm, LANES), row)],
        out_shape=[jax.ShapeDtypeStruct((n, d), f32), jax.ShapeDtypeStruct((n, d), bf16),
                   jax.ShapeDtypeStruct((n, LANES), f32)],
        compiler_params=pltpu.CompilerParams(dimension_semantics=("parallel",), vmem_limit_bytes=VMEM_LIMIT),
        name="merge",
    )(oa, ob, ga, gb, x2, gate1, shift2, scale2, wa, wb, wo, g, b, wr_hi, wr_lo, br)


def _expert_kernel(be_ref, nb_ref, x_ref, wgu_ref, bgu_ref, wdn_ref, bdn_ref, y_ref):
    i = pl.program_id(0)

    @pl.when(i < nb_ref[0])
    def _():
        gu = _dot(x_ref[...], wgu_ref[0]) + bgu_ref[0]
        a = jnp.minimum(gu[:, :D_EXPERT], SWIGLU_LIMIT)
        u = jnp.clip(gu[:, D_EXPERT:], -SWIGLU_LIMIT, SWIGLU_LIMIT)
        act = (u + 1.0) * a * jax.nn.sigmoid(SWIGLU_ALPHA * a)
        y_ref[...] = _dot(act.astype(bf16), wdn_ref[0]) + bdn_ref[0]

    @pl.when(i >= nb_ref[0])
    def _():
        y_ref[...] = jnp.zeros(y_ref.shape, f32)


def _experts(xs, block_expert, n_used, wgu, bgu, wdn, bdn):
    rows, d = xs.shape
    n_blocks = rows // MOE_ROWS
    e_idx = lambda i, be, nb: (be[i], 0, 0)
    grid_spec = pltpu.PrefetchScalarGridSpec(
        num_scalar_prefetch=2,
        grid=(n_blocks,),
        in_specs=[pl.BlockSpec((MOE_ROWS, d), lambda i, be, nb: (i, 0)),
                  pl.BlockSpec((1, d, 2 * D_EXPERT), e_idx),
                  pl.BlockSpec((1, 1, 2 * D_EXPERT), e_idx),
                  pl.BlockSpec((1, D_EXPERT, d), e_idx),
                  pl.BlockSpec((1, 1, d), e_idx)],
        out_specs=pl.BlockSpec((MOE_ROWS, d), lambda i, be, nb: (i, 0)),
    )
    return pl.pallas_call(
        _expert_kernel,
        grid_spec=grid_spec,
        out_shape=jax.ShapeDtypeStruct((rows, d), f32),
        compiler_params=pltpu.CompilerParams(dimension_semantics=("arbitrary",), vmem_limit_bytes=VMEM_LIMIT),
        name="experts",
    )(block_expert, n_used, xs, wgu, bgu, wdn, bdn)


def _combine_kernel(x_ref, y_ref, gk_ref, gate_ref, g_ref, b_ref, o_ref):
    gk = gk_ref[...]
    y = y_ref[0] * gk[:, 0:1]
    for k in range(1, TOP_K):
        y = y + y_ref[k] * gk[:, k:k + 1]
    o_ref[...] = _ln(DN_ALPHA * x_ref[...] + gate_ref[0] * y) * g_ref[...] + b_ref[...]


def _combine(x1, yk, gates, gate2, g, b, seq):
    n, d = x1.shape
    tm = min(ROW_TILE, seq)
    per_b = seq // tm
    row = lambda i: (i, 0)
    const = lambda i: (0, 0)
    return pl.pallas_call(
        _combine_kernel,
        grid=(n // tm,),
        in_specs=[pl.BlockSpec((tm, d), row),
                  pl.BlockSpec((TOP_K, tm, d), lambda i: (0, i, 0)),
                  pl.BlockSpec((tm, TOP_K), row),
                  pl.BlockSpec((1, 1, d), lambda i: (i // per_b, 0, 0)),
                  pl.BlockSpec((1, d), const), pl.BlockSpec((1, d), const)],
        out_specs=pl.BlockSpec((tm, d), row),
        out_shape=jax.ShapeDtypeStruct((n, d), f32),
        compiler_params=pltpu.CompilerParams(dimension_semantics=("parallel",), vmem_limit_bytes=VMEM_LIMIT),
        name="combine",
    )(x1, yk, gates, gate2, g, b)


def _route(logits, n):
    top_val, top_idx = lax.top_k(logits, TOP_K)
    gates = jax.nn.softmax(top_val, axis=-1)
    onehot = jnp.sum(jax.nn.one_hot(top_idx, N_EXPERTS, dtype=jnp.int32), axis=1)
    before = jnp.cumsum(onehot, axis=0) - onehot
    counts = before[-1] + onehot[-1]
    padded = (counts + MOE_ROWS - 1) // MOE_ROWS * MOE_ROWS
    pad_end = jnp.cumsum(padded)
    pad_start = pad_end - padded
    dest = pad_start[top_idx] + jnp.take_along_axis(before, top_idx, axis=1)
    n_blocks = -(-(n * TOP_K + N_EXPERTS * (MOE_ROWS - 1)) // MOE_ROWS)
    block_expert = jnp.minimum(
        jnp.searchsorted(pad_end, jnp.arange(n_blocks, dtype=jnp.int32) * MOE_ROWS, side='right'),
        N_EXPERTS - 1).astype(jnp.int32)
    n_used = (pad_end[-1:] // MOE_ROWS).astype(jnp.int32)
    return gates, dest.astype(jnp.int32), block_expert, n_used, n_blocks


def _pad_w_in(w):
    z = jnp.zeros((w.shape[0], IDX_PAD), w.dtype)
    split = sum(COL_SPLITS[:5])
    return jnp.concatenate([w[:, :split], z, w[:, split:]], axis=1).astype(bf16)


def kernel(x, c, rel_bias, w_ada, b_ada, w_in, g_kv, w_uk, w_uv, w_a_out, w_b_out, w_o, ln1_g, ln1_b,
           w_router, b_router, w_gu, b_gu, w_dn, b_dn, ln2_g, ln2_b):
    B, S, D = x.shape
    N = B * S
    assert D == D_MODEL and S % ATT_BLOCK == 0
    cond = jax.nn.silu(c)
    x2 = x.reshape(N, D)
    for l in range(DEPTH):
        mod = jnp.dot(cond, w_ada[l], precision=lax.Precision.HIGHEST) + b_ada[l]
        shift1, scale1, gate1, shift2, scale2, gate2 = [m[:, None, :] for m in jnp.split(mod, 6, axis=-1)]

        qa, ckv, qidx, kidx, widx, qb, kb, vb, ga, gb = _proj(
            x2, shift1, scale1, _pad_w_in(w_in[l]), g_kv[l][None, :], S)
        wuk_t = jnp.transpose(w_uk[l], (1, 2, 0)).astype(bf16)
        wuv_t = jnp.transpose(w_uv[l], (1, 0, 2)).astype(bf16)
        oa = _dsa(qa, qidx, widx, kidx, ckv, wuk_t, wuv_t, rel_bias, B, S)
        ob = _sb(qb, kb, vb, B, S)

        wr = jnp.pad(w_router[l], ((0, 0), (0, LANES - N_EXPERTS)))
        wr_hi, wr_lo = _split_bf16(wr)
        br = jnp.pad(b_router[l], (0, LANES - N_EXPERTS))[None, :]
        x1, h2, logits = _merge(oa, ob, ga, gb, x2, gate1, shift2, scale2,
                                w_a_out[l].astype(bf16), w_b_out[l].astype(bf16), w_o[l].astype(bf16),
                                ln1_g[l][None, :], ln1_b[l][None, :], wr_hi, wr_lo, br, S)

        gates, dest, block_expert, n_used, n_blocks = _route(logits[:, :N_EXPERTS], N)
        rows = n_blocks * MOE_ROWS
        row_token = jnp.full((rows,), N, jnp.int32).at[dest.reshape(-1)].set(
            jnp.repeat(jnp.arange(N, dtype=jnp.int32), TOP_K))
        h2_pad = jnp.concatenate([h2, jnp.zeros((1, D), h2.dtype)], axis=0)
        xs = h2_pad[row_token]
        ys = _experts(xs, block_expert, n_used, w_gu[l].astype(bf16), b_gu[l][:, None, :],
                      w_dn[l].astype(bf16), b_dn[l][:, None, :])
        yk = ys[dest.T]
        x2 = _combine(x1, yk, gates, gate2, ln2_g[l][None, :], ln2_b[l][None, :], S)
    return x2.reshape(B, S, D)
```

```python
import functools
import math

import numpy as np
import jax
import jax.numpy as jnp
from jax import lax
from jax.experimental import pallas as pl
from jax.experimental.pallas import tpu as pltpu

D_MODEL = 1024
HEAD_DIM = 64
N_HEADS_A = 8
WIDTH_A = N_HEADS_A * HEAD_DIM
KV_RANK = 128
N_IDX_HEADS = 4
IDX_DIM = 64
INDEX_TOPK = 256
N_HEADS_B = 8
WIDTH_B = N_HEADS_B * HEAD_DIM
N_BUCKETS = 32
MAX_DISTANCE = 128
N_EXPERTS = 32
TOP_K = 4
D_EXPERT = 1024
SWIGLU_LIMIT = 7.0
SWIGLU_ALPHA = 1.702
LN_EPS = 1e-5
RMS_EPS = 1e-6
DEPTH = 2
DN_ALPHA = (2 * DEPTH) ** 0.25
IDX_SCALE = (N_IDX_HEADS ** -0.5) * (IDX_DIM ** -0.5)
LOG2E = math.log2(math.e)

COL_SPLITS = (WIDTH_A, KV_RANK, N_IDX_HEADS * IDX_DIM, IDX_DIM, N_IDX_HEADS,
              WIDTH_B, WIDTH_B, WIDTH_B, D_MODEL, D_MODEL)
N_COLS = sum(COL_SPLITS)

LANES = 128
VMEM_LIMIT = 56 * 1024 * 1024

IDX_PAD = LANES - IDX_DIM - N_IDX_HEADS
C_QA = 0
C_CKV = C_QA + WIDTH_A
C_QIDX = C_CKV + KV_RANK
C_KW = C_QIDX + N_IDX_HEADS * IDX_DIM
C_QB = C_KW + LANES
C_KB = C_QB + WIDTH_B
C_VB = C_KB + WIDTH_B
C_GA = C_VB + WIDTH_B
C_GB = C_GA + D_MODEL
N_COLS_PAD = C_GB + D_MODEL

ROW_TILE = 512
ATT_BLOCK = 256
BISECT_ROWS = 128
MOE_ROWS = 256
NEG_MASK = -1e30
M_INIT = -1e29
SB_SKIP = -110.0
INT_MIN = np.int32(-2 ** 31)

f32 = jnp.float32
bf16 = jnp.bfloat16


def _ln(x):
    mu = jnp.mean(x, axis=-1, keepdims=True)
    xc = x - mu
    var = jnp.mean(xc * xc, axis=-1, keepdims=True)
    return xc * lax.rsqrt(var + LN_EPS)


def _dot(a, b):
    return jnp.dot(a, b, preferred_element_type=f32)


def _dot_nt(a, b):
    return lax.dot_general(a, b, (((1,), (1,)), ((), ())), preferred_element_type=f32)


def _split_bf16(x):
    hi = x.astype(bf16)
    lo = (x - hi.astype(f32)).astype(bf16)
    return hi, lo


def _proj_kernel(x_ref, shift_ref, scale_ref, wa_ref, wb_ref, gkv_ref,
                 qa_ref, ckv_ref, qidx_ref, kidx_ref, widx_ref, qb_ref, kb_ref, vb_ref, ga_ref, gb_ref):
    h = _ln(x_ref[...]) * (1.0 + scale_ref[0]) + shift_ref[0]
    hb = h.astype(bf16)

    def mm(c0, width):
        if c0 < C_QB:
            return _dot(hb, wa_ref[:, c0:c0 + width])
        return _dot(hb, wb_ref[:, c0 - C_QB:c0 - C_QB + width])

    qa_ref[...] = mm(C_QA, WIDTH_A).astype(bf16)
    ckv = mm(C_CKV, KV_RANK)
    ckv = ckv * lax.rsqrt(jnp.mean(ckv * ckv, axis=-1, keepdims=True) + RMS_EPS) * gkv_ref[...]
    ckv_ref[...] = jnp.concatenate([ckv, jnp.ones_like(ckv)], axis=1).astype(bf16)
    qidx_ref[...] = mm(C_QIDX, N_IDX_HEADS * IDX_DIM).astype(bf16)
    kw = mm(C_KW, LANES)
    kidx_ref[...] = kw[:, :IDX_DIM].astype(bf16)
    widx_ref[...] = kw * IDX_SCALE
    qb_ref[...] = mm(C_QB, WIDTH_B).astype(bf16)
    kb_ref[...] = mm(C_KB, WIDTH_B).astype(bf16)
    vb_ref[...] = mm(C_VB, WIDTH_B).astype(bf16)
    for c in range(0, D_MODEL, 512):
        ga_ref[:, c:c + 512] = mm(C_GA + c, 512).astype(bf16)
        gb_ref[:, c:c + 512] = mm(C_GB + c, 512).astype(bf16)


def _proj(x2, shift, scale, w_dsa, w_rest, g_kv, seq):
    n, d = x2.shape
    tm = min(ROW_TILE, seq)
    per_b = seq // tm
    row = lambda i: (i, 0)
    bat = lambda i: (i // per_b, 0, 0)
    const = lambda i: (0, 0)
    widths = (WIDTH_A, 2 * KV_RANK, N_IDX_HEADS * IDX_DIM, IDX_DIM, LANES, WIDTH_B, WIDTH_B, WIDTH_B, D_MODEL, D_MODEL)
    dtypes = (bf16, bf16, bf16, bf16, f32, bf16, bf16, bf16, bf16, bf16)
    return pl.pallas_call(
        _proj_kernel,
        grid=(n // tm,),
        in_specs=[pl.BlockSpec((tm, d), row),
                  pl.BlockSpec((1, 1, d), bat),
                  pl.BlockSpec((1, 1, d), bat),
                  pl.BlockSpec((d, C_QB), const),
                  pl.BlockSpec((d, N_COLS_PAD - C_QB), const),
                  pl.BlockSpec((1, KV_RANK), const)],
        out_specs=[pl.BlockSpec((tm, w), row) for w in widths],
        out_shape=[jax.ShapeDtypeStruct((n, w), dt) for w, dt in zip(widths, dtypes)],
        compiler_params=pltpu.CompilerParams(dimension_semantics=("parallel",), vmem_limit_bytes=VMEM_LIMIT),
        name="proj",
    )(x2, shift, scale, w_dsa, w_rest, g_kv)


def _dsa_kernel(qa_ref, qidx_ref, widx_ref, kidx_ref, ckv_ref, wuk_ref, wuv_ref, bias_ref, tri_ref,
                o_ref, key_ref, qlat_ref, wrep_ref, thr_ref, need_ref, ceq_ref, mb_ref, m_ref, acc_ref, *, topk):
    tq = sk = ATT_BLOCK
    i = pl.program_id(1)
    n_blocks = i + 1

    for h in range(N_HEADS_A):
        ql = _dot(qa_ref[:, h * HEAD_DIM:(h + 1) * HEAD_DIM], wuk_ref[h]) * (HEAD_DIM ** -0.5 * LOG2E)
        qlat_ref[h] = ql.astype(bf16)

    rowi = lax.broadcasted_iota(jnp.int32, (tq, sk), 0)
    coli = lax.broadcasted_iota(jnp.int32, (tq, sk), 1)
    w = widx_ref[:, IDX_DIM:IDX_DIM + N_IDX_HEADS]
    for h in range(N_IDX_HEADS):
        wrep_ref[h] = jnp.broadcast_to(w[:, h:h + 1], (tq, LANES))

    def score_pair(p, carry):
        for u in range(2):
            j = 2 * p + u
            kblk = kidx_ref[pl.ds(pl.multiple_of(jnp.minimum(j, i) * sk, sk), sk), :]
            s = None
            for h in range(N_IDX_HEADS):
                d = _dot_nt(qidx_ref[:, h * IDX_DIM:(h + 1) * IDX_DIM], kblk)
                t = jnp.maximum(d, 0.0) * jnp.concatenate([wrep_ref[h]] * (sk // LANES), axis=1)
                s = t if s is None else s + t
            s = jnp.where(s == 0.0, 0.0, s)
            s = jnp.where((j - i) * sk + coli <= rowi, s, -jnp.inf)
            bits = pltpu.bitcast(s, jnp.int32)
            key_ref[j] = bits ^ ((bits >> 31) & np.int32(0x7FFFFFFF))
        return carry

    lax.fori_loop(0, (n_blocks + 1) // 2, score_pair, 0)
    key_ref[n_blocks] = jnp.full((tq, sk), INT_MIN, jnp.int32)

    for r0 in range(0, tq, BISECT_ROWS):
        def count_ge(cand):
            cb = jnp.broadcast_to(cand, (BISECT_ROWS, LANES))

            def body(p, acc):
                for u in range(2):
                    for c in range(0, sk, LANES):
                        kk = key_ref[2 * p + u, r0:r0 + BISECT_ROWS, c:c + LANES]
                        acc = acc + jnp.where(kk >= cb, 1.0, 0.0)
                return acc

            acc = lax.fori_loop(0, (n_blocks + 1) // 2, body, jnp.zeros((BISECT_ROWS, LANES), f32))
            return jnp.sum(acc, axis=1, keepdims=True)

        def bit_step(b, ans):
            cand = ans + jnp.left_shift(jnp.int32(1), 31 - b)
            return jnp.where(count_ge(cand) >= float(topk), cand, ans)

        thr = lax.fori_loop(0, 32, bit_step, jnp.full((BISECT_ROWS, 1), INT_MIN, jnp.int32))
        n_gt = count_ge(thr + 1)
        thr_ref[r0:r0 + BISECT_ROWS, :] = jnp.broadcast_to(thr, (BISECT_ROWS, LANES))
        need_ref[r0:r0 + BISECT_ROWS, :] = jnp.broadcast_to(float(topk) - n_gt, (BISECT_ROWS, LANES))

    m_ref[...] = jnp.full(m_ref.shape, M_INIT, f32)
    acc_ref[...] = jnp.zeros(acc_ref.shape, f32)
    ceq_ref[...] = jnp.zeros(ceq_ref.shape, f32)

    def key_rows(j):
        return pl.ds(pl.multiple_of(j * sk, sk), sk)

    def mask_block(j, slot, diag):
        thr, need, ceq = thr_ref[...], need_ref[...], ceq_ref[...]
        keys = [key_ref[j, :, c:c + LANES] for c in range(0, sk, LANES)]
        eqs = [k == thr for k in keys]
        eqf = jnp.concatenate([jnp.where(e, 1.0, 0.0) for e in eqs], axis=1).astype(bf16)
        pref = _dot(eqf, tri_ref[...])
        total = jnp.broadcast_to(pref[:, sk - 1:sk], (tq, LANES))
        for n, (k, e) in enumerate(zip(keys, eqs)):
            c = n * LANES
            sel = jnp.logical_or(k > thr, jnp.logical_and(e, pref[:, c:c + LANES] + ceq <= need))
            if diag:
                ri = lax.broadcasted_iota(jnp.int32, (tq, LANES), 0)
                ci = lax.broadcasted_iota(jnp.int32, (tq, LANES), 1)
                sel = jnp.logical_and(sel, ci + c <= ri)
            mb_ref[:, slot * sk + c:slot * sk + c + LANES] = jnp.where(sel, 0.0, NEG_MASK)
        ceq_ref[...] = ceq + total

    def attend(j0, n_blk, near_col):
        for b in range(n_blk):
            mask_block(j0 + b, b, near_col is not None and b == n_blk - 1)
        heads, blks = range(N_HEADS_A), range(n_blk)
        logits = []
        for h in heads:
            q = qlat_ref[h]
            parts = []
            for b in blks:
                lg = _dot_nt(q, ckv_ref[key_rows(j0 + b), :KV_RANK]) + mb_ref[:, b * sk:(b + 1) * sk]
                if near_col is not None:
                    lg = lg + bias_ref[h, :, near_col + b * sk:near_col + (b + 1) * sk]
                parts.append(lg)
            logits.append(parts)
        m_olds = [m_ref[h] for h in heads]
        m_news = []
        for h in heads:
            mx = functools.reduce(jnp.maximum, [p[:, c:c + LANES] for p in logits[h] for c in range(0, sk, LANES)])
            m_news.append(jnp.maximum(m_olds[h], jnp.max(mx, axis=1, keepdims=True)))
        probs = []
        for h in heads:
            m_wide = jnp.concatenate([m_news[h]] * (sk // LANES), axis=1)
            probs.append([jnp.exp2(logits[h][b] - m_wide).astype(bf16) for b in blks])
        for h in heads:
            pv = None
            for b in blks:
                d = _dot(probs[h][b], ckv_ref[key_rows(j0 + b), :])
                pv = d if pv is None else pv + d
            alpha = jnp.exp2(m_olds[h] - m_news[h])
            acc_ref[h] = jnp.concatenate([alpha, alpha], axis=1) * acc_ref[h] + pv
            m_ref[h] = m_news[h]

    lone = jnp.logical_and(i >= 2, i % 2 == 0)

    @pl.when(lone)
    def _():
        attend(0, 1, None)

    first = jnp.where(lone, 1, 0)

    def far_pair(p, carry):
        attend(first + 2 * p, 2, None)
        return carry

    lax.fori_loop(0, (i - 1 - first) // 2, far_pair, 0)

    @pl.when(i >= 1)
    def _():
        attend(i - 1, 2, 0)

    @pl.when(i == 0)
    def _():
        attend(0, 1, sk)

    outs = []
    for h in range(N_HEADS_A):
        acc = acc_ref[h]
        o_lat = acc[:, :KV_RANK] / acc[:, KV_RANK:]
        outs.append(_dot(o_lat.astype(bf16), wuv_ref[h]))
    o_ref[...] = jnp.concatenate(outs, axis=1).astype(bf16)


def _t5_bucket(n):
    max_exact = N_BUCKETS // 2
    nf = jnp.maximum(n, 1).astype(f32)
    large = max_exact + (jnp.log(nf / max_exact) / math.log(MAX_DISTANCE / max_exact)
                         * (N_BUCKETS - max_exact)).astype(jnp.int32)
    large = jnp.minimum(large, N_BUCKETS - 1)
    return jnp.where(n < max_exact, n, large)


def _dsa(qa, qidx, widx, kidx, ckv, wuk_t, wuv_t, rel_bias, batch, seq):
    n = qa.shape[0]
    tq = ATT_BLOCK
    nq = seq // tq
    topk = min(INDEX_TOPK, seq // 4)
    dist = tq + jnp.arange(tq)[:, None] - jnp.arange(2 * tq)[None, :]
    bias_near = jnp.moveaxis(rel_bias[_t5_bucket(jnp.maximum(dist, 0))], -1, 0).astype(f32)
    far_n = np.float32(tq + 1)
    assert 16 + int(np.log(far_n / 16) / math.log(MAX_DISTANCE / 16) * 16) >= N_BUCKETS - 1
    bias_near = (bias_near - rel_bias[N_BUCKETS - 1].astype(f32)[:, None, None]) * LOG2E
    tri = (jnp.arange(tq)[:, None] <= jnp.arange(tq)[None, :]).astype(bf16)

    qrow = lambda b, i: (b * nq + i, 0)
    full = lambda b, i: (b, 0)
    c3 = lambda b, i: (0, 0, 0)
    c2 = lambda b, i: (0, 0)
    kern = functools.partial(_dsa_kernel, topk=topk)
    return pl.pallas_call(
        kern,
        grid=(batch, nq),
        in_specs=[pl.BlockSpec((tq, WIDTH_A), qrow),
                  pl.BlockSpec((tq, N_IDX_HEADS * IDX_DIM), qrow),
                  pl.BlockSpec((tq, LANES), qrow),
                  pl.BlockSpec((seq, IDX_DIM), full),
                  pl.BlockSpec((seq, 2 * KV_RANK), full),
                  pl.BlockSpec((N_HEADS_A, HEAD_DIM, KV_RANK), c3),
                  pl.BlockSpec((N_HEADS_A, KV_RANK, HEAD_DIM), c3),
                  pl.BlockSpec((N_HEADS_A, tq, 2 * tq), c3),
                  pl.BlockSpec((tq, tq), c2)],
        out_specs=pl.BlockSpec((tq, WIDTH_A), qrow),
        out_shape=jax.ShapeDtypeStruct((n, WIDTH_A), bf16),
        scratch_shapes=[pltpu.VMEM((nq + 1, tq, tq), jnp.int32),
                        pltpu.VMEM((N_HEADS_A, tq, KV_RANK), bf16),
                        pltpu.VMEM((N_IDX_HEADS, tq, LANES), f32),
                        pltpu.VMEM((tq, LANES), jnp.int32),
                        pltpu.VMEM((tq, LANES), f32),
                        pltpu.VMEM((tq, LANES), f32),
                        pltpu.VMEM((tq, 2 * tq), f32),
                        pltpu.VMEM((N_HEADS_A, tq, LANES), f32),
                        pltpu.VMEM((N_HEADS_A, tq, 2 * KV_RANK), f32)],
        compiler_params=pltpu.CompilerParams(dimension_semantics=("parallel", "arbitrary"),
                                             vmem_limit_bytes=VMEM_LIMIT),
        name="dsa",
    )(qa, qidx, widx, kidx, ckv, wuk_t, wuv_t, bias_near, tri)


def _sb_kernel(q_ref, k_ref, v_ref, tri_ref, o_ref, carry_ref, acc_ref):
    tq = sk = ATT_BLOCK
    i = pl.program_id(1)
    carry_ref[...] = jnp.zeros(carry_ref.shape, f32)
    acc_ref[...] = jnp.zeros(acc_ref.shape, f32)

    def step(j, diag):
        rows = pl.ds(pl.multiple_of(j * sk, sk), sk)
        if diag:
            strict = lax.broadcasted_iota(jnp.int32, (tq, sk), 1) < lax.broadcasted_iota(jnp.int32, (tq, sk), 0)
        heads = range(N_HEADS_B)
        lanes = [slice(h * HEAD_DIM, (h + 1) * HEAD_DIM) for h in heads]
        zs = [_dot_nt(q_ref[:, lanes[h]] * (HEAD_DIM ** -0.5), k_ref[rows, lanes[h]]) for h in heads]
        lfs = [-(jnp.maximum(z, 0.0) + jnp.log(1.0 + jnp.exp(-jnp.abs(z)))) for z in zs]
        if diag:
            lfs = [jnp.where(strict, lf, 0.0) for lf in lfs]
        splits = [_split_bf16(lf) for lf in lfs]
        laters = [_dot(hi, tri_ref[...]) + _dot(lo, tri_ref[...]) for hi, lo in splits]
        carries = [carry_ref[h] for h in heads]
        probs = [jnp.exp(zs[h] + lfs[h] + laters[h] + jnp.concatenate([carries[h]] * (sk // LANES), axis=1))
                 for h in heads]
        if diag:
            probs = [jnp.where(strict, a, 0.0) for a in probs]
        top = None
        for h in heads:
            acc_ref[h] += _dot(probs[h].astype(bf16), v_ref[rows, lanes[h]])
            carry = carries[h] + jnp.sum(lfs[h], axis=1, keepdims=True)
            carry_ref[h] = carry
            top = carry if top is None else jnp.maximum(top, carry)
        return jnp.max(top)

    def cond(st):
        j, top = st
        return jnp.logical_and(j >= 0, top > SB_SKIP)

    def body(st):
        j, _ = st
        return j - 1, step(j, False)

    lax.while_loop(cond, body, (i - 1, step(i, True)))
    o_ref[...] = jnp.concatenate([acc_ref[h] for h in range(N_HEADS_B)], axis=1).astype(bf16)


def _sb(qb, kb, vb, batch, seq):
    n = qb.shape[0]
    tq = ATT_BLOCK
    nq = seq // tq
    tri = (jnp.arange(tq)[:, None] > jnp.arange(tq)[None, :]).astype(bf16)
    qrow = lambda b, i: (b * nq + i, 0)
    full = lambda b, i: (b, 0)
    return pl.pallas_call(
        _sb_kernel,
        grid=(batch, nq),
        in_specs=[pl.BlockSpec((tq, WIDTH_B), qrow),
                  pl.BlockSpec((seq, WIDTH_B), full),
                  pl.BlockSpec((seq, WIDTH_B), full),
                  pl.BlockSpec((tq, tq), lambda b, i: (0, 0))],
        out_specs=pl.BlockSpec((tq, WIDTH_B), qrow),
        out_shape=jax.ShapeDtypeStruct((n, WIDTH_B), bf16),
        scratch_shapes=[pltpu.VMEM((N_HEADS_B, tq, LANES), f32),
                        pltpu.VMEM((N_HEADS_B, tq, HEAD_DIM), f32)],
        compiler_params=pltpu.CompilerParams(dimension_semantics=("parallel", "arbitrary"),
                                             vmem_limit_bytes=VMEM_LIMIT),
        name="sb",
    )(qb, kb, vb, tri)


def _merge_kernel(oa_ref, ob_ref, ga_ref, gb_ref, x_ref, gate_ref, shift_ref, scale_ref,
                  wa_ref, wb_ref, wo_ref, g_ref, b_ref, wr_hi_ref, wr_lo_ref, br_ref,
                  x1_ref, h2_ref, logit_ref):
    ya = _dot(oa_ref[...], wa_ref[...])
    yb = _dot(ob_ref[...], wb_ref[...])
    merged = jax.nn.sigmoid(ga_ref[...].astype(f32)) * ya + jax.nn.sigmoid(gb_ref[...].astype(f32)) * yb
    y = _dot(merged.astype(bf16), wo_ref[...])
    x1 = _ln(DN_ALPHA * x_ref[...] + gate_ref[0] * y) * g_ref[...] + b_ref[...]
    x1_ref[...] = x1
    h2 = _ln(x1) * (1.0 + scale_ref[0]) + shift_ref[0]
    hi, lo = _split_bf16(h2)
    h2_ref[...] = hi
    logit_ref[...] = (_dot(hi, wr_hi_ref[...]) + _dot(lo, wr_hi_ref[...]) + _dot(hi, wr_lo_ref[...])
                      + br_ref[...])


def _merge(oa, ob, ga, gb, x2, gate1, shift2, scale2, wa, wb, wo, g, b, wr_hi, wr_lo, br, seq):
    n, d = x2.shape
    tm = min(ROW_TILE, seq)
    per_b = seq // tm
    row = lambda i: (i, 0)
    bat = lambda i: (i // per_b, 0, 0)
    const = lambda i: (0, 0)
    return pl.pallas_call(
        _merge_kernel,
        grid=(n // tm,),
        in_specs=[pl.BlockSpec((tm, WIDTH_A), row), pl.BlockSpec((tm, WIDTH_B), row),
                  pl.BlockSpec((tm, d), row), pl.BlockSpec((tm, d), row), pl.BlockSpec((tm, d), row),
                  pl.BlockSpec((1, 1, d), bat), pl.BlockSpec((1, 1, d), bat), pl.BlockSpec((1, 1, d), bat),
                  pl.BlockSpec((WIDTH_A, d), const), pl.BlockSpec((WIDTH_B, d), const), pl.BlockSpec((d, d), const),
                  pl.BlockSpec((1, d), const), pl.BlockSpec((1, d), const),
                  pl.BlockSpec((d, LANES), const), pl.BlockSpec((d, LANES), const), pl.BlockSpec((1, LANES), const)],
        out_specs=[pl.BlockSpec((tm, d), row), pl.BlockSpec((tm, d), row), pl.BlockSpec((tm, LANES), row)],
        out_shape=[jax.ShapeDtypeStruct((n, d), f32), jax.ShapeDtypeStruct((n, d), bf16),
                   jax.ShapeDtypeStruct((n, LANES), f32)],
        compiler_params=pltpu.CompilerParams(dimension_semantics=("parallel",), vmem_limit_bytes=VMEM_LIMIT),
        name="merge",
    )(oa, ob, ga, gb, x2, gate1, shift2, scale2, wa, wb, wo, g, b, wr_hi, wr_lo, br)


def _expert_kernel(be_ref, nb_ref, x_ref, wgu_ref, bgu_ref, wdn_ref, bdn_ref, y_ref, wgu_bf, wdn_bf):
    i = pl.program_id(0)

    @pl.when(jnp.logical_or(i == 0, be_ref[i] != be_ref[jnp.maximum(i - 1, 0)]))
    def _():
        wgu_bf[...] = wgu_ref[0].astype(bf16)
        wdn_bf[...] = wdn_ref[0].astype(bf16)

    @pl.when(i < nb_ref[0])
    def _():
        gu = _dot(x_ref[...], wgu_bf[...]) + bgu_ref[0]
        a = jnp.minimum(gu[:, :D_EXPERT], SWIGLU_LIMIT)
        u = jnp.clip(gu[:, D_EXPERT:], -SWIGLU_LIMIT, SWIGLU_LIMIT)
        act = (u + 1.0) * a * jax.nn.sigmoid(SWIGLU_ALPHA * a)
        y_ref[...] = (_dot(act.astype(bf16), wdn_bf[...]) + bdn_ref[0]).astype(y_ref.dtype)

    @pl.when(i >= nb_ref[0])
    def _():
        y_ref[...] = jnp.zeros(y_ref.shape, y_ref.dtype)


def _experts(xs, block_expert, n_used, wgu, bgu, wdn, bdn):
    rows, d = xs.shape
    n_blocks = rows // MOE_ROWS
    e_idx = lambda i, be, nb: (be[i], 0, 0)
    grid_spec = pltpu.PrefetchScalarGridSpec(
        num_scalar_prefetch=2,
        grid=(n_blocks,),
        in_specs=[pl.BlockSpec((MOE_ROWS, d), lambda i, be, nb: (i, 0)),
                  pl.BlockSpec((1, d, 2 * D_EXPERT), e_idx),
                  pl.BlockSpec((1, 1, 2 * D_EXPERT), e_idx),
                  pl.BlockSpec((1, D_EXPERT, d), e_idx),
                  pl.BlockSpec((1, 1, d), e_idx)],
        out_specs=pl.BlockSpec((MOE_ROWS, d), lambda i, be, nb: (i, 0)),
        scratch_shapes=[pltpu.VMEM((d, 2 * D_EXPERT), bf16), pltpu.VMEM((D_EXPERT, d), bf16)],
    )
    return pl.pallas_call(
        _expert_kernel,
        grid_spec=grid_spec,
        out_shape=jax.ShapeDtypeStruct((rows, d), bf16),
        compiler_params=pltpu.CompilerParams(dimension_semantics=("arbitrary",), vmem_limit_bytes=VMEM_LIMIT),
        name="experts",
    )(block_expert, n_used, xs, wgu, bgu, wdn, bdn)


def _combine_kernel(x_ref, y_ref, gk_ref, gate_ref, g_ref, b_ref, o_ref):
    gk = gk_ref[...]
    y = y_ref[0].astype(f32) * gk[:, 0:1]
    for k in range(1, TOP_K):
        y = y + y_ref[k].astype(f32) * gk[:, k:k + 1]
    o_ref[...] = _ln(DN_ALPHA * x_ref[...] + gate_ref[0] * y) * g_ref[...] + b_ref[...]


def _combine(x1, yk, gates, gate2, g, b, seq):
    n, d = x1.shape
    tm = min(ROW_TILE, seq)
    per_b = seq // tm
    row = lambda i: (i, 0)
    const = lambda i: (0, 0)
    return pl.pallas_call(
        _combine_kernel,
        grid=(n // tm,),
        in_specs=[pl.BlockSpec((tm, d), row),
                  pl.BlockSpec((TOP_K, tm, d), lambda i: (0, i, 0)),
                  pl.BlockSpec((tm, TOP_K), row),
                  pl.BlockSpec((1, 1, d), lambda i: (i // per_b, 0, 0)),
                  pl.BlockSpec((1, d), const), pl.BlockSpec((1, d), const)],
        out_specs=pl.BlockSpec((tm, d), row),
        out_shape=jax.ShapeDtypeStruct((n, d), f32),
        compiler_params=pltpu.CompilerParams(dimension_semantics=("parallel",), vmem_limit_bytes=VMEM_LIMIT),
        name="combine",
    )(x1, yk, gates, gate2, g, b)


def _route(logits, n):
    top_val, top_idx = lax.top_k(logits, TOP_K)
    gates = jax.nn.softmax(top_val, axis=-1)
    onehot = jnp.sum(jax.nn.one_hot(top_idx, N_EXPERTS, dtype=jnp.int32), axis=1)
    before = jnp.cumsum(onehot, axis=0) - onehot
    counts = before[-1] + onehot[-1]
    padded = (counts + MOE_ROWS - 1) // MOE_ROWS * MOE_ROWS
    pad_end = jnp.cumsum(padded)
    pad_start = pad_end - padded
    dest = pad_start[top_idx] + jnp.take_along_axis(before, top_idx, axis=1)
    n_blocks = -(-(n * TOP_K + N_EXPERTS * (MOE_ROWS - 1)) // MOE_ROWS)
    block_expert = jnp.minimum(
        jnp.searchsorted(pad_end, jnp.arange(n_blocks, dtype=jnp.int32) * MOE_ROWS, side='right'),
        N_EXPERTS - 1).astype(jnp.int32)
    n_used = (pad_end[-1:] // MOE_ROWS).astype(jnp.int32)
    return gates, dest.astype(jnp.int32), block_expert, n_used, n_blocks


def _split_w_in(w):
    split = sum(COL_SPLITS[:5])
    return jnp.pad(w[:, :split], ((0, 0), (0, IDX_PAD))).astype(bf16), w[:, split:].astype(bf16)


def kernel(x, c, rel_bias, w_ada, b_ada, w_in, g_kv, w_uk, w_uv, w_a_out, w_b_out, w_o, ln1_g, ln1_b,
           w_router, b_router, w_gu, b_gu, w_dn, b_dn, ln2_g, ln2_b):
    B, S, D = x.shape
    N = B * S
    assert D == D_MODEL and S % ATT_BLOCK == 0
    cond = jax.nn.silu(c)
    x2 = x.reshape(N, D)
    for l in range(DEPTH):
        mod = jnp.dot(cond, w_ada[l], precision=lax.Precision.HIGHEST) + b_ada[l]
        shift1, scale1, gate1, shift2, scale2, gate2 = [m[:, None, :] for m in jnp.split(mod, 6, axis=-1)]

        qa, ckv, qidx, kidx, widx, qb, kb, vb, ga, gb = _proj(
            x2, shift1, scale1, *_split_w_in(w_in[l]), g_kv[l][None, :], S)
        wuk_t = jnp.transpose(w_uk[l], (1, 2, 0)).astype(bf16)
        wuv_t = jnp.transpose(w_uv[l], (1, 0, 2)).astype(bf16)
        oa = _dsa(qa, qidx, widx, kidx, ckv, wuk_t, wuv_t, rel_bias, B, S)
        ob = _sb(qb, kb, vb, B, S)

        wr = jnp.pad(w_router[l], ((0, 0), (0, LANES - N_EXPERTS)))
        wr_hi, wr_lo = _split_bf16(wr)
        br = jnp.pad(b_router[l], (0, LANES - N_EXPERTS))[None, :]
        x1, h2, logits = _merge(oa, ob, ga, gb, x2, gate1, shift2, scale2,
                                w_a_out[l].astype(bf16), w_b_out[l].astype(bf16), w_o[l].astype(bf16),
                                ln1_g[l][None, :], ln1_b[l][None, :], wr_hi, wr_lo, br, S)

        gates, dest, block_expert, n_used, n_blocks = _route(logits[:, :N_EXPERTS], N)
        rows = n_blocks * MOE_ROWS
        row_token = jnp.zeros((rows,), jnp.int32).at[dest.reshape(-1)].set(
            jnp.repeat(jnp.arange(N, dtype=jnp.int32), TOP_K))
        xs = h2[row_token]
        ys = _experts(xs, block_expert, n_used, w_gu[l], b_gu[l][:, None, :], w_dn[l], b_dn[l][:, None, :])
        yk = ys[dest.T]
        x2 = _combine(x1, yk, gates, gate2, ln2_g[l][None, :], ln2_b[l][None, :], S)
    return x2.reshape(B, S, D)
```

```python
import functools
import math

import numpy as np
import jax
import jax.numpy as jnp
from jax import lax
from jax.experimental import pallas as pl
from jax.experimental.pallas import tpu as pltpu

D_MODEL = 1024
HEAD_DIM = 64
N_HEADS_A = 8
WIDTH_A = N_HEADS_A * HEAD_DIM
KV_RANK = 128
N_IDX_HEADS = 4
IDX_DIM = 64
INDEX_TOPK = 256
N_HEADS_B = 8
WIDTH_B = N_HEADS_B * HEAD_DIM
N_BUCKETS = 32
MAX_DISTANCE = 128
N_EXPERTS = 32
TOP_K = 4
D_EXPERT = 1024
SWIGLU_LIMIT = 7.0
SWIGLU_ALPHA = 1.702
LN_EPS = 1e-5
RMS_EPS = 1e-6
DEPTH = 2
DN_ALPHA = (2 * DEPTH) ** 0.25
IDX_SCALE = (N_IDX_HEADS ** -0.5) * (IDX_DIM ** -0.5)
LOG2E = math.log2(math.e)

COL_SPLITS = (WIDTH_A, KV_RANK, N_IDX_HEADS * IDX_DIM, IDX_DIM, N_IDX_HEADS,
              WIDTH_B, WIDTH_B, WIDTH_B, D_MODEL, D_MODEL)
N_COLS = sum(COL_SPLITS)

LANES = 128
VMEM_LIMIT = 56 * 1024 * 1024

IDX_PAD = LANES - IDX_DIM - N_IDX_HEADS
C_QA = 0
C_CKV = C_QA + WIDTH_A
C_QIDX = C_CKV + KV_RANK
C_KW = C_QIDX + N_IDX_HEADS * IDX_DIM
C_QB = C_KW + LANES
C_KB = C_QB + WIDTH_B
C_VB = C_KB + WIDTH_B
C_GA = C_VB + WIDTH_B
C_GB = C_GA + D_MODEL
N_COLS_PAD = C_GB + D_MODEL

ROW_TILE = 512
ATT_BLOCK = 256
MOE_ROWS = 256
NEG_MASK = -1e30
M_INIT = -1e29
SB_SKIP = -110.0
KEY16_NEG_INF = -32641
MIN_NORMAL_KEY16 = 128

f32 = jnp.float32
bf16 = jnp.bfloat16


def _ln(x):
    mu = jnp.mean(x, axis=-1, keepdims=True)
    xc = x - mu
    var = jnp.mean(xc * xc, axis=-1, keepdims=True)
    return xc * lax.rsqrt(var + LN_EPS)


def _dot(a, b):
    return jnp.dot(a, b, preferred_element_type=f32)


def _dot_nt(a, b):
    return lax.dot_general(a, b, (((1,), (1,)), ((), ())), preferred_element_type=f32)


def _split_bf16(x):
    hi = x.astype(bf16)
    lo = (x - hi.astype(f32)).astype(bf16)
    return hi, lo


def _proj_kernel(x_ref, shift_ref, scale_ref, wa_ref, wb_ref, gkv_ref,
                 qa_ref, ckv_ref, qidx_ref, kidx_ref, widx_ref, qb_ref, kb_ref, vb_ref, ga_ref, gb_ref):
    h = _ln(x_ref[...]) * (1.0 + scale_ref[0]) + shift_ref[0]
    hb = h.astype(bf16)

    def mm(c0, width):
        if c0 < C_QB:
            return _dot(hb, wa_ref[:, c0:c0 + width])
        return _dot(hb, wb_ref[:, c0 - C_QB:c0 - C_QB + width])

    qa_ref[...] = mm(C_QA, WIDTH_A).astype(bf16)
    ckv = mm(C_CKV, KV_RANK)
    ckv = ckv * lax.rsqrt(jnp.mean(ckv * ckv, axis=-1, keepdims=True) + RMS_EPS) * gkv_ref[...]
    ckv_ref[...] = jnp.concatenate([ckv, jnp.ones_like(ckv)], axis=1).astype(bf16)
    qidx_ref[...] = mm(C_QIDX, N_IDX_HEADS * IDX_DIM).astype(bf16)
    kw = mm(C_KW, LANES)
    kidx_ref[...] = kw[:, :IDX_DIM].astype(bf16)
    widx_ref[...] = kw * IDX_SCALE
    qb_ref[...] = mm(C_QB, WIDTH_B).astype(bf16)
    kb_ref[...] = mm(C_KB, WIDTH_B).astype(bf16)
    vb_ref[...] = mm(C_VB, WIDTH_B).astype(bf16)
    for c in range(0, D_MODEL, 512):
        ga_ref[:, c:c + 512] = mm(C_GA + c, 512).astype(bf16)
        gb_ref[:, c:c + 512] = mm(C_GB + c, 512).astype(bf16)


def _proj(x2, shift, scale, w_dsa, w_rest, g_kv, seq):
    n, d = x2.shape
    tm = min(ROW_TILE, seq)
    per_b = seq // tm
    row = lambda i: (i, 0)
    bat = lambda i: (i // per_b, 0, 0)
    const = lambda i: (0, 0)
    widths = (WIDTH_A, 2 * KV_RANK, N_IDX_HEADS * IDX_DIM, IDX_DIM, LANES, WIDTH_B, WIDTH_B, WIDTH_B, D_MODEL, D_MODEL)
    dtypes = (bf16, bf16, bf16, bf16, f32, bf16, bf16, bf16, bf16, bf16)
    return pl.pallas_call(
        _proj_kernel,
        grid=(n // tm,),
        in_specs=[pl.BlockSpec((tm, d), row),
                  pl.BlockSpec((1, 1, d), bat),
                  pl.BlockSpec((1, 1, d), bat),
                  pl.BlockSpec((d, C_QB), const),
                  pl.BlockSpec((d, N_COLS_PAD - C_QB), const),
                  pl.BlockSpec((1, KV_RANK), const)],
        out_specs=[pl.BlockSpec((tm, w), row) for w in widths],
        out_shape=[jax.ShapeDtypeStruct((n, w), dt) for w, dt in zip(widths, dtypes)],
        compiler_params=pltpu.CompilerParams(dimension_semantics=("parallel",), vmem_limit_bytes=VMEM_LIMIT),
        name="proj",
    )(x2, shift, scale, w_dsa, w_rest, g_kv)


def _dsa_kernel(qa_ref, qidx_ref, widx_ref, kidx_ref, ckv_ref, wuk_ref, wuv_ref, bias_ref, tri_ref,
                o_ref, key_ref, hi_ref, d1_ref, d0_ref, qlat_ref, wrep_ref, thr_ref, need_ref, ceq_ref, mb_ref, m_ref, acc_ref, *, topk):
    tq = sk = ATT_BLOCK
    i = pl.program_id(1)
    n_blocks = i + 1

    for h in range(N_HEADS_A):
        ql = _dot(qa_ref[:, h * HEAD_DIM:(h + 1) * HEAD_DIM], wuk_ref[h]) * (HEAD_DIM ** -0.5 * LOG2E)
        qlat_ref[h] = ql.astype(bf16)

    rowi = lax.broadcasted_iota(jnp.int32, (tq, sk), 0)
    coli = lax.broadcasted_iota(jnp.int32, (tq, sk), 1)
    w = widx_ref[:, IDX_DIM:IDX_DIM + N_IDX_HEADS]
    for h in range(N_IDX_HEADS):
        wrep_ref[h] = jnp.broadcast_to(w[:, h:h + 1], (tq, LANES))

    def score_pair(p, carry):
        for u in range(2):
            j = 2 * p + u
            kblk = kidx_ref[pl.ds(pl.multiple_of(jnp.minimum(j, i) * sk, sk), sk), :]
            s = None
            for h in range(N_IDX_HEADS):
                d = _dot_nt(qidx_ref[:, h * IDX_DIM:(h + 1) * IDX_DIM], kblk)
                t = jnp.maximum(d, 0.0) * jnp.concatenate([wrep_ref[h]] * (sk // LANES), axis=1)
                s = t if s is None else s + t
            s = jnp.where(s == 0.0, 0.0, s)
            s = jnp.where((j - i) * sk + coli <= rowi, s, -jnp.inf)
            bits = pltpu.bitcast(s, jnp.int32)
            sign = bits >> 31
            key_ref[j] = bits ^ (sign & np.int32(0x7FFFFFFF))
            hi_ref[j] = pltpu.bitcast(bits & np.int32(-65536), f32).astype(bf16)
            low = (bits ^ sign) & np.int32(0xFFFF)
            d1_ref[j] = (low >> 8).astype(f32).astype(bf16)
            d0_ref[j] = (low & np.int32(0xFF)).astype(f32).astype(bf16)
        return carry

    n_pairs = (n_blocks + 1) // 2
    lax.fori_loop(0, n_pairs, score_pair, 0)
    hi_ref[n_blocks] = jnp.full((tq, sk), -jnp.inf, bf16)
    d1_ref[n_blocks] = jnp.full((tq, sk), -1.0, bf16)
    d0_ref[n_blocks] = jnp.full((tq, sk), -1.0, bf16)

    one, zero, gone = (jnp.asarray(v, bf16) for v in (1.0, 0.0, -1.0))
    k_top = float(topk)

    def count(ref, cand, strict):
        cb = jnp.broadcast_to(cand, (tq, LANES)).astype(bf16)

        def body(p, acc):
            for u in range(2):
                for c in range(0, sk, LANES):
                    v = ref[2 * p + u, :, c:c + LANES]
                    acc = acc + jnp.where(v > cb if strict else v >= cb, one, zero)
            return acc

        acc = lax.fori_loop(0, n_pairs, body, jnp.zeros((tq, LANES), bf16))
        return jnp.sum(acc.astype(f32), axis=1, keepdims=True)

    def keep_where_equal(dst_ref, src_ref, value):
        vb = jnp.broadcast_to(value, (tq, LANES)).astype(bf16)

        def body(p, carry):
            for u in range(2):
                for c in range(0, sk, LANES):
                    idx = (2 * p + u, slice(None), slice(c, c + LANES))
                    dst_ref[idx] = jnp.where(src_ref[idx] == vb, dst_ref[idx], gone)
            return carry

        lax.fori_loop(0, n_pairs, body, 0)

    def hi_value(key16):
        pattern = key16 ^ ((key16 >> 31) & np.int32(0x7FFF))
        return pltpu.bitcast(jnp.left_shift(pattern, 16), f32)

    def hi_step(b, code):
        cand = code + jnp.left_shift(jnp.int32(1), 15 - b)
        ok = jnp.logical_or(count(hi_ref, hi_value(cand), False) >= k_top, cand <= KEY16_NEG_INF)
        return jnp.where(ok, cand, code)

    hi_code = lax.fori_loop(0, 16, hi_step, jnp.full((tq, 1), -32768, jnp.int32))
    hi_code = jnp.where(jnp.logical_and(hi_code >= 1, hi_code < MIN_NORMAL_KEY16), 0, hi_code)
    hi_thr = hi_value(hi_code)
    above_hi = count(hi_ref, hi_thr, True)

    def byte_digit(ref, k_left):
        def step(b, digit):
            cand = digit + jnp.left_shift(jnp.int32(1), 7 - b).astype(f32)
            return jnp.where(count(ref, cand, False) >= k_left, cand, digit)

        return lax.fori_loop(0, 8, step, jnp.zeros((tq, 1), f32))

    keep_where_equal(d1_ref, hi_ref, hi_thr)
    d1_thr = byte_digit(d1_ref, k_top - above_hi)
    above_d1 = count(d1_ref, d1_thr, True)
    keep_where_equal(d0_ref, d1_ref, d1_thr)
    d0_thr = byte_digit(d0_ref, k_top - above_hi - above_d1)
    above_d0 = count(d0_ref, d0_thr, True)

    pattern_hi = hi_code ^ ((hi_code >> 31) & np.int32(0x7FFF))
    low = (d1_thr * 256.0 + d0_thr).astype(jnp.int32) ^ ((pattern_hi >> 31) & np.int32(0xFFFF))
    thr_bits = jnp.left_shift(pattern_hi, 16) | low
    thr = thr_bits ^ ((thr_bits >> 31) & np.int32(0x7FFFFFFF))
    thr_ref[...] = jnp.broadcast_to(thr, (tq, LANES))
    need_ref[...] = jnp.broadcast_to(k_top - above_hi - above_d1 - above_d0, (tq, LANES))

    m_ref[...] = jnp.full(m_ref.shape, M_INIT, f32)
    acc_ref[...] = jnp.zeros(acc_ref.shape, f32)
    ceq_ref[...] = jnp.zeros(ceq_ref.shape, f32)

    def key_rows(j):
        return pl.ds(pl.multiple_of(j * sk, sk), sk)

    def mask_block(j, slot, diag):
        thr, need, ceq = thr_ref[...], need_ref[...], ceq_ref[...]
        keys = [key_ref[j, :, c:c + LANES] for c in range(0, sk, LANES)]
        eqs = [k == thr for k in keys]
        eqf = jnp.concatenate([jnp.where(e, 1.0, 0.0) for e in eqs], axis=1).astype(bf16)
        pref = _dot(eqf, tri_ref[...])
        total = jnp.broadcast_to(pref[:, sk - 1:sk], (tq, LANES))
        for n, (k, e) in enumerate(zip(keys, eqs)):
            c = n * LANES
            sel = jnp.logical_or(k > thr, jnp.logical_and(e, pref[:, c:c + LANES] + ceq <= need))
            if diag:
                ri = lax.broadcasted_iota(jnp.int32, (tq, LANES), 0)
                ci = lax.broadcasted_iota(jnp.int32, (tq, LANES), 1)
                sel = jnp.logical_and(sel, ci + c <= ri)
            mb_ref[:, slot * sk + c:slot * sk + c + LANES] = jnp.where(sel, 0.0, NEG_MASK)
        ceq_ref[...] = ceq + total

    def attend(j0, n_blk, near_col):
        for b in range(n_blk):
            mask_block(j0 + b, b, near_col is not None and b == n_blk - 1)
        heads, blks = range(N_HEADS_A), range(n_blk)
        logits = []
        for h in heads:
            q = qlat_ref[h]
            parts = []
            for b in blks:
                lg = _dot_nt(q, ckv_ref[key_rows(j0 + b), :KV_RANK]) + mb_ref[:, b * sk:(b + 1) * sk]
                if near_col is not None:
                    lg = lg + bias_ref[h, :, near_col + b * sk:near_col + (b + 1) * sk]
                parts.append(lg)
            logits.append(parts)
        m_olds = [m_ref[h] for h in heads]
        m_news = []
        for h in heads:
            mx = functools.reduce(jnp.maximum, [p[:, c:c + LANES] for p in logits[h] for c in range(0, sk, LANES)])
            m_news.append(jnp.maximum(m_olds[h], jnp.max(mx, axis=1, keepdims=True)))
        probs = []
        for h in heads:
            m_wide = jnp.concatenate([m_news[h]] * (sk // LANES), axis=1)
            probs.append([jnp.exp2(logits[h][b] - m_wide).astype(bf16) for b in blks])
        for h in heads:
            pv = None
            for b in blks:
                d = _dot(probs[h][b], ckv_ref[key_rows(j0 + b), :])
                pv = d if pv is None else pv + d
            alpha = jnp.exp2(m_olds[h] - m_news[h])
            acc_ref[h] = jnp.concatenate([alpha, alpha], axis=1) * acc_ref[h] + pv
            m_ref[h] = m_news[h]

    lone = jnp.logical_and(i >= 2, i % 2 == 0)

    @pl.when(lone)
    def _():
        attend(0, 1, None)

    first = jnp.where(lone, 1, 0)

    def far_pair(p, carry):
        attend(first + 2 * p, 2, None)
        return carry

    lax.fori_loop(0, (i - 1 - first) // 2, far_pair, 0)

    @pl.when(i >= 1)
    def _():
        attend(i - 1, 2, 0)

    @pl.when(i == 0)
    def _():
        attend(0, 1, sk)

    outs = []
    for h in range(N_HEADS_A):
        acc = acc_ref[h]
        o_lat = acc[:, :KV_RANK] / acc[:, KV_RANK:]
        outs.append(_dot(o_lat.astype(bf16), wuv_ref[h]))
    o_ref[...] = jnp.concatenate(outs, axis=1).astype(bf16)


def _t5_bucket(n):
    max_exact = N_BUCKETS // 2
    nf = jnp.maximum(n, 1).astype(f32)
    large = max_exact + (jnp.log(nf / max_exact) / math.log(MAX_DISTANCE / max_exact)
                         * (N_BUCKETS - max_exact)).astype(jnp.int32)
    large = jnp.minimum(large, N_BUCKETS - 1)
    return jnp.where(n < max_exact, n, large)


def _dsa(qa, qidx, widx, kidx, ckv, wuk_t, wuv_t, rel_bias, batch, seq):
    n = qa.shape[0]
    tq = ATT_BLOCK
    nq = seq // tq
    topk = min(INDEX_TOPK, seq // 4)
    dist = tq + jnp.arange(tq)[:, None] - jnp.arange(2 * tq)[None, :]
    bias_near = jnp.moveaxis(rel_bias[_t5_bucket(jnp.maximum(dist, 0))], -1, 0).astype(f32)
    far_n = np.float32(tq + 1)
    assert 16 + int(np.log(far_n / 16) / math.log(MAX_DISTANCE / 16) * 16) >= N_BUCKETS - 1
    bias_near = (bias_near - rel_bias[N_BUCKETS - 1].astype(f32)[:, None, None]) * LOG2E
    tri = (jnp.arange(tq)[:, None] <= jnp.arange(tq)[None, :]).astype(bf16)

    qrow = lambda b, i: (b * nq + i, 0)
    full = lambda b, i: (b, 0)
    c3 = lambda b, i: (0, 0, 0)
    c2 = lambda b, i: (0, 0)
    kern = functools.partial(_dsa_kernel, topk=topk)
    return pl.pallas_call(
        kern,
        grid=(batch, nq),
        in_specs=[pl.BlockSpec((tq, WIDTH_A), qrow),
                  pl.BlockSpec((tq, N_IDX_HEADS * IDX_DIM), qrow),
                  pl.BlockSpec((tq, LANES), qrow),
                  pl.BlockSpec((seq, IDX_DIM), full),
                  pl.BlockSpec((seq, 2 * KV_RANK), full),
                  pl.BlockSpec((N_HEADS_A, HEAD_DIM, KV_RANK), c3),
                  pl.BlockSpec((N_HEADS_A, KV_RANK, HEAD_DIM), c3),
                  pl.BlockSpec((N_HEADS_A, tq, 2 * tq), c3),
                  pl.BlockSpec((tq, tq), c2)],
        out_specs=pl.BlockSpec((tq, WIDTH_A), qrow),
        out_shape=jax.ShapeDtypeStruct((n, WIDTH_A), bf16),
        scratch_shapes=[pltpu.VMEM((nq + 1, tq, tq), jnp.int32),
                        pltpu.VMEM((nq + 1, tq, tq), bf16),
                        pltpu.VMEM((nq + 1, tq, tq), bf16),
                        pltpu.VMEM((nq + 1, tq, tq), bf16),
                        pltpu.VMEM((N_HEADS_A, tq, KV_RANK), bf16),
                        pltpu.VMEM((N_IDX_HEADS, tq, LANES), f32),
                        pltpu.VMEM((tq, LANES), jnp.int32),
                        pltpu.VMEM((tq, LANES), f32),
                        pltpu.VMEM((tq, LANES), f32),
                        pltpu.VMEM((tq, 2 * tq), f32),
                        pltpu.VMEM((N_HEADS_A, tq, LANES), f32),
                        pltpu.VMEM((N_HEADS_A, tq, 2 * KV_RANK), f32)],
        compiler_params=pltpu.CompilerParams(dimension_semantics=("parallel", "arbitrary"),
                                             vmem_limit_bytes=VMEM_LIMIT),
        name="dsa",
    )(qa, qidx, widx, kidx, ckv, wuk_t, wuv_t, bias_near, tri)


def _sb_kernel(q_ref, k_ref, v_ref, tri_ref, o_ref, carry_ref, acc_ref):
    tq = sk = ATT_BLOCK
    i = pl.program_id(1)
    carry_ref[...] = jnp.zeros(carry_ref.shape, f32)
    acc_ref[...] = jnp.zeros(acc_ref.shape, f32)

    def step(j, diag):
        rows = pl.ds(pl.multiple_of(j * sk, sk), sk)
        if diag:
            strict = lax.broadcasted_iota(jnp.int32, (tq, sk), 1) < lax.broadcasted_iota(jnp.int32, (tq, sk), 0)
        heads = range(N_HEADS_B)
        lanes = [slice(h * HEAD_DIM, (h + 1) * HEAD_DIM) for h in heads]
        zs = [_dot_nt(q_ref[:, lanes[h]] * (HEAD_DIM ** -0.5), k_ref[rows, lanes[h]]) for h in heads]
        lfs = [-(jnp.maximum(z, 0.0) + jnp.log(1.0 + jnp.exp(-jnp.abs(z)))) for z in zs]
        if diag:
            lfs = [jnp.where(strict, lf, 0.0) for lf in lfs]
        splits = [_split_bf16(lf) for lf in lfs]
        laters = [_dot(hi, tri_ref[...]) + _dot(lo, tri_ref[...]) for hi, lo in splits]
        carries = [carry_ref[h] for h in heads]
        probs = [jnp.exp(zs[h] + lfs[h] + laters[h] + jnp.concatenate([carries[h]] * (sk // LANES), axis=1))
                 for h in heads]
        if diag:
            probs = [jnp.where(strict, a, 0.0) for a in probs]
        top = None
        for h in heads:
            acc_ref[h] += _dot(probs[h].astype(bf16), v_ref[rows, lanes[h]])
            carry = carries[h] + jnp.sum(lfs[h], axis=1, keepdims=True)
            carry_ref[h] = carry
            top = carry if top is None else jnp.maximum(top, carry)
        return jnp.max(top)

    def cond(st):
        j, top = st
        return jnp.logical_and(j >= 0, top > SB_SKIP)

    def body(st):
        j, _ = st
        return j - 1, step(j, False)

    lax.while_loop(cond, body, (i - 1, step(i, True)))
    o_ref[...] = jnp.concatenate([acc_ref[h] for h in range(N_HEADS_B)], axis=1).astype(bf16)


def _sb(qb, kb, vb, batch, seq):
    n = qb.shape[0]
    tq = ATT_BLOCK
    nq = seq // tq
    tri = (jnp.arange(tq)[:, None] > jnp.arange(tq)[None, :]).astype(bf16)
    qrow = lambda b, i: (b * nq + i, 0)
    full = lambda b, i: (b, 0)
    return pl.pallas_call(
        _sb_kernel,
        grid=(batch, nq),
        in_specs=[pl.BlockSpec((tq, WIDTH_B), qrow),
                  pl.BlockSpec((seq, WIDTH_B), full),
                  pl.BlockSpec((seq, WIDTH_B), full),
                  pl.BlockSpec((tq, tq), lambda b, i: (0, 0))],
        out_specs=pl.BlockSpec((tq, WIDTH_B), qrow),
        out_shape=jax.ShapeDtypeStruct((n, WIDTH_B), bf16),
        scratch_shapes=[pltpu.VMEM((N_HEADS_B, tq, LANES), f32),
                        pltpu.VMEM((N_HEADS_B, tq, HEAD_DIM), f32)],
        compiler_params=pltpu.CompilerParams(dimension_semantics=("parallel", "arbitrary"),
                                             vmem_limit_bytes=VMEM_LIMIT),
        name="sb",
    )(qb, kb, vb, tri)


def _merge_kernel(oa_ref, ob_ref, ga_ref, gb_ref, x_ref, gate_ref, shift_ref, scale_ref,
                  wa_ref, wb_ref, wo_ref, g_ref, b_ref, wr_hi_ref, wr_lo_ref, br_ref,
                  x1_ref, h2_ref, logit_ref):
    ya = _dot(oa_ref[...], wa_ref[...])
    yb = _dot(ob_ref[...], wb_ref[...])
    merged = jax.nn.sigmoid(ga_ref[...].astype(f32)) * ya + jax.nn.sigmoid(gb_ref[...].astype(f32)) * yb
    y = _dot(merged.astype(bf16), wo_ref[...])
    x1 = _ln(DN_ALPHA * x_ref[...] + gate_ref[0] * y) * g_ref[...] + b_ref[...]
    x1_ref[...] = x1
    h2 = _ln(x1) * (1.0 + scale_ref[0]) + shift_ref[0]
    hi, lo = _split_bf16(h2)
    h2_ref[...] = hi
    logit_ref[...] = (_dot(hi, wr_hi_ref[...]) + _dot(lo, wr_hi_ref[...]) + _dot(hi, wr_lo_ref[...])
                      + br_ref[...])


def _merge(oa, ob, ga, gb, x2, gate1, shift2, scale2, wa, wb, wo, g, b, wr_hi, wr_lo, br, seq):
    n, d = x2.shape
    tm = min(ROW_TILE, seq)
    per_b = seq // tm
    row = lambda i: (i, 0)
    bat = lambda i: (i // per_b, 0, 0)
    const = lambda i: (0, 0)
    return pl.pallas_call(
        _merge_kernel,
        grid=(n // tm,),
        in_specs=[pl.BlockSpec((tm, WIDTH_A), row), pl.BlockSpec((tm, WIDTH_B), row),
                  pl.BlockSpec((tm, d), row), pl.BlockSpec((tm, d), row), pl.BlockSpec((tm, d), row),
                  pl.BlockSpec((1, 1, d), bat), pl.BlockSpec((1, 1, d), bat), pl.BlockSpec((1, 1, d), bat),
                  pl.BlockSpec((WIDTH_A, d), const), pl.BlockSpec((WIDTH_B, d), const), pl.BlockSpec((d, d), const),
                  pl.BlockSpec((1, d), const), pl.BlockSpec((1, d), const),
                  pl.BlockSpec((d, LANES), const), pl.BlockSpec((d, LANES), const), pl.BlockSpec((1, LANES), const)],
        out_specs=[pl.BlockSpec((tm, d), row), pl.BlockSpec((tm, d), row), pl.BlockSpec((tm, LANES), row)],
        out_shape=[jax.ShapeDtypeStruct((n, d), f32), jax.ShapeDtypeStruct((n, d), bf16),
                   jax.ShapeDtypeStruct((n, LANES), f32)],
        compiler_params=pltpu.CompilerParams(dimension_semantics=("parallel",), vmem_limit_bytes=VMEM_LIMIT),
        name="merge",
    )(oa, ob, ga, gb, x2, gate1, shift2, scale2, wa, wb, wo, g, b, wr_hi, wr_lo, br)


def _expert_kernel(be_ref, nb_ref, x_ref, wgu_ref, bgu_ref, wdn_ref, bdn_ref, y_ref, wgu_bf, wdn_bf):
    i = pl.program_id(0)

    @pl.when(jnp.logical_or(i == 0, be_ref[i] != be_ref[jnp.maximum(i - 1, 0)]))
    def _():
        wgu_bf[...] = wgu_ref[0, 0].astype(bf16)
        wdn_bf[...] = wdn_ref[0, 0].astype(bf16)

    @pl.when(i < nb_ref[0])
    def _():
        gu = _dot(x_ref[...], wgu_bf[...]) + bgu_ref[0, 0]
        a = jnp.minimum(gu[:, :D_EXPERT], SWIGLU_LIMIT)
        u = jnp.clip(gu[:, D_EXPERT:], -SWIGLU_LIMIT, SWIGLU_LIMIT)
        act = (u + 1.0) * a * jax.nn.sigmoid(SWIGLU_ALPHA * a)
        y_ref[...] = (_dot(act.astype(bf16), wdn_bf[...]) + bdn_ref[0, 0]).astype(y_ref.dtype)

    @pl.when(i >= nb_ref[0])
    def _():
        y_ref[...] = jnp.zeros(y_ref.shape, y_ref.dtype)


def _experts(xs, block_expert, n_used, layer, wgu, bgu, wdn, bdn):
    rows, d = xs.shape
    n_blocks = rows // MOE_ROWS
    e_idx = lambda i, be, nb: (layer, be[i], 0, 0)
    grid_spec = pltpu.PrefetchScalarGridSpec(
        num_scalar_prefetch=2,
        grid=(n_blocks,),
        in_specs=[pl.BlockSpec((MOE_ROWS, d), lambda i, be, nb: (i, 0)),
                  pl.BlockSpec((1, 1, d, 2 * D_EXPERT), e_idx),
                  pl.BlockSpec((1, 1, 1, 2 * D_EXPERT), e_idx),
                  pl.BlockSpec((1, 1, D_EXPERT, d), e_idx),
                  pl.BlockSpec((1, 1, 1, d), e_idx)],
        out_specs=pl.BlockSpec((MOE_ROWS, d), lambda i, be, nb: (i, 0)),
        scratch_shapes=[pltpu.VMEM((d, 2 * D_EXPERT), bf16), pltpu.VMEM((D_EXPERT, d), bf16)],
    )
    return pl.pallas_call(
        _expert_kernel,
        grid_spec=grid_spec,
        out_shape=jax.ShapeDtypeStruct((rows, d), bf16),
        compiler_params=pltpu.CompilerParams(dimension_semantics=("arbitrary",), vmem_limit_bytes=VMEM_LIMIT),
        name="experts",
    )(block_expert, n_used, xs, wgu, bgu, wdn, bdn)


def _combine_kernel(x_ref, y_ref, gk_ref, gate_ref, g_ref, b_ref, o_ref):
    gk = gk_ref[...]
    y = y_ref[0].astype(f32) * gk[:, 0:1]
    for k in range(1, TOP_K):
        y = y + y_ref[k].astype(f32) * gk[:, k:k + 1]
    o_ref[...] = _ln(DN_ALPHA * x_ref[...] + gate_ref[0] * y) * g_ref[...] + b_ref[...]


def _combine(x1, yk, gates, gate2, g, b, seq):
    n, d = x1.shape
    tm = min(ROW_TILE, seq)
    per_b = seq // tm
    row = lambda i: (i, 0)
    const = lambda i: (0, 0)
    return pl.pallas_call(
        _combine_kernel,
        grid=(n // tm,),
        in_specs=[pl.BlockSpec((tm, d), row),
                  pl.BlockSpec((TOP_K, tm, d), lambda i: (0, i, 0)),
                  pl.BlockSpec((tm, TOP_K), row),
                  pl.BlockSpec((1, 1, d), lambda i: (i // per_b, 0, 0)),
                  pl.BlockSpec((1, d), const), pl.BlockSpec((1, d), const)],
        out_specs=pl.BlockSpec((tm, d), row),
        out_shape=jax.ShapeDtypeStruct((n, d), f32),
        compiler_params=pltpu.CompilerParams(dimension_semantics=("parallel",), vmem_limit_bytes=VMEM_LIMIT),
        name="combine",
    )(x1, yk, gates, gate2, g, b)


def _route(logits, n):
    top_val, top_idx = lax.top_k(logits, TOP_K)
    gates = jax.nn.softmax(top_val, axis=-1)
    onehot = jnp.sum(jax.nn.one_hot(top_idx, N_EXPERTS, dtype=jnp.int32), axis=1)
    before = jnp.cumsum(onehot, axis=0) - onehot
    counts = before[-1] + onehot[-1]
    padded = (counts + MOE_ROWS - 1) // MOE_ROWS * MOE_ROWS
    pad_end = jnp.cumsum(padded)
    pad_start = pad_end - padded
    dest = pad_start[top_idx] + jnp.take_along_axis(before, top_idx, axis=1)
    n_blocks = -(-(n * TOP_K + N_EXPERTS * (MOE_ROWS - 1)) // MOE_ROWS)
    block_expert = jnp.minimum(
        jnp.searchsorted(pad_end, jnp.arange(n_blocks, dtype=jnp.int32) * MOE_ROWS, side='right'),
        N_EXPERTS - 1).astype(jnp.int32)
    n_used = (pad_end[-1:] // MOE_ROWS).astype(jnp.int32)
    return gates, dest.astype(jnp.int32), block_expert, n_used, n_blocks


def _split_w_in(w):
    split = sum(COL_SPLITS[:5])
    return jnp.pad(w[:, :split], ((0, 0), (0, IDX_PAD))).astype(bf16), w[:, split:].astype(bf16)


def kernel(x, c, rel_bias, w_ada, b_ada, w_in, g_kv, w_uk, w_uv, w_a_out, w_b_out, w_o, ln1_g, ln1_b,
           w_router, b_router, w_gu, b_gu, w_dn, b_dn, ln2_g, ln2_b):
    B, S, D = x.shape
    N = B * S
    assert D == D_MODEL and S % ATT_BLOCK == 0
    cond = jax.nn.silu(c)
    x2 = x.reshape(N, D)
    for l in range(DEPTH):
        mod = jnp.dot(cond, w_ada[l], precision=lax.Precision.HIGHEST) + b_ada[l]
        shift1, scale1, gate1, shift2, scale2, gate2 = [m[:, None, :] for m in jnp.split(mod, 6, axis=-1)]

        qa, ckv, qidx, kidx, widx, qb, kb, vb, ga, gb = _proj(
            x2, shift1, scale1, *_split_w_in(w_in[l]), g_kv[l][None, :], S)
        wuk_t = jnp.transpose(w_uk[l], (1, 2, 0)).astype(bf16)
        wuv_t = jnp.transpose(w_uv[l], (1, 0, 2)).astype(bf16)
        oa = _dsa(qa, qidx, widx, kidx, ckv, wuk_t, wuv_t, rel_bias, B, S)
        ob = _sb(qb, kb, vb, B, S)

        wr = jnp.pad(w_router[l], ((0, 0), (0, LANES - N_EXPERTS)))
        wr_hi, wr_lo = _split_bf16(wr)
        br = jnp.pad(b_router[l], (0, LANES - N_EXPERTS))[None, :]
        x1, h2, logits = _merge(oa, ob, ga, gb, x2, gate1, shift2, scale2,
                                w_a_out[l].astype(bf16), w_b_out[l].astype(bf16), w_o[l].astype(bf16),
                                ln1_g[l][None, :], ln1_b[l][None, :], wr_hi, wr_lo, br, S)

        gates, dest, block_expert, n_used, n_blocks = _route(logits[:, :N_EXPERTS], N)
        rows = n_blocks * MOE_ROWS
        row_token = jnp.zeros((rows,), jnp.int32).at[dest.reshape(-1)].set(
            jnp.repeat(jnp.arange(N, dtype=jnp.int32), TOP_K))
        xs = h2[row_token]
        ys = _experts(xs, block_expert, n_used, l, w_gu, b_gu[:, :, None, :], w_dn, b_dn[:, :, None, :])
        yk = ys[dest.T]
        x2 = _combine(x1, yk, gates, gate2, ln2_g[l][None, :], ln2_b[l][None, :], S)
    return x2.reshape(B, S, D)
```

```python
import functools
import math

import numpy as np
import jax
import jax.numpy as jnp
from jax import lax
from jax.experimental import pallas as pl
from jax.experimental.pallas import tpu as pltpu

D_MODEL = 1024
HEAD_DIM = 64
N_HEADS_A = 8
WIDTH_A = N_HEADS_A * HEAD_DIM
KV_RANK = 128
N_IDX_HEADS = 4
IDX_DIM = 64
INDEX_TOPK = 256
N_HEADS_B = 8
WIDTH_B = N_HEADS_B * HEAD_DIM
N_BUCKETS = 32
MAX_DISTANCE = 128
N_EXPERTS = 32
TOP_K = 4
D_EXPERT = 1024
SWIGLU_LIMIT = 7.0
SWIGLU_ALPHA = 1.702
LN_EPS = 1e-5
RMS_EPS = 1e-6
DEPTH = 2
DN_ALPHA = (2 * DEPTH) ** 0.25
IDX_SCALE = (N_IDX_HEADS ** -0.5) * (IDX_DIM ** -0.5)
LOG2E = math.log2(math.e)

COL_SPLITS = (WIDTH_A, KV_RANK, N_IDX_HEADS * IDX_DIM, IDX_DIM, N_IDX_HEADS,
              WIDTH_B, WIDTH_B, WIDTH_B, D_MODEL, D_MODEL)
N_COLS = sum(COL_SPLITS)

LANES = 128
VMEM_LIMIT = 56 * 1024 * 1024

IDX_PAD = LANES - IDX_DIM - N_IDX_HEADS
C_QA = 0
C_CKV = C_QA + WIDTH_A
C_QIDX = C_CKV + KV_RANK
C_KW = C_QIDX + N_IDX_HEADS * IDX_DIM
C_QB = C_KW + LANES
C_KB = C_QB + WIDTH_B
C_VB = C_KB + WIDTH_B
C_GA = C_VB + WIDTH_B
C_GB = C_GA + D_MODEL
N_COLS_PAD = C_GB + D_MODEL

ROW_TILE = 512
ATT_BLOCK = 256
BISECT_ROWS = 128
MOE_ROWS = 512
NEG_MASK = -1e30
M_INIT = -1e29
SB_SKIP = -110.0
INT_MIN = np.int32(-2 ** 31)

f32 = jnp.float32
bf16 = jnp.bfloat16


def _ln(x):
    mu = jnp.mean(x, axis=-1, keepdims=True)
    xc = x - mu
    var = jnp.mean(xc * xc, axis=-1, keepdims=True)
    return xc * lax.rsqrt(var + LN_EPS)


def _dot(a, b):
    return jnp.dot(a, b, preferred_element_type=f32)


def _dot_nt(a, b):
    return lax.dot_general(a, b, (((1,), (1,)), ((), ())), preferred_element_type=f32)


def _split_bf16(x):
    hi = x.astype(bf16)
    lo = (x - hi.astype(f32)).astype(bf16)
    return hi, lo


def _proj_kernel(x_ref, shift_ref, scale_ref, wa_ref, wb_ref, gkv_ref,
                 qa_ref, ckv_ref, qidx_ref, kidx_ref, widx_ref, qb_ref, kb_ref, vb_ref, ga_ref, gb_ref):
    h = _ln(x_ref[...]) * (1.0 + scale_ref[0]) + shift_ref[0]
    hb = h.astype(bf16)

    def mm(c0, width):
        if c0 < C_QB:
            return _dot(hb, wa_ref[:, c0:c0 + width])
        return _dot(hb, wb_ref[:, c0 - C_QB:c0 - C_QB + width])

    qa_ref[...] = mm(C_QA, WIDTH_A).astype(bf16)
    ckv = mm(C_CKV, KV_RANK)
    ckv = ckv * lax.rsqrt(jnp.mean(ckv * ckv, axis=-1, keepdims=True) + RMS_EPS) * gkv_ref[...]
    ckv_ref[...] = jnp.concatenate([ckv, jnp.ones_like(ckv)], axis=1).astype(bf16)
    qidx_ref[...] = mm(C_QIDX, N_IDX_HEADS * IDX_DIM).astype(bf16)
    kw = mm(C_KW, LANES)
    kidx_ref[...] = kw[:, :IDX_DIM].astype(bf16)
    widx_ref[...] = kw * IDX_SCALE
    qb_ref[...] = mm(C_QB, WIDTH_B).astype(bf16)
    kb_ref[...] = mm(C_KB, WIDTH_B).astype(bf16)
    vb_ref[...] = mm(C_VB, WIDTH_B).astype(bf16)
    for c in range(0, D_MODEL, 512):
        ga_ref[:, c:c + 512] = mm(C_GA + c, 512).astype(bf16)
        gb_ref[:, c:c + 512] = mm(C_GB + c, 512).astype(bf16)


def _proj(x2, shift, scale, w_dsa, w_rest, g_kv, seq):
    n, d = x2.shape
    tm = min(ROW_TILE, seq)
    per_b = seq // tm
    row = lambda i: (i, 0)
    bat = lambda i: (i // per_b, 0, 0)
    const = lambda i: (0, 0)
    widths = (WIDTH_A, 2 * KV_RANK, N_IDX_HEADS * IDX_DIM, IDX_DIM, LANES, WIDTH_B, WIDTH_B, WIDTH_B, D_MODEL, D_MODEL)
    dtypes = (bf16, bf16, bf16, bf16, f32, bf16, bf16, bf16, bf16, bf16)
    return pl.pallas_call(
        _proj_kernel,
        grid=(n // tm,),
        in_specs=[pl.BlockSpec((tm, d), row),
                  pl.BlockSpec((1, 1, d), bat),
                  pl.BlockSpec((1, 1, d), bat),
                  pl.BlockSpec((d, C_QB), const),
                  pl.BlockSpec((d, N_COLS_PAD - C_QB), const),
                  pl.BlockSpec((1, KV_RANK), const)],
        out_specs=[pl.BlockSpec((tm, w), row) for w in widths],
        out_shape=[jax.ShapeDtypeStruct((n, w), dt) for w, dt in zip(widths, dtypes)],
        compiler_params=pltpu.CompilerParams(dimension_semantics=("parallel",), vmem_limit_bytes=VMEM_LIMIT),
        name="proj",
    )(x2, shift, scale, w_dsa, w_rest, g_kv)


def _dsa_kernel(qa_ref, qidx_ref, widx_ref, kidx_ref, ckv_ref, wuk_ref, wuv_ref, bias_ref, tri_ref,
                o_ref, key_ref, qlat_ref, wrep_ref, thr_ref, need_ref, ceq_ref, mb_ref, m_ref, acc_ref, *, topk):
    tq = sk = ATT_BLOCK
    i = pl.program_id(1)
    n_blocks = i + 1

    for h in range(N_HEADS_A):
        ql = _dot(qa_ref[:, h * HEAD_DIM:(h + 1) * HEAD_DIM], wuk_ref[h]) * (HEAD_DIM ** -0.5 * LOG2E)
        qlat_ref[h] = ql.astype(bf16)

    rowi = lax.broadcasted_iota(jnp.int32, (tq, sk), 0)
    coli = lax.broadcasted_iota(jnp.int32, (tq, sk), 1)
    w = widx_ref[:, IDX_DIM:IDX_DIM + N_IDX_HEADS]
    for h in range(N_IDX_HEADS):
        wrep_ref[h] = jnp.broadcast_to(w[:, h:h + 1], (tq, LANES))

    def score_pair(p, carry):
        for u in range(2):
            j = 2 * p + u
            kblk = kidx_ref[pl.ds(pl.multiple_of(jnp.minimum(j, i) * sk, sk), sk), :]
            s = None
            for h in range(N_IDX_HEADS):
                d = _dot_nt(qidx_ref[:, h * IDX_DIM:(h + 1) * IDX_DIM], kblk)
                t = jnp.maximum(d, 0.0) * jnp.concatenate([wrep_ref[h]] * (sk // LANES), axis=1)
                s = t if s is None else s + t
            s = jnp.where(s == 0.0, 0.0, s)
            s = jnp.where((j - i) * sk + coli <= rowi, s, -jnp.inf)
            bits = pltpu.bitcast(s, jnp.int32)
            key_ref[j] = bits ^ ((bits >> 31) & np.int32(0x7FFFFFFF))
        return carry

    lax.fori_loop(0, (n_blocks + 1) // 2, score_pair, 0)
    key_ref[n_blocks] = jnp.full((tq, sk), INT_MIN, jnp.int32)

    halves = range(0, tq, BISECT_ROWS)

    def count_ge(cands):
        accs = []
        for r0, cand in zip(halves, cands):
            cb = jnp.broadcast_to(cand, (BISECT_ROWS, LANES))

            def body(p, acc, r0=r0, cb=cb):
                for u in range(2):
                    for c in range(0, sk, LANES):
                        kk = key_ref[2 * p + u, r0:r0 + BISECT_ROWS, c:c + LANES]
                        acc = acc + jnp.where(kk >= cb, 1.0, 0.0)
                return acc

            accs.append(lax.fori_loop(0, (n_blocks + 1) // 2, body, jnp.zeros((BISECT_ROWS, LANES), f32)))
        return [jnp.sum(acc, axis=1, keepdims=True) for acc in accs]

    def bit_step(b, answers):
        cands = [ans + jnp.left_shift(jnp.int32(1), 31 - b) for ans in answers]
        return tuple(jnp.where(cnt >= float(topk), cand, ans)
                     for cnt, cand, ans in zip(count_ge(cands), cands, answers))

    thrs = lax.fori_loop(0, 32, bit_step, tuple(jnp.full((BISECT_ROWS, 1), INT_MIN, jnp.int32) for _ in halves))
    for r0, thr, n_gt in zip(halves, thrs, count_ge([thr + 1 for thr in thrs])):
        thr_ref[r0:r0 + BISECT_ROWS, :] = jnp.broadcast_to(thr, (BISECT_ROWS, LANES))
        need_ref[r0:r0 + BISECT_ROWS, :] = jnp.broadcast_to(float(topk) - n_gt, (BISECT_ROWS, LANES))

    m_ref[...] = jnp.full(m_ref.shape, M_INIT, f32)
    acc_ref[...] = jnp.zeros(acc_ref.shape, f32)
    ceq_ref[...] = jnp.zeros(ceq_ref.shape, f32)

    def key_rows(j):
        return pl.ds(pl.multiple_of(j * sk, sk), sk)

    def mask_block(j, slot, diag):
        thr, need, ceq = thr_ref[...], need_ref[...], ceq_ref[...]
        keys = [key_ref[j, :, c:c + LANES] for c in range(0, sk, LANES)]
        eqs = [k == thr for k in keys]
        eqf = jnp.concatenate([jnp.where(e, 1.0, 0.0) for e in eqs], axis=1).astype(bf16)
        pref = _dot(eqf, tri_ref[...])
        total = jnp.broadcast_to(pref[:, sk - 1:sk], (tq, LANES))
        for n, (k, e) in enumerate(zip(keys, eqs)):
            c = n * LANES
            sel = jnp.logical_or(k > thr, jnp.logical_and(e, pref[:, c:c + LANES] + ceq <= need))
            if diag:
                ri = lax.broadcasted_iota(jnp.int32, (tq, LANES), 0)
                ci = lax.broadcasted_iota(jnp.int32, (tq, LANES), 1)
                sel = jnp.logical_and(sel, ci + c <= ri)
            mb_ref[:, slot * sk + c:slot * sk + c + LANES] = jnp.where(sel, 0.0, NEG_MASK)
        ceq_ref[...] = ceq + total

    def attend(j0, n_blk, near_col):
        for b in range(n_blk):
            mask_block(j0 + b, b, near_col is not None and b == n_blk - 1)
        heads, blks = range(N_HEADS_A), range(n_blk)
        logits = []
        for h in heads:
            q = qlat_ref[h]
            parts = []
            for b in blks:
                lg = _dot_nt(q, ckv_ref[key_rows(j0 + b), :KV_RANK]) + mb_ref[:, b * sk:(b + 1) * sk]
                if near_col is not None:
                    lg = lg + bias_ref[h, :, near_col + b * sk:near_col + (b + 1) * sk]
                parts.append(lg)
            logits.append(parts)
        m_olds = [m_ref[h] for h in heads]
        m_news = []
        for h in heads:
            mx = functools.reduce(jnp.maximum, [p[:, c:c + LANES] for p in logits[h] for c in range(0, sk, LANES)])
            m_news.append(jnp.maximum(m_olds[h], jnp.max(mx, axis=1, keepdims=True)))
        probs = []
        for h in heads:
            m_wide = jnp.concatenate([m_news[h]] * (sk // LANES), axis=1)
            probs.append([jnp.exp2(logits[h][b] - m_wide).astype(bf16) for b in blks])
        for h in heads:
            pv = None
            for b in blks:
                d = _dot(probs[h][b], ckv_ref[key_rows(j0 + b), :])
                pv = d if pv is None else pv + d
            alpha = jnp.exp2(m_olds[h] - m_news[h])
            acc_ref[h] = jnp.concatenate([alpha, alpha], axis=1) * acc_ref[h] + pv
            m_ref[h] = m_news[h]

    lone = jnp.logical_and(i >= 2, i % 2 == 0)

    @pl.when(lone)
    def _():
        attend(0, 1, None)

    first = jnp.where(lone, 1, 0)

    def far_pair(p, carry):
        attend(first + 2 * p, 2, None)
        return carry

    lax.fori_loop(0, (i - 1 - first) // 2, far_pair, 0)

    @pl.when(i >= 1)
    def _():
        attend(i - 1, 2, 0)

    @pl.when(i == 0)
    def _():
        attend(0, 1, sk)

    outs = []
    for h in range(N_HEADS_A):
        acc = acc_ref[h]
        o_lat = acc[:, :KV_RANK] / acc[:, KV_RANK:]
        outs.append(_dot(o_lat.astype(bf16), wuv_ref[h]))
    o_ref[...] = jnp.concatenate(outs, axis=1).astype(bf16)


def _t5_bucket(n):
    max_exact = N_BUCKETS // 2
    nf = jnp.maximum(n, 1).astype(f32)
    large = max_exact + (jnp.log(nf / max_exact) / math.log(MAX_DISTANCE / max_exact)
                         * (N_BUCKETS - max_exact)).astype(jnp.int32)
    large = jnp.minimum(large, N_BUCKETS - 1)
    return jnp.where(n < max_exact, n, large)


def _dsa(qa, qidx, widx, kidx, ckv, wuk_t, wuv_t, rel_bias, batch, seq):
    n = qa.shape[0]
    tq = ATT_BLOCK
    nq = seq // tq
    topk = min(INDEX_TOPK, seq // 4)
    dist = tq + jnp.arange(tq)[:, None] - jnp.arange(2 * tq)[None, :]
    bias_near = jnp.moveaxis(rel_bias[_t5_bucket(jnp.maximum(dist, 0))], -1, 0).astype(f32)
    far_n = np.float32(tq + 1)
    assert 16 + int(np.log(far_n / 16) / math.log(MAX_DISTANCE / 16) * 16) >= N_BUCKETS - 1
    bias_near = (bias_near - rel_bias[N_BUCKETS - 1].astype(f32)[:, None, None]) * LOG2E
    tri = (jnp.arange(tq)[:, None] <= jnp.arange(tq)[None, :]).astype(bf16)

    qrow = lambda b, i: (b * nq + i, 0)
    full = lambda b, i: (b, 0)
    c3 = lambda b, i: (0, 0, 0)
    c2 = lambda b, i: (0, 0)
    kern = functools.partial(_dsa_kernel, topk=topk)
    return pl.pallas_call(
        kern,
        grid=(batch, nq),
        in_specs=[pl.BlockSpec((tq, WIDTH_A), qrow),
                  pl.BlockSpec((tq, N_IDX_HEADS * IDX_DIM), qrow),
                  pl.BlockSpec((tq, LANES), qrow),
                  pl.BlockSpec((seq, IDX_DIM), full),
                  pl.BlockSpec((seq, 2 * KV_RANK), full),
                  pl.BlockSpec((N_HEADS_A, HEAD_DIM, KV_RANK), c3),
                  pl.BlockSpec((N_HEADS_A, KV_RANK, HEAD_DIM), c3),
                  pl.BlockSpec((N_HEADS_A, tq, 2 * tq), c3),
                  pl.BlockSpec((tq, tq), c2)],
        out_specs=pl.BlockSpec((tq, WIDTH_A), qrow),
        out_shape=jax.ShapeDtypeStruct((n, WIDTH_A), bf16),
        scratch_shapes=[pltpu.VMEM((nq + 1, tq, tq), jnp.int32),
                        pltpu.VMEM((N_HEADS_A, tq, KV_RANK), bf16),
                        pltpu.VMEM((N_IDX_HEADS, tq, LANES), f32),
                        pltpu.VMEM((tq, LANES), jnp.int32),
                        pltpu.VMEM((tq, LANES), f32),
                        pltpu.VMEM((tq, LANES), f32),
                        pltpu.VMEM((tq, 2 * tq), f32),
                        pltpu.VMEM((N_HEADS_A, tq, LANES), f32),
                        pltpu.VMEM((N_HEADS_A, tq, 2 * KV_RANK), f32)],
        compiler_params=pltpu.CompilerParams(dimension_semantics=("parallel", "arbitrary"),
                                             vmem_limit_bytes=VMEM_LIMIT),
        name="dsa",
    )(qa, qidx, widx, kidx, ckv, wuk_t, wuv_t, bias_near, tri)


def _sb_kernel(q_ref, k_ref, v_ref, tri_ref, o_ref, carry_ref, acc_ref):
    tq = sk = ATT_BLOCK
    i = pl.program_id(1)
    carry_ref[...] = jnp.zeros(carry_ref.shape, f32)
    acc_ref[...] = jnp.zeros(acc_ref.shape, f32)

    def step(j, diag):
        rows = pl.ds(pl.multiple_of(j * sk, sk), sk)
        if diag:
            strict = lax.broadcasted_iota(jnp.int32, (tq, sk), 1) < lax.broadcasted_iota(jnp.int32, (tq, sk), 0)
        heads = range(N_HEADS_B)
        lanes = [slice(h * HEAD_DIM, (h + 1) * HEAD_DIM) for h in heads]
        zs = [_dot_nt(q_ref[:, lanes[h]] * (HEAD_DIM ** -0.5), k_ref[rows, lanes[h]]) for h in heads]
        lfs = [-(jnp.maximum(z, 0.0) + jnp.log(1.0 + jnp.exp(-jnp.abs(z)))) for z in zs]
        if diag:
            lfs = [jnp.where(strict, lf, 0.0) for lf in lfs]
        splits = [_split_bf16(lf) for lf in lfs]
        laters = [_dot(hi, tri_ref[...]) + _dot(lo, tri_ref[...]) for hi, lo in splits]
        carries = [carry_ref[h] for h in heads]
        probs = [jnp.exp(zs[h] + lfs[h] + laters[h] + jnp.concatenate([carries[h]] * (sk // LANES), axis=1))
                 for h in heads]
        if diag:
            probs = [jnp.where(strict, a, 0.0) for a in probs]
        top = None
        for h in heads:
            acc_ref[h] += _dot(probs[h].astype(bf16), v_ref[rows, lanes[h]])
            carry = carries[h] + jnp.sum(lfs[h], axis=1, keepdims=True)
            carry_ref[h] = carry
            top = carry if top is None else jnp.maximum(top, carry)
        return jnp.max(top)

    def cond(st):
        j, top = st
        return jnp.logical_and(j >= 0, top > SB_SKIP)

    def body(st):
        j, _ = st
        return j - 1, step(j, False)

    lax.while_loop(cond, body, (i - 1, step(i, True)))
    o_ref[...] = jnp.concatenate([acc_ref[h] for h in range(N_HEADS_B)], axis=1).astype(bf16)


def _sb(qb, kb, vb, batch, seq):
    n = qb.shape[0]
    tq = ATT_BLOCK
    nq = seq // tq
    tri = (jnp.arange(tq)[:, None] > jnp.arange(tq)[None, :]).astype(bf16)
    qrow = lambda b, i: (b * nq + i, 0)
    full = lambda b, i: (b, 0)
    return pl.pallas_call(
        _sb_kernel,
        grid=(batch, nq),
        in_specs=[pl.BlockSpec((tq, WIDTH_B), qrow),
                  pl.BlockSpec((seq, WIDTH_B), full),
                  pl.BlockSpec((seq, WIDTH_B), full),
                  pl.BlockSpec((tq, tq), lambda b, i: (0, 0))],
        out_specs=pl.BlockSpec((tq, WIDTH_B), qrow),
        out_shape=jax.ShapeDtypeStruct((n, WIDTH_B), bf16),
        scratch_shapes=[pltpu.VMEM((N_HEADS_B, tq, LANES), f32),
                        pltpu.VMEM((N_HEADS_B, tq, HEAD_DIM), f32)],
        compiler_params=pltpu.CompilerParams(dimension_semantics=("parallel", "arbitrary"),
                                             vmem_limit_bytes=VMEM_LIMIT),
        name="sb",
    )(qb, kb, vb, tri)


def _merge_kernel(oa_ref, ob_ref, ga_ref, gb_ref, x_ref, gate_ref, shift_ref, scale_ref,
                  wa_ref, wb_ref, wo_ref, g_ref, b_ref, wr_hi_ref, wr_lo_ref, br_ref,
                  x1_ref, h2_ref, logit_ref):
    ya = _dot(oa_ref[...], wa_ref[...])
    yb = _dot(ob_ref[...], wb_ref[...])
    merged = jax.nn.sigmoid(ga_ref[...].astype(f32)) * ya + jax.nn.sigmoid(gb_ref[...].astype(f32)) * yb
    y = _dot(merged.astype(bf16), wo_ref[...])
    x1 = _ln(DN_ALPHA * x_ref[...] + gate_ref[0] * y) * g_ref[...] + b_ref[...]
    x1_ref[...] = x1
    h2 = _ln(x1) * (1.0 + scale_ref[0]) + shift_ref[0]
    hi, lo = _split_bf16(h2)
    h2_ref[...] = hi
    logit_ref[...] = (_dot(hi, wr_hi_ref[...]) + _dot(lo, wr_hi_ref[...]) + _dot(hi, wr_lo_ref[...])
                      + br_ref[...])


def _merge(oa, ob, ga, gb, x2, gate1, shift2, scale2, wa, wb, wo, g, b, wr_hi, wr_lo, br, seq):
    n, d = x2.shape
    tm = min(ROW_TILE, seq)
    per_b = seq // tm
    row = lambda i: (i, 0)
    bat = lambda i: (i // per_b, 0, 0)
    const = lambda i: (0, 0)
    return pl.pallas_call(
        _merge_kernel,
        grid=(n // tm,),
        in_specs=[pl.BlockSpec((tm, WIDTH_A), row), pl.BlockSpec((tm, WIDTH_B), row),
                  pl.BlockSpec((tm, d), row), pl.BlockSpec((tm, d), row), pl.BlockSpec((tm, d), row),
                  pl.BlockSpec((1, 1, d), bat), pl.BlockSpec((1, 1, d), bat), pl.BlockSpec((1, 1, d), bat),
                  pl.BlockSpec((WIDTH_A, d), const), pl.BlockSpec((WIDTH_B, d), const), pl.BlockSpec((d, d), const),
                  pl.BlockSpec((1, d), const), pl.BlockSpec((1, d), const),
                  pl.BlockSpec((d, LANES), const), pl.BlockSpec((d, LANES), const), pl.BlockSpec((1, LANES), const)],
        out_specs=[pl.BlockSpec((tm, d), row), pl.BlockSpec((tm, d), row), pl.BlockSpec((tm, LANES), row)],
        out_shape=[jax.ShapeDtypeStruct((n, d), f32), jax.ShapeDtypeStruct((n, d), bf16),
                   jax.ShapeDtypeStruct((n, LANES), f32)],
        compiler_params=pltpu.CompilerParams(dimension_semantics=("parallel",), vmem_limit_bytes=VMEM_LIMIT),
        name="merge",
    )(oa, ob, ga, gb, x2, gate1, shift2, scale2, wa, wb, wo, g, b, wr_hi, wr_lo, br)


def _expert_kernel(be_ref, nb_ref, x_ref, wgu_ref, bgu_ref, wdn_ref, bdn_ref, y_ref, wgu_bf, wdn_bf):
    i = pl.program_id(0)

    @pl.when(jnp.logical_or(i == 0, be_ref[i] != be_ref[jnp.maximum(i - 1, 0)]))
    def _():
        wgu_bf[...] = wgu_ref[0, 0].astype(bf16)
        wdn_bf[...] = wdn_ref[0, 0].astype(bf16)

    @pl.when(i < nb_ref[0])
    def _():
        gu = _dot(x_ref[...], wgu_bf[...]) + bgu_ref[0, 0]
        a = jnp.minimum(gu[:, :D_EXPERT], SWIGLU_LIMIT)
        u = jnp.clip(gu[:, D_EXPERT:], -SWIGLU_LIMIT, SWIGLU_LIMIT)
        act = (u + 1.0) * a * jax.nn.sigmoid(SWIGLU_ALPHA * a)
        y_ref[...] = (_dot(act.astype(bf16), wdn_bf[...]) + bdn_ref[0, 0]).astype(y_ref.dtype)

    @pl.when(i >= nb_ref[0])
    def _():
        y_ref[...] = jnp.zeros(y_ref.shape, y_ref.dtype)


def _experts(xs, block_expert, n_used, layer, wgu, bgu, wdn, bdn):
    rows, d = xs.shape
    n_blocks = rows // MOE_ROWS
    e_idx = lambda i, be, nb: (layer, be[i], 0, 0)
    grid_spec = pltpu.PrefetchScalarGridSpec(
        num_scalar_prefetch=2,
        grid=(n_blocks,),
        in_specs=[pl.BlockSpec((MOE_ROWS, d), lambda i, be, nb: (i, 0)),
                  pl.BlockSpec((1, 1, d, 2 * D_EXPERT), e_idx),
                  pl.BlockSpec((1, 1, 1, 2 * D_EXPERT), e_idx),
                  pl.BlockSpec((1, 1, D_EXPERT, d), e_idx),
                  pl.BlockSpec((1, 1, 1, d), e_idx)],
        out_specs=pl.BlockSpec((MOE_ROWS, d), lambda i, be, nb: (i, 0)),
        scratch_shapes=[pltpu.VMEM((d, 2 * D_EXPERT), bf16), pltpu.VMEM((D_EXPERT, d), bf16)],
    )
    return pl.pallas_call(
        _expert_kernel,
        grid_spec=grid_spec,
        out_shape=jax.ShapeDtypeStruct((rows, d), bf16),
        compiler_params=pltpu.CompilerParams(dimension_semantics=("arbitrary",), vmem_limit_bytes=VMEM_LIMIT),
        name="experts",
    )(block_expert, n_used, xs, wgu, bgu, wdn, bdn)


def _combine_kernel(x_ref, y_ref, gk_ref, gate_ref, g_ref, b_ref, o_ref):
    gk = gk_ref[...]
    y = y_ref[0].astype(f32) * gk[:, 0:1]
    for k in range(1, TOP_K):
        y = y + y_ref[k].astype(f32) * gk[:, k:k + 1]
    o_ref[...] = _ln(DN_ALPHA * x_ref[...] + gate_ref[0] * y) * g_ref[...] + b_ref[...]


def _combine(x1, yk, gates, gate2, g, b, seq):
    n, d = x1.shape
    tm = min(ROW_TILE, seq)
    per_b = seq // tm
    row = lambda i: (i, 0)
    const = lambda i: (0, 0)
    return pl.pallas_call(
        _combine_kernel,
        grid=(n // tm,),
        in_specs=[pl.BlockSpec((tm, d), row),
                  pl.BlockSpec((TOP_K, tm, d), lambda i: (0, i, 0)),
                  pl.BlockSpec((tm, TOP_K), row),
                  pl.BlockSpec((1, 1, d), lambda i: (i // per_b, 0, 0)),
                  pl.BlockSpec((1, d), const), pl.BlockSpec((1, d), const)],
        out_specs=pl.BlockSpec((tm, d), row),
        out_shape=jax.ShapeDtypeStruct((n, d), f32),
        compiler_params=pltpu.CompilerParams(dimension_semantics=("parallel",), vmem_limit_bytes=VMEM_LIMIT),
        name="combine",
    )(x1, yk, gates, gate2, g, b)


def _route_kernel(logit_ref, tri_ref, idx_ref, gate_ref, rank_ref, count_ref, carry_ref):
    @pl.when(pl.program_id(0) == 0)
    def _():
        carry_ref[...] = jnp.zeros(carry_ref.shape, f32)

    tm = logit_ref.shape[0]
    lane = lax.broadcasted_iota(jnp.int32, (tm, LANES), 1).astype(f32)
    v = jnp.where(lane < N_EXPERTS, logit_ref[...], -jnp.inf)
    tops, picks = [], []
    for _ in range(TOP_K):
        top = jnp.max(v, axis=1, keepdims=True)
        pick = jnp.min(jnp.where(v == top, lane, float(LANES)), axis=1, keepdims=True)
        tops.append(top)
        picks.append(pick)
        v = jnp.where(lane == pick, -jnp.inf, v)
    exps = [jnp.exp(top - tops[0]) for top in tops]
    denom = functools.reduce(jnp.add, exps)
    hits = [lane == pick for pick in picks]
    onehot = functools.reduce(jnp.add, [jnp.where(hit, 1.0, 0.0) for hit in hits])
    before = _dot(tri_ref[...], onehot.astype(bf16)) + carry_ref[...]
    ranks = [jnp.sum(jnp.where(hit, before, 0.0), axis=1, keepdims=True) for hit in hits]

    def spread(cols):
        return functools.reduce(jnp.add, [jnp.where(lane == float(k), col, 0.0) for k, col in enumerate(cols)])

    idx_ref[...] = spread(picks).astype(jnp.int32)
    gate_ref[...] = spread([e / denom for e in exps])
    rank_ref[...] = spread(ranks).astype(jnp.int32)
    carry_ref[...] += jnp.sum(onehot, axis=0, keepdims=True)
    count_ref[...] = carry_ref[...]


def _route(logits):
    n = logits.shape[0]
    tm = ROW_TILE
    tri = (jnp.arange(tm)[:, None] > jnp.arange(tm)[None, :]).astype(bf16)
    row = lambda i: (i, 0)
    const = lambda i: (0, 0)
    idx, gates, rank, counts = pl.pallas_call(
        _route_kernel,
        grid=(n // tm,),
        in_specs=[pl.BlockSpec((tm, LANES), row), pl.BlockSpec((tm, tm), const)],
        out_specs=[pl.BlockSpec((tm, LANES), row), pl.BlockSpec((tm, LANES), row), pl.BlockSpec((tm, LANES), row),
                   pl.BlockSpec((1, LANES), const)],
        out_shape=[jax.ShapeDtypeStruct((n, LANES), jnp.int32), jax.ShapeDtypeStruct((n, LANES), f32),
                   jax.ShapeDtypeStruct((n, LANES), jnp.int32), jax.ShapeDtypeStruct((1, LANES), f32)],
        scratch_shapes=[pltpu.VMEM((1, LANES), f32)],
        compiler_params=pltpu.CompilerParams(dimension_semantics=("arbitrary",), vmem_limit_bytes=VMEM_LIMIT),
        name="route",
    )(logits, tri)
    top_idx, gates, rank = idx[:, :TOP_K], gates[:, :TOP_K], rank[:, :TOP_K]
    counts = counts[0, :N_EXPERTS].astype(jnp.int32)
    padded = (counts + MOE_ROWS - 1) // MOE_ROWS * MOE_ROWS
    pad_end = jnp.cumsum(padded)
    pad_start = pad_end - padded
    dest = pad_start[top_idx] + rank
    n_blocks = -(-(n * TOP_K + N_EXPERTS * (MOE_ROWS - 1)) // MOE_ROWS)
    block_expert = jnp.minimum(
        jnp.searchsorted(pad_end, jnp.arange(n_blocks, dtype=jnp.int32) * MOE_ROWS, side='right'),
        N_EXPERTS - 1).astype(jnp.int32)
    n_used = (pad_end[-1:] // MOE_ROWS).astype(jnp.int32)
    return gates, dest, block_expert, n_used, n_blocks


def _split_w_in(w):
    split = sum(COL_SPLITS[:5])
    return jnp.pad(w[:, :split], ((0, 0), (0, IDX_PAD))).astype(bf16), w[:, split:].astype(bf16)


def kernel(x, c, rel_bias, w_ada, b_ada, w_in, g_kv, w_uk, w_uv, w_a_out, w_b_out, w_o, ln1_g, ln1_b,
           w_router, b_router, w_gu, b_gu, w_dn, b_dn, ln2_g, ln2_b):
    B, S, D = x.shape
    N = B * S
    assert D == D_MODEL and S % ATT_BLOCK == 0
    cond = jax.nn.silu(c)
    x2 = x.reshape(N, D)
    for l in range(DEPTH):
        mod = jnp.dot(cond, w_ada[l], precision=lax.Precision.HIGHEST) + b_ada[l]
        shift1, scale1, gate1, shift2, scale2, gate2 = [m[:, None, :] for m in jnp.split(mod, 6, axis=-1)]

        qa, ckv, qidx, kidx, widx, qb, kb, vb, ga, gb = _proj(
            x2, shift1, scale1, *_split_w_in(w_in[l]), g_kv[l][None, :], S)
        wuk_t = jnp.transpose(w_uk[l], (1, 2, 0)).astype(bf16)
        wuv_t = jnp.transpose(w_uv[l], (1, 0, 2)).astype(bf16)
        oa = _dsa(qa, qidx, widx, kidx, ckv, wuk_t, wuv_t, rel_bias, B, S)
        ob = _sb(qb, kb, vb, B, S)

        wr = jnp.pad(w_router[l], ((0, 0), (0, LANES - N_EXPERTS)))
        wr_hi, wr_lo = _split_bf16(wr)
        br = jnp.pad(b_router[l], (0, LANES - N_EXPERTS))[None, :]
        x1, h2, logits = _merge(oa, ob, ga, gb, x2, gate1, shift2, scale2,
                                w_a_out[l].astype(bf16), w_b_out[l].astype(bf16), w_o[l].astype(bf16),
                                ln1_g[l][None, :], ln1_b[l][None, :], wr_hi, wr_lo, br, S)

        gates, dest, block_expert, n_used, n_blocks = _route(logits)
        rows = n_blocks * MOE_ROWS
        row_token = jnp.zeros((rows,), jnp.int32).at[dest.reshape(-1)].set(
            jnp.repeat(jnp.arange(N, dtype=jnp.int32), TOP_K))
        xs = h2[row_token]
        ys = _experts(xs, block_expert, n_used, l, w_gu, b_gu[:, :, None, :], w_dn, b_dn[:, :, None, :])
        yk = ys[dest.T]
        x2 = _combine(x1, yk, gates, gate2, ln2_g[l][None, :], ln2_b[l][None, :], S)
    return x2.reshape(B, S, D)
```

```python
import functools
import math

import numpy as np
import jax
import jax.numpy as jnp
from jax import lax
from jax.experimental import pallas as pl
from jax.experimental.pallas import tpu as pltpu

D_MODEL = 1024
HEAD_DIM = 64
N_HEADS_A = 8
WIDTH_A = N_HEADS_A * HEAD_DIM
KV_RANK = 128
N_IDX_HEADS = 4
IDX_DIM = 64
INDEX_TOPK = 256
N_HEADS_B = 8
WIDTH_B = N_HEADS_B * HEAD_DIM
N_BUCKETS = 32
MAX_DISTANCE = 128
N_EXPERTS = 32
TOP_K = 4
D_EXPERT = 1024
SWIGLU_LIMIT = 7.0
SWIGLU_ALPHA = 1.702
LN_EPS = 1e-5
RMS_EPS = 1e-6
DEPTH = 2
DN_ALPHA = (2 * DEPTH) ** 0.25
IDX_SCALE = (N_IDX_HEADS ** -0.5) * (IDX_DIM ** -0.5)
LOG2E = math.log2(math.e)

COL_SPLITS = (WIDTH_A, KV_RANK, N_IDX_HEADS * IDX_DIM, IDX_DIM, N_IDX_HEADS,
              WIDTH_B, WIDTH_B, WIDTH_B, D_MODEL, D_MODEL)
N_COLS = sum(COL_SPLITS)

LANES = 128
VMEM_LIMIT = 56 * 1024 * 1024

IDX_PAD = LANES - IDX_DIM - N_IDX_HEADS
C_QA = 0
C_CKV = C_QA + WIDTH_A
C_QIDX = C_CKV + KV_RANK
C_KW = C_QIDX + N_IDX_HEADS * IDX_DIM
C_QB = C_KW + LANES
C_KB = C_QB + WIDTH_B
C_VB = C_KB + WIDTH_B
C_GA = C_VB + WIDTH_B
C_GB = C_GA + D_MODEL
N_COLS_PAD = C_GB + D_MODEL

ROW_TILE = 512
ATT_BLOCK = 256
BISECT_ROWS = 128
MOE_ROWS = 512
MOE_CHUNKS = 4
NEG_MASK = -1e30
M_INIT = -1e29
SB_SKIP = -110.0
INT_MIN = np.int32(-2 ** 31)

f32 = jnp.float32
bf16 = jnp.bfloat16


def _ln(x):
    mu = jnp.mean(x, axis=-1, keepdims=True)
    xc = x - mu
    var = jnp.mean(xc * xc, axis=-1, keepdims=True)
    return xc * lax.rsqrt(var + LN_EPS)


def _dot(a, b):
    return jnp.dot(a, b, preferred_element_type=f32)


def _dot_nt(a, b):
    return lax.dot_general(a, b, (((1,), (1,)), ((), ())), preferred_element_type=f32)


def _split_bf16(x):
    hi = x.astype(bf16)
    lo = (x - hi.astype(f32)).astype(bf16)
    return hi, lo


def _proj_kernel(x_ref, shift_ref, scale_ref, wa_ref, wb_ref, gkv_ref,
                 qa_ref, ckv_ref, qidx_ref, kidx_ref, widx_ref, qb_ref, kb_ref, vb_ref, ga_ref, gb_ref):
    h = _ln(x_ref[...]) * (1.0 + scale_ref[0]) + shift_ref[0]
    hb = h.astype(bf16)

    def mm(c0, width):
        if c0 < C_QB:
            return _dot(hb, wa_ref[:, c0:c0 + width])
        return _dot(hb, wb_ref[:, c0 - C_QB:c0 - C_QB + width])

    qa_ref[...] = mm(C_QA, WIDTH_A).astype(bf16)
    ckv = mm(C_CKV, KV_RANK)
    ckv = ckv * lax.rsqrt(jnp.mean(ckv * ckv, axis=-1, keepdims=True) + RMS_EPS) * gkv_ref[...]
    ckv_ref[...] = jnp.concatenate([ckv, jnp.ones_like(ckv)], axis=1).astype(bf16)
    qidx_ref[...] = mm(C_QIDX, N_IDX_HEADS * IDX_DIM).astype(bf16)
    kw = mm(C_KW, LANES)
    kidx_ref[...] = kw[:, :IDX_DIM].astype(bf16)
    widx_ref[...] = kw * IDX_SCALE
    qb_ref[...] = mm(C_QB, WIDTH_B).astype(bf16)
    kb_ref[...] = mm(C_KB, WIDTH_B).astype(bf16)
    vb_ref[...] = mm(C_VB, WIDTH_B).astype(bf16)
    for c in range(0, D_MODEL, 512):
        ga_ref[:, c:c + 512] = mm(C_GA + c, 512).astype(bf16)
        gb_ref[:, c:c + 512] = mm(C_GB + c, 512).astype(bf16)


def _proj(x2, shift, scale, w_dsa, w_rest, g_kv, seq):
    n, d = x2.shape
    tm = min(ROW_TILE, seq)
    per_b = seq // tm
    row = lambda i: (i, 0)
    bat = lambda i: (i // per_b, 0, 0)
    const = lambda i: (0, 0)
    widths = (WIDTH_A, 2 * KV_RANK, N_IDX_HEADS * IDX_DIM, IDX_DIM, LANES, WIDTH_B, WIDTH_B, WIDTH_B, D_MODEL, D_MODEL)
    dtypes = (bf16, bf16, bf16, bf16, f32, bf16, bf16, bf16, bf16, bf16)
    return pl.pallas_call(
        _proj_kernel,
        grid=(n // tm,),
        in_specs=[pl.BlockSpec((tm, d), row),
                  pl.BlockSpec((1, 1, d), bat),
                  pl.BlockSpec((1, 1, d), bat),
                  pl.BlockSpec((d, C_QB), const),
                  pl.BlockSpec((d, N_COLS_PAD - C_QB), const),
                  pl.BlockSpec((1, KV_RANK), const)],
        out_specs=[pl.BlockSpec((tm, w), row) for w in widths],
        out_shape=[jax.ShapeDtypeStruct((n, w), dt) for w, dt in zip(widths, dtypes)],
        compiler_params=pltpu.CompilerParams(dimension_semantics=("parallel",), vmem_limit_bytes=VMEM_LIMIT),
        name="proj",
    )(x2, shift, scale, w_dsa, w_rest, g_kv)


def _dsa_kernel(qa_ref, qidx_ref, widx_ref, kidx_ref, ckv_ref, wuk_ref, wuv_ref, bias_ref, tri_ref,
                o_ref, key_ref, qlat_ref, wrep_ref, thr_ref, need_ref, ceq_ref, mb_ref, m_ref, acc_ref, *, topk):
    tq = sk = ATT_BLOCK
    i = pl.program_id(1)
    n_blocks = i + 1

    for h in range(N_HEADS_A):
        ql = _dot(qa_ref[:, h * HEAD_DIM:(h + 1) * HEAD_DIM], wuk_ref[h]) * (HEAD_DIM ** -0.5 * LOG2E)
        qlat_ref[h] = ql.astype(bf16)

    rowi = lax.broadcasted_iota(jnp.int32, (tq, sk), 0)
    coli = lax.broadcasted_iota(jnp.int32, (tq, sk), 1)
    w = widx_ref[:, IDX_DIM:IDX_DIM + N_IDX_HEADS]
    for h in range(N_IDX_HEADS):
        wrep_ref[h] = jnp.broadcast_to(w[:, h:h + 1], (tq, LANES))

    def score_pair(p, carry):
        for u in range(2):
            j = 2 * p + u
            kblk = kidx_ref[pl.ds(pl.multiple_of(jnp.minimum(j, i) * sk, sk), sk), :]
            s = None
            for h in range(N_IDX_HEADS):
                d = _dot_nt(qidx_ref[:, h * IDX_DIM:(h + 1) * IDX_DIM], kblk)
                t = jnp.maximum(d, 0.0) * jnp.concatenate([wrep_ref[h]] * (sk // LANES), axis=1)
                s = t if s is None else s + t
            s = jnp.where(s == 0.0, 0.0, s)
            s = jnp.where((j - i) * sk + coli <= rowi, s, -jnp.inf)
            bits = pltpu.bitcast(s, jnp.int32)
            key_ref[j] = bits ^ ((bits >> 31) & np.int32(0x7FFFFFFF))
        return carry

    lax.fori_loop(0, (n_blocks + 1) // 2, score_pair, 0)
    key_ref[n_blocks] = jnp.full((tq, sk), INT_MIN, jnp.int32)

    halves = range(0, tq, BISECT_ROWS)

    def count_ge(cands):
        accs = []
        for r0, cand in zip(halves, cands):
            cb = jnp.broadcast_to(cand, (BISECT_ROWS, LANES))

            def body(p, acc, r0=r0, cb=cb):
                for u in range(2):
                    for c in range(0, sk, LANES):
                        kk = key_ref[2 * p + u, r0:r0 + BISECT_ROWS, c:c + LANES]
                        acc = acc + jnp.where(kk >= cb, 1.0, 0.0)
                return acc

            accs.append(lax.fori_loop(0, (n_blocks + 1) // 2, body, jnp.zeros((BISECT_ROWS, LANES), f32)))
        return [jnp.sum(acc, axis=1, keepdims=True) for acc in accs]

    def bit_step(b, answers):
        cands = [ans + jnp.left_shift(jnp.int32(1), 31 - b) for ans in answers]
        return tuple(jnp.where(cnt >= float(topk), cand, ans)
                     for cnt, cand, ans in zip(count_ge(cands), cands, answers))

    thrs = lax.fori_loop(0, 32, bit_step, tuple(jnp.full((BISECT_ROWS, 1), INT_MIN, jnp.int32) for _ in halves))
    for r0, thr, n_gt in zip(halves, thrs, count_ge([thr + 1 for thr in thrs])):
        thr_ref[r0:r0 + BISECT_ROWS, :] = jnp.broadcast_to(thr, (BISECT_ROWS, LANES))
        need_ref[r0:r0 + BISECT_ROWS, :] = jnp.broadcast_to(float(topk) - n_gt, (BISECT_ROWS, LANES))

    m_ref[...] = jnp.full(m_ref.shape, M_INIT, f32)
    acc_ref[...] = jnp.zeros(acc_ref.shape, f32)
    ceq_ref[...] = jnp.zeros(ceq_ref.shape, f32)

    def key_rows(j):
        return pl.ds(pl.multiple_of(j * sk, sk), sk)

    def mask_block(j, slot, diag):
        thr, need, ceq = thr_ref[...], need_ref[...], ceq_ref[...]
        keys = [key_ref[j, :, c:c + LANES] for c in range(0, sk, LANES)]
        eqs = [k == thr for k in keys]
        eqf = jnp.concatenate([jnp.where(e, 1.0, 0.0) for e in eqs], axis=1).astype(bf16)
        pref = _dot(eqf, tri_ref[...])
        total = jnp.broadcast_to(pref[:, sk - 1:sk], (tq, LANES))
        for n, (k, e) in enumerate(zip(keys, eqs)):
            c = n * LANES
            sel = jnp.logical_or(k > thr, jnp.logical_and(e, pref[:, c:c + LANES] + ceq <= need))
            if diag:
                ri = lax.broadcasted_iota(jnp.int32, (tq, LANES), 0)
                ci = lax.broadcasted_iota(jnp.int32, (tq, LANES), 1)
                sel = jnp.logical_and(sel, ci + c <= ri)
            mb_ref[:, slot * sk + c:slot * sk + c + LANES] = jnp.where(sel, 0.0, NEG_MASK)
        ceq_ref[...] = ceq + total

    def attend(j0, n_blk, near_col):
        for b in range(n_blk):
            mask_block(j0 + b, b, near_col is not None and b == n_blk - 1)
        heads, blks = range(N_HEADS_A), range(n_blk)
        logits = []
        for h in heads:
            q = qlat_ref[h]
            parts = []
            for b in blks:
                lg = _dot_nt(q, ckv_ref[key_rows(j0 + b), :KV_RANK]) + mb_ref[:, b * sk:(b + 1) * sk]
                if near_col is not None:
                    lg = lg + bias_ref[h, :, near_col + b * sk:near_col + (b + 1) * sk]
                parts.append(lg)
            logits.append(parts)
        m_olds = [m_ref[h] for h in heads]
        m_news = []
        for h in heads:
            mx = functools.reduce(jnp.maximum, [p[:, c:c + LANES] for p in logits[h] for c in range(0, sk, LANES)])
            m_news.append(jnp.maximum(m_olds[h], jnp.max(mx, axis=1, keepdims=True)))
        probs = []
        for h in heads:
            m_wide = jnp.concatenate([m_news[h]] * (sk // LANES), axis=1)
            probs.append([jnp.exp2(logits[h][b] - m_wide).astype(bf16) for b in blks])
        for h in heads:
            pv = None
            for b in blks:
                d = _dot(probs[h][b], ckv_ref[key_rows(j0 + b), :])
                pv = d if pv is None else pv + d
            alpha = jnp.exp2(m_olds[h] - m_news[h])
            acc_ref[h] = jnp.concatenate([alpha, alpha], axis=1) * acc_ref[h] + pv
            m_ref[h] = m_news[h]

    lone = jnp.logical_and(i >= 2, i % 2 == 0)

    @pl.when(lone)
    def _():
        attend(0, 1, None)

    first = jnp.where(lone, 1, 0)

    def far_pair(p, carry):
        attend(first + 2 * p, 2, None)
        return carry

    lax.fori_loop(0, (i - 1 - first) // 2, far_pair, 0)

    @pl.when(i >= 1)
    def _():
        attend(i - 1, 2, 0)

    @pl.when(i == 0)
    def _():
        attend(0, 1, sk)

    outs = []
    for h in range(N_HEADS_A):
        acc = acc_ref[h]
        o_lat = acc[:, :KV_RANK] / acc[:, KV_RANK:]
        outs.append(_dot(o_lat.astype(bf16), wuv_ref[h]))
    o_ref[...] = jnp.concatenate(outs, axis=1).astype(bf16)


def _t5_bucket(n):
    max_exact = N_BUCKETS // 2
    nf = jnp.maximum(n, 1).astype(f32)
    large = max_exact + (jnp.log(nf / max_exact) / math.log(MAX_DISTANCE / max_exact)
                         * (N_BUCKETS - max_exact)).astype(jnp.int32)
    large = jnp.minimum(large, N_BUCKETS - 1)
    return jnp.where(n < max_exact, n, large)


def _dsa(qa, qidx, widx, kidx, ckv, wuk_t, wuv_t, rel_bias, batch, seq):
    n = qa.shape[0]
    tq = ATT_BLOCK
    nq = seq // tq
    topk = min(INDEX_TOPK, seq // 4)
    dist = tq + jnp.arange(tq)[:, None] - jnp.arange(2 * tq)[None, :]
    bias_near = jnp.moveaxis(rel_bias[_t5_bucket(jnp.maximum(dist, 0))], -1, 0).astype(f32)
    far_n = np.float32(tq + 1)
    assert 16 + int(np.log(far_n / 16) / math.log(MAX_DISTANCE / 16) * 16) >= N_BUCKETS - 1
    bias_near = (bias_near - rel_bias[N_BUCKETS - 1].astype(f32)[:, None, None]) * LOG2E
    tri = (jnp.arange(tq)[:, None] <= jnp.arange(tq)[None, :]).astype(bf16)

    qrow = lambda b, i: (b * nq + i, 0)
    full = lambda b, i: (b, 0)
    c3 = lambda b, i: (0, 0, 0)
    c2 = lambda b, i: (0, 0)
    kern = functools.partial(_dsa_kernel, topk=topk)
    return pl.pallas_call(
        kern,
        grid=(batch, nq),
        in_specs=[pl.BlockSpec((tq, WIDTH_A), qrow),
                  pl.BlockSpec((tq, N_IDX_HEADS * IDX_DIM), qrow),
                  pl.BlockSpec((tq, LANES), qrow),
                  pl.BlockSpec((seq, IDX_DIM), full),
                  pl.BlockSpec((seq, 2 * KV_RANK), full),
                  pl.BlockSpec((N_HEADS_A, HEAD_DIM, KV_RANK), c3),
                  pl.BlockSpec((N_HEADS_A, KV_RANK, HEAD_DIM), c3),
                  pl.BlockSpec((N_HEADS_A, tq, 2 * tq), c3),
                  pl.BlockSpec((tq, tq), c2)],
        out_specs=pl.BlockSpec((tq, WIDTH_A), qrow),
        out_shape=jax.ShapeDtypeStruct((n, WIDTH_A), bf16),
        scratch_shapes=[pltpu.VMEM((nq + 1, tq, tq), jnp.int32),
                        pltpu.VMEM((N_HEADS_A, tq, KV_RANK), bf16),
                        pltpu.VMEM((N_IDX_HEADS, tq, LANES), f32),
                        pltpu.VMEM((tq, LANES), jnp.int32),
                        pltpu.VMEM((tq, LANES), f32),
                        pltpu.VMEM((tq, LANES), f32),
                        pltpu.VMEM((tq, 2 * tq), f32),
                        pltpu.VMEM((N_HEADS_A, tq, LANES), f32),
                        pltpu.VMEM((N_HEADS_A, tq, 2 * KV_RANK), f32)],
        compiler_params=pltpu.CompilerParams(dimension_semantics=("parallel", "arbitrary"),
                                             vmem_limit_bytes=VMEM_LIMIT),
        name="dsa",
    )(qa, qidx, widx, kidx, ckv, wuk_t, wuv_t, bias_near, tri)


def _sb_kernel(q_ref, k_ref, v_ref, tri_ref, o_ref, carry_ref, acc_ref):
    tq = sk = ATT_BLOCK
    i = pl.program_id(1)
    carry_ref[...] = jnp.zeros(carry_ref.shape, f32)
    acc_ref[...] = jnp.zeros(acc_ref.shape, f32)

    def step(j, diag):
        rows = pl.ds(pl.multiple_of(j * sk, sk), sk)
        if diag:
            strict = lax.broadcasted_iota(jnp.int32, (tq, sk), 1) < lax.broadcasted_iota(jnp.int32, (tq, sk), 0)
        heads = range(N_HEADS_B)
        lanes = [slice(h * HEAD_DIM, (h + 1) * HEAD_DIM) for h in heads]
        zs = [_dot_nt(q_ref[:, lanes[h]] * (HEAD_DIM ** -0.5), k_ref[rows, lanes[h]]) for h in heads]
        lfs = [-(jnp.maximum(z, 0.0) + jnp.log(1.0 + jnp.exp(-jnp.abs(z)))) for z in zs]
        if diag:
            lfs = [jnp.where(strict, lf, 0.0) for lf in lfs]
        splits = [_split_bf16(lf) for lf in lfs]
        laters = [_dot(hi, tri_ref[...]) + _dot(lo, tri_ref[...]) for hi, lo in splits]
        carries = [carry_ref[h] for h in heads]
        probs = [jnp.exp(zs[h] + lfs[h] + laters[h] + jnp.concatenate([carries[h]] * (sk // LANES), axis=1))
                 for h in heads]
        if diag:
            probs = [jnp.where(strict, a, 0.0) for a in probs]
        top = None
        for h in heads:
            acc_ref[h] += _dot(probs[h].astype(bf16), v_ref[rows, lanes[h]])
            carry = carries[h] + jnp.sum(lfs[h], axis=1, keepdims=True)
            carry_ref[h] = carry
            top = carry if top is None else jnp.maximum(top, carry)
        return jnp.max(top)

    def cond(st):
        j, top = st
        return jnp.logical_and(j >= 0, top > SB_SKIP)

    def body(st):
        j, _ = st
        return j - 1, step(j, False)

    lax.while_loop(cond, body, (i - 1, step(i, True)))
    o_ref[...] = jnp.concatenate([acc_ref[h] for h in range(N_HEADS_B)], axis=1).astype(bf16)


def _sb(qb, kb, vb, batch, seq):
    n = qb.shape[0]
    tq = ATT_BLOCK
    nq = seq // tq
    tri = (jnp.arange(tq)[:, None] > jnp.arange(tq)[None, :]).astype(bf16)
    qrow = lambda b, i: (b * nq + i, 0)
    full = lambda b, i: (b, 0)
    return pl.pallas_call(
        _sb_kernel,
        grid=(batch, nq),
        in_specs=[pl.BlockSpec((tq, WIDTH_B), qrow),
                  pl.BlockSpec((seq, WIDTH_B), full),
                  pl.BlockSpec((seq, WIDTH_B), full),
                  pl.BlockSpec((tq, tq), lambda b, i: (0, 0))],
        out_specs=pl.BlockSpec((tq, WIDTH_B), qrow),
        out_shape=jax.ShapeDtypeStruct((n, WIDTH_B), bf16),
        scratch_shapes=[pltpu.VMEM((N_HEADS_B, tq, LANES), f32),
                        pltpu.VMEM((N_HEADS_B, tq, HEAD_DIM), f32)],
        compiler_params=pltpu.CompilerParams(dimension_semantics=("parallel", "arbitrary"),
                                             vmem_limit_bytes=VMEM_LIMIT),
        name="sb",
    )(qb, kb, vb, tri)


def _merge_kernel(oa_ref, ob_ref, ga_ref, gb_ref, x_ref, gate_ref, shift_ref, scale_ref,
                  wa_ref, wb_ref, wo_ref, g_ref, b_ref, wr_hi_ref, wr_lo_ref, br_ref,
                  x1_ref, h2_ref, logit_ref):
    ya = _dot(oa_ref[...], wa_ref[...])
    yb = _dot(ob_ref[...], wb_ref[...])
    merged = jax.nn.sigmoid(ga_ref[...].astype(f32)) * ya + jax.nn.sigmoid(gb_ref[...].astype(f32)) * yb
    y = _dot(merged.astype(bf16), wo_ref[...])
    x1 = _ln(DN_ALPHA * x_ref[...] + gate_ref[0] * y) * g_ref[...] + b_ref[...]
    x1_ref[...] = x1
    h2 = _ln(x1) * (1.0 + scale_ref[0]) + shift_ref[0]
    hi, lo = _split_bf16(h2)
    h2_ref[...] = hi
    logit_ref[...] = (_dot(hi, wr_hi_ref[...]) + _dot(lo, wr_hi_ref[...]) + _dot(hi, wr_lo_ref[...])
                      + br_ref[...])


def _merge(oa, ob, ga, gb, x2, gate1, shift2, scale2, wa, wb, wo, g, b, wr_hi, wr_lo, br, seq):
    n, d = x2.shape
    tm = min(ROW_TILE, seq)
    per_b = seq // tm
    row = lambda i: (i, 0)
    bat = lambda i: (i // per_b, 0, 0)
    const = lambda i: (0, 0)
    return pl.pallas_call(
        _merge_kernel,
        grid=(n // tm,),
        in_specs=[pl.BlockSpec((tm, WIDTH_A), row), pl.BlockSpec((tm, WIDTH_B), row),
                  pl.BlockSpec((tm, d), row), pl.BlockSpec((tm, d), row), pl.BlockSpec((tm, d), row),
                  pl.BlockSpec((1, 1, d), bat), pl.BlockSpec((1, 1, d), bat), pl.BlockSpec((1, 1, d), bat),
                  pl.BlockSpec((WIDTH_A, d), const), pl.BlockSpec((WIDTH_B, d), const), pl.BlockSpec((d, d), const),
                  pl.BlockSpec((1, d), const), pl.BlockSpec((1, d), const),
                  pl.BlockSpec((d, LANES), const), pl.BlockSpec((d, LANES), const), pl.BlockSpec((1, LANES), const)],
        out_specs=[pl.BlockSpec((tm, d), row), pl.BlockSpec((tm, d), row), pl.BlockSpec((tm, LANES), row)],
        out_shape=[jax.ShapeDtypeStruct((n, d), f32), jax.ShapeDtypeStruct((n, d), bf16),
                   jax.ShapeDtypeStruct((n, LANES), f32)],
        compiler_params=pltpu.CompilerParams(dimension_semantics=("parallel",), vmem_limit_bytes=VMEM_LIMIT),
        name="merge",
    )(oa, ob, ga, gb, x2, gate1, shift2, scale2, wa, wb, wo, g, b, wr_hi, wr_lo, br)


def _expert_kernel(be_ref, nb_ref, x_ref, wgu_ref, bgu_ref, wdn_ref, bdn_ref, *rest, block0):
    y_ref, wgu_bf, wdn_bf = rest[-3:]
    i = pl.program_id(0)
    g = i + block0

    @pl.when(jnp.logical_or(i == 0, be_ref[g] != be_ref[jnp.maximum(g - 1, 0)]))
    def _():
        wgu_bf[...] = wgu_ref[0, 0].astype(bf16)
        wdn_bf[...] = wdn_ref[0, 0].astype(bf16)

    @pl.when(g < nb_ref[0])
    def _():
        gu = _dot(x_ref[...], wgu_bf[...]) + bgu_ref[0, 0]
        a = jnp.minimum(gu[:, :D_EXPERT], SWIGLU_LIMIT)
        u = jnp.clip(gu[:, D_EXPERT:], -SWIGLU_LIMIT, SWIGLU_LIMIT)
        act = (u + 1.0) * a * jax.nn.sigmoid(SWIGLU_ALPHA * a)
        y_ref[...] = (_dot(act.astype(bf16), wdn_bf[...]) + bdn_ref[0, 0]).astype(y_ref.dtype)

    @pl.when(g >= nb_ref[0])
    def _():
        y_ref[...] = jnp.zeros(y_ref.shape, y_ref.dtype)


def _experts(h2, row_token, block_expert, n_used, layer, wgu, bgu, wdn, bdn):
    rows, d = row_token.shape[0], h2.shape[1]
    n_blocks = rows // MOE_ROWS
    bounds = [n_blocks * c // MOE_CHUNKS for c in range(MOE_CHUNKS + 1)]
    ys = None
    for b0, b1 in zip(bounds[:-1], bounds[1:]):
        xs = h2[row_token[b0 * MOE_ROWS:b1 * MOE_ROWS]]
        e_idx = lambda i, be, nb, b0=b0: (layer, be[i + b0], 0, 0)
        in_specs = [pl.BlockSpec((MOE_ROWS, d), lambda i, be, nb: (i, 0)),
                    pl.BlockSpec((1, 1, d, 2 * D_EXPERT), e_idx),
                    pl.BlockSpec((1, 1, 1, 2 * D_EXPERT), e_idx),
                    pl.BlockSpec((1, 1, D_EXPERT, d), e_idx),
                    pl.BlockSpec((1, 1, 1, d), e_idx)]
        args = [block_expert, n_used, xs, wgu, bgu, wdn, bdn]
        aliases = {}
        if ys is not None:
            in_specs.append(pl.BlockSpec(memory_space=pl.ANY))
            aliases = {len(args): 0}
            args.append(ys)
        grid_spec = pltpu.PrefetchScalarGridSpec(
            num_scalar_prefetch=2,
            grid=(b1 - b0,),
            in_specs=in_specs,
            out_specs=pl.BlockSpec((MOE_ROWS, d), lambda i, be, nb, b0=b0: (i + b0, 0)),
            scratch_shapes=[pltpu.VMEM((d, 2 * D_EXPERT), bf16), pltpu.VMEM((D_EXPERT, d), bf16)],
        )
        ys = pl.pallas_call(
            functools.partial(_expert_kernel, block0=b0),
            grid_spec=grid_spec,
            out_shape=jax.ShapeDtypeStruct((rows, d), bf16),
            input_output_aliases=aliases,
            compiler_params=pltpu.CompilerParams(dimension_semantics=("arbitrary",), vmem_limit_bytes=VMEM_LIMIT),
            name="experts",
        )(*args)
    return ys


def _combine_kernel(x_ref, y_ref, gk_ref, gate_ref, g_ref, b_ref, o_ref):
    gk = gk_ref[...]
    y = y_ref[0].astype(f32) * gk[:, 0:1]
    for k in range(1, TOP_K):
        y = y + y_ref[k].astype(f32) * gk[:, k:k + 1]
    o_ref[...] = _ln(DN_ALPHA * x_ref[...] + gate_ref[0] * y) * g_ref[...] + b_ref[...]


def _combine(x1, yk, gates, gate2, g, b, seq):
    n, d = x1.shape
    tm = min(ROW_TILE, seq)
    per_b = seq // tm
    row = lambda i: (i, 0)
    const = lambda i: (0, 0)
    return pl.pallas_call(
        _combine_kernel,
        grid=(n // tm,),
        in_specs=[pl.BlockSpec((tm, d), row),
                  pl.BlockSpec((TOP_K, tm, d), lambda i: (0, i, 0)),
                  pl.BlockSpec((tm, TOP_K), row),
                  pl.BlockSpec((1, 1, d), lambda i: (i // per_b, 0, 0)),
                  pl.BlockSpec((1, d), const), pl.BlockSpec((1, d), const)],
        out_specs=pl.BlockSpec((tm, d), row),
        out_shape=jax.ShapeDtypeStruct((n, d), f32),
        compiler_params=pltpu.CompilerParams(dimension_semantics=("parallel",), vmem_limit_bytes=VMEM_LIMIT),
        name="combine",
    )(x1, yk, gates, gate2, g, b)


def _route_kernel(logit_ref, tri_ref, idx_ref, gate_ref, rank_ref, count_ref, carry_ref):
    @pl.when(pl.program_id(0) == 0)
    def _():
        carry_ref[...] = jnp.zeros(carry_ref.shape, f32)

    tm = logit_ref.shape[0]
    lane = lax.broadcasted_iota(jnp.int32, (tm, LANES), 1).astype(f32)
    v = jnp.where(lane < N_EXPERTS, logit_ref[...], -jnp.inf)
    tops, picks = [], []
    for _ in range(TOP_K):
        top = jnp.max(v, axis=1, keepdims=True)
        pick = jnp.min(jnp.where(v == top, lane, float(LANES)), axis=1, keepdims=True)
        tops.append(top)
        picks.append(pick)
        v = jnp.where(lane == pick, -jnp.inf, v)
    exps = [jnp.exp(top - tops[0]) for top in tops]
    denom = functools.reduce(jnp.add, exps)
    hits = [lane == pick for pick in picks]
    onehot = functools.reduce(jnp.add, [jnp.where(hit, 1.0, 0.0) for hit in hits])
    before = _dot(tri_ref[...], onehot.astype(bf16)) + carry_ref[...]
    ranks = [jnp.sum(jnp.where(hit, before, 0.0), axis=1, keepdims=True) for hit in hits]

    def spread(cols):
        return functools.reduce(jnp.add, [jnp.where(lane == float(k), col, 0.0) for k, col in enumerate(cols)])

    idx_ref[...] = spread(picks).astype(jnp.int32)
    gate_ref[...] = spread([e / denom for e in exps])
    rank_ref[...] = spread(ranks).astype(jnp.int32)
    carry_ref[...] += jnp.sum(onehot, axis=0, keepdims=True)
    count_ref[...] = carry_ref[...]


def _route(logits):
    n = logits.shape[0]
    tm = ROW_TILE
    tri = (jnp.arange(tm)[:, None] > jnp.arange(tm)[None, :]).astype(bf16)
    row = lambda i: (i, 0)
    const = lambda i: (0, 0)
    idx, gates, rank, counts = pl.pallas_call(
        _route_kernel,
        grid=(n // tm,),
        in_specs=[pl.BlockSpec((tm, LANES), row), pl.BlockSpec((tm, tm), const)],
        out_specs=[pl.BlockSpec((tm, LANES), row), pl.BlockSpec((tm, LANES), row), pl.BlockSpec((tm, LANES), row),
                   pl.BlockSpec((1, LANES), const)],
        out_shape=[jax.ShapeDtypeStruct((n, LANES), jnp.int32), jax.ShapeDtypeStruct((n, LANES), f32),
                   jax.ShapeDtypeStruct((n, LANES), jnp.int32), jax.ShapeDtypeStruct((1, LANES), f32)],
        scratch_shapes=[pltpu.VMEM((1, LANES), f32)],
        compiler_params=pltpu.CompilerParams(dimension_semantics=("arbitrary",), vmem_limit_bytes=VMEM_LIMIT),
        name="route",
    )(logits, tri)
    top_idx, gates, rank = idx[:, :TOP_K], gates[:, :TOP_K], rank[:, :TOP_K]
    counts = counts[0, :N_EXPERTS].astype(jnp.int32)
    padded = (counts + MOE_ROWS - 1) // MOE_ROWS * MOE_ROWS
    pad_end = jnp.cumsum(padded)
    pad_start = pad_end - padded
    dest = pad_start[top_idx] + rank
    n_blocks = -(-(n * TOP_K + N_EXPERTS * (MOE_ROWS - 1)) // MOE_ROWS)
    block_expert = jnp.minimum(
        jnp.searchsorted(pad_end, jnp.arange(n_blocks, dtype=jnp.int32) * MOE_ROWS, side='right'),
        N_EXPERTS - 1).astype(jnp.int32)
    n_used = (pad_end[-1:] // MOE_ROWS).astype(jnp.int32)
    return gates, dest, block_expert, n_used, n_blocks


def _split_w_in(w):
    split = sum(COL_SPLITS[:5])
    return jnp.pad(w[:, :split], ((0, 0), (0, IDX_PAD))).astype(bf16), w[:, split:].astype(bf16)


def kernel(x, c, rel_bias, w_ada, b_ada, w_in, g_kv, w_uk, w_uv, w_a_out, w_b_out, w_o, ln1_g, ln1_b,
           w_router, b_router, w_gu, b_gu, w_dn, b_dn, ln2_g, ln2_b):
    B, S, D = x.shape
    N = B * S
    assert D == D_MODEL and S % ATT_BLOCK == 0
    cond = jax.nn.silu(c)
    x2 = x.reshape(N, D)
    for l in range(DEPTH):
        mod = jnp.dot(cond, w_ada[l], precision=lax.Precision.HIGHEST) + b_ada[l]
        shift1, scale1, gate1, shift2, scale2, gate2 = [m[:, None, :] for m in jnp.split(mod, 6, axis=-1)]

        qa, ckv, qidx, kidx, widx, qb, kb, vb, ga, gb = _proj(
            x2, shift1, scale1, *_split_w_in(w_in[l]), g_kv[l][None, :], S)
        wuk_t = jnp.transpose(w_uk[l], (1, 2, 0)).astype(bf16)
        wuv_t = jnp.transpose(w_uv[l], (1, 0, 2)).astype(bf16)
        oa = _dsa(qa, qidx, widx, kidx, ckv, wuk_t, wuv_t, rel_bias, B, S)
        ob = _sb(qb, kb, vb, B, S)

        wr = jnp.pad(w_router[l], ((0, 0), (0, LANES - N_EXPERTS)))
        wr_hi, wr_lo = _split_bf16(wr)
        br = jnp.pad(b_router[l], (0, LANES - N_EXPERTS))[None, :]
        x1, h2, logits = _merge(oa, ob, ga, gb, x2, gate1, shift2, scale2,
                                w_a_out[l].astype(bf16), w_b_out[l].astype(bf16), w_o[l].astype(bf16),
                                ln1_g[l][None, :], ln1_b[l][None, :], wr_hi, wr_lo, br, S)

        gates, dest, block_expert, n_used, n_blocks = _route(logits)
        rows = n_blocks * MOE_ROWS
        row_token = jnp.zeros((rows,), jnp.int32).at[dest.reshape(-1)].set(
            jnp.repeat(jnp.arange(N, dtype=jnp.int32), TOP_K))
        ys = _experts(h2, row_token, block_expert, n_used, l, w_gu, b_gu[:, :, None, :], w_dn, b_dn[:, :, None, :])
        yk = ys[dest.T]
        x2 = _combine(x1, yk, gates, gate2, ln2_g[l][None, :], ln2_b[l][None, :], S)
    return x2.reshape(B, S, D)
```

```python
import functools
import math

import numpy as np
import jax
import jax.numpy as jnp
from jax import lax
from jax.experimental import pallas as pl
from jax.experimental.pallas import tpu as pltpu

D_MODEL = 1024
HEAD_DIM = 64
N_HEADS_A = 8
WIDTH_A = N_HEADS_A * HEAD_DIM
KV_RANK = 128
N_IDX_HEADS = 4
IDX_DIM = 64
INDEX_TOPK = 256
N_HEADS_B = 8
WIDTH_B = N_HEADS_B * HEAD_DIM
N_BUCKETS = 32
MAX_DISTANCE = 128
N_EXPERTS = 32
TOP_K = 4
D_EXPERT = 1024
SWIGLU_LIMIT = 7.0
SWIGLU_ALPHA = 1.702
LN_EPS = 1e-5
RMS_EPS = 1e-6
DEPTH = 2
DN_ALPHA = (2 * DEPTH) ** 0.25
IDX_SCALE = (N_IDX_HEADS ** -0.5) * (IDX_DIM ** -0.5)
LOG2E = math.log2(math.e)

COL_SPLITS = (WIDTH_A, KV_RANK, N_IDX_HEADS * IDX_DIM, IDX_DIM, N_IDX_HEADS,
              WIDTH_B, WIDTH_B, WIDTH_B, D_MODEL, D_MODEL)
N_COLS = sum(COL_SPLITS)

LANES = 128
VMEM_LIMIT = 56 * 1024 * 1024

IDX_PAD = LANES - IDX_DIM - N_IDX_HEADS
C_QA = 0
C_CKV = C_QA + WIDTH_A
C_QIDX = C_CKV + KV_RANK
C_KW = C_QIDX + N_IDX_HEADS * IDX_DIM
C_QB = C_KW + LANES
C_KB = C_QB + WIDTH_B
C_VB = C_KB + WIDTH_B
C_GA = C_VB + WIDTH_B
C_GB = C_GA + D_MODEL
N_COLS_PAD = C_GB + D_MODEL

ROW_TILE = 512
ATT_BLOCK = 256
BISECT_ROWS = 128
MOE_ROWS = 512
MOE_CHUNKS = 4
COMBINE_CHUNKS = 4
NEG_MASK = -1e30
M_INIT = -1e29
SB_SKIP = -110.0
INT_MIN = np.int32(-2 ** 31)

f32 = jnp.float32
bf16 = jnp.bfloat16


def _ln(x):
    mu = jnp.mean(x, axis=-1, keepdims=True)
    xc = x - mu
    var = jnp.mean(xc * xc, axis=-1, keepdims=True)
    return xc * lax.rsqrt(var + LN_EPS)


def _dot(a, b):
    return jnp.dot(a, b, preferred_element_type=f32)


def _dot_nt(a, b):
    return lax.dot_general(a, b, (((1,), (1,)), ((), ())), preferred_element_type=f32)


def _split_bf16(x):
    hi = x.astype(bf16)
    lo = (x - hi.astype(f32)).astype(bf16)
    return hi, lo


def _proj_kernel(x_ref, shift_ref, scale_ref, wa_ref, wb_ref, gkv_ref,
                 qa_ref, ckv_ref, qidx_ref, kidx_ref, widx_ref, qb_ref, kb_ref, vb_ref, ga_ref, gb_ref):
    h = _ln(x_ref[...]) * (1.0 + scale_ref[0]) + shift_ref[0]
    hb = h.astype(bf16)

    def mm(c0, width):
        if c0 < C_QB:
            return _dot(hb, wa_ref[:, c0:c0 + width])
        return _dot(hb, wb_ref[:, c0 - C_QB:c0 - C_QB + width])

    qa_ref[...] = mm(C_QA, WIDTH_A).astype(bf16)
    ckv = mm(C_CKV, KV_RANK)
    ckv = ckv * lax.rsqrt(jnp.mean(ckv * ckv, axis=-1, keepdims=True) + RMS_EPS) * gkv_ref[...]
    ckv_ref[...] = jnp.concatenate([ckv, jnp.ones_like(ckv)], axis=1).astype(bf16)
    qidx_ref[...] = mm(C_QIDX, N_IDX_HEADS * IDX_DIM).astype(bf16)
    kw = mm(C_KW, LANES)
    kidx_ref[...] = kw[:, :IDX_DIM].astype(bf16)
    widx_ref[...] = kw * IDX_SCALE
    qb_ref[...] = mm(C_QB, WIDTH_B).astype(bf16)
    kb_ref[...] = mm(C_KB, WIDTH_B).astype(bf16)
    vb_ref[...] = mm(C_VB, WIDTH_B).astype(bf16)
    for c in range(0, D_MODEL, 512):
        ga_ref[:, c:c + 512] = mm(C_GA + c, 512).astype(bf16)
        gb_ref[:, c:c + 512] = mm(C_GB + c, 512).astype(bf16)


def _proj(x2, shift, scale, w_dsa, w_rest, g_kv, seq):
    n, d = x2.shape
    tm = min(ROW_TILE, seq)
    per_b = seq // tm
    row = lambda i: (i, 0)
    bat = lambda i: (i // per_b, 0, 0)
    const = lambda i: (0, 0)
    widths = (WIDTH_A, 2 * KV_RANK, N_IDX_HEADS * IDX_DIM, IDX_DIM, LANES, WIDTH_B, WIDTH_B, WIDTH_B, D_MODEL, D_MODEL)
    dtypes = (bf16, bf16, bf16, bf16, f32, bf16, bf16, bf16, bf16, bf16)
    return pl.pallas_call(
        _proj_kernel,
        grid=(n // tm,),
        in_specs=[pl.BlockSpec((tm, d), row),
                  pl.BlockSpec((1, 1, d), bat),
                  pl.BlockSpec((1, 1, d), bat),
                  pl.BlockSpec((d, C_QB), const),
                  pl.BlockSpec((d, N_COLS_PAD - C_QB), const),
                  pl.BlockSpec((1, KV_RANK), const)],
        out_specs=[pl.BlockSpec((tm, w), row) for w in widths],
        out_shape=[jax.ShapeDtypeStruct((n, w), dt) for w, dt in zip(widths, dtypes)],
        compiler_params=pltpu.CompilerParams(dimension_semantics=("parallel",), vmem_limit_bytes=VMEM_LIMIT),
        name="proj",
    )(x2, shift, scale, w_dsa, w_rest, g_kv)


def _dsa_kernel(qa_ref, qidx_ref, widx_ref, kidx_ref, ckv_ref, wuk_ref, wuv_ref, bias_ref, tri_ref,
                o_ref, key_ref, qlat_ref, wrep_ref, thr_ref, need_ref, ceq_ref, mb_ref, m_ref, acc_ref, *, topk):
    tq = sk = ATT_BLOCK
    i = pl.program_id(1)
    n_blocks = i + 1

    for h in range(N_HEADS_A):
        ql = _dot(qa_ref[:, h * HEAD_DIM:(h + 1) * HEAD_DIM], wuk_ref[h]) * (HEAD_DIM ** -0.5 * LOG2E)
        qlat_ref[h] = ql.astype(bf16)

    rowi = lax.broadcasted_iota(jnp.int32, (tq, sk), 0)
    coli = lax.broadcasted_iota(jnp.int32, (tq, sk), 1)
    w = widx_ref[:, IDX_DIM:IDX_DIM + N_IDX_HEADS]
    for h in range(N_IDX_HEADS):
        wrep_ref[h] = jnp.broadcast_to(w[:, h:h + 1], (tq, LANES))

    def score_pair(p, carry):
        for u in range(2):
            j = 2 * p + u
            kblk = kidx_ref[pl.ds(pl.multiple_of(jnp.minimum(j, i) * sk, sk), sk), :]
            s = None
            for h in range(N_IDX_HEADS):
                d = _dot_nt(qidx_ref[:, h * IDX_DIM:(h + 1) * IDX_DIM], kblk)
                t = jnp.maximum(d, 0.0) * jnp.concatenate([wrep_ref[h]] * (sk // LANES), axis=1)
                s = t if s is None else s + t
            s = jnp.where(s == 0.0, 0.0, s)
            s = jnp.where((j - i) * sk + coli <= rowi, s, -jnp.inf)
            bits = pltpu.bitcast(s, jnp.int32)
            key_ref[j] = bits ^ ((bits >> 31) & np.int32(0x7FFFFFFF))
        return carry

    lax.fori_loop(0, (n_blocks + 1) // 2, score_pair, 0)
    key_ref[n_blocks] = jnp.full((tq, sk), INT_MIN, jnp.int32)

    halves = range(0, tq, BISECT_ROWS)

    def count_ge(cands):
        accs = []
        for r0, cand in zip(halves, cands):
            cb = jnp.broadcast_to(cand, (BISECT_ROWS, LANES))

            def body(p, acc, r0=r0, cb=cb):
                for u in range(2):
                    for c in range(0, sk, LANES):
                        kk = key_ref[2 * p + u, r0:r0 + BISECT_ROWS, c:c + LANES]
                        acc = acc + jnp.where(kk >= cb, 1.0, 0.0)
                return acc

            accs.append(lax.fori_loop(0, (n_blocks + 1) // 2, body, jnp.zeros((BISECT_ROWS, LANES), f32)))
        return [jnp.sum(acc, axis=1, keepdims=True) for acc in accs]

    def bit_step(b, answers):
        cands = [ans + jnp.left_shift(jnp.int32(1), 31 - b) for ans in answers]
        return tuple(jnp.where(cnt >= float(topk), cand, ans)
                     for cnt, cand, ans in zip(count_ge(cands), cands, answers))

    thrs = lax.fori_loop(0, 32, bit_step, tuple(jnp.full((BISECT_ROWS, 1), INT_MIN, jnp.int32) for _ in halves))
    for r0, thr, n_gt in zip(halves, thrs, count_ge([thr + 1 for thr in thrs])):
        thr_ref[r0:r0 + BISECT_ROWS, :] = jnp.broadcast_to(thr, (BISECT_ROWS, LANES))
        need_ref[r0:r0 + BISECT_ROWS, :] = jnp.broadcast_to(float(topk) - n_gt, (BISECT_ROWS, LANES))

    m_ref[...] = jnp.full(m_ref.shape, M_INIT, f32)
    acc_ref[...] = jnp.zeros(acc_ref.shape, f32)
    ceq_ref[...] = jnp.zeros(ceq_ref.shape, f32)

    def key_rows(j):
        return pl.ds(pl.multiple_of(j * sk, sk), sk)

    def mask_block(j, slot, diag):
        thr, need, ceq = thr_ref[...], need_ref[...], ceq_ref[...]
        keys = [key_ref[j, :, c:c + LANES] for c in range(0, sk, LANES)]
        eqs = [k == thr for k in keys]
        eqf = jnp.concatenate([jnp.where(e, 1.0, 0.0) for e in eqs], axis=1).astype(bf16)
        pref = _dot(eqf, tri_ref[...])
        total = jnp.broadcast_to(pref[:, sk - 1:sk], (tq, LANES))
        for n, (k, e) in enumerate(zip(keys, eqs)):
            c = n * LANES
            sel = jnp.logical_or(k > thr, jnp.logical_and(e, pref[:, c:c + LANES] + ceq <= need))
            if diag:
                ri = lax.broadcasted_iota(jnp.int32, (tq, LANES), 0)
                ci = lax.broadcasted_iota(jnp.int32, (tq, LANES), 1)
                sel = jnp.logical_and(sel, ci + c <= ri)
            mb_ref[:, slot * sk + c:slot * sk + c + LANES] = jnp.where(sel, 0.0, NEG_MASK)
        ceq_ref[...] = ceq + total

    def attend(j0, n_blk, near_col):
        for b in range(n_blk):
            mask_block(j0 + b, b, near_col is not None and b == n_blk - 1)
        heads, blks = range(N_HEADS_A), range(n_blk)
        logits = []
        for h in heads:
            q = qlat_ref[h]
            parts = []
            for b in blks:
                lg = _dot_nt(q, ckv_ref[key_rows(j0 + b), :KV_RANK]) + mb_ref[:, b * sk:(b + 1) * sk]
                if near_col is not None:
                    lg = lg + bias_ref[h, :, near_col + b * sk:near_col + (b + 1) * sk]
                parts.append(lg)
            logits.append(parts)
        m_olds = [m_ref[h] for h in heads]
        m_news = []
        for h in heads:
            mx = functools.reduce(jnp.maximum, [p[:, c:c + LANES] for p in logits[h] for c in range(0, sk, LANES)])
            m_news.append(jnp.maximum(m_olds[h], jnp.max(mx, axis=1, keepdims=True)))
        probs = []
        for h in heads:
            m_wide = jnp.concatenate([m_news[h]] * (sk // LANES), axis=1)
            probs.append([jnp.exp2(logits[h][b] - m_wide).astype(bf16) for b in blks])
        for h in heads:
            pv = None
            for b in blks:
                d = _dot(probs[h][b], ckv_ref[key_rows(j0 + b), :])
                pv = d if pv is None else pv + d
            alpha = jnp.exp2(m_olds[h] - m_news[h])
            acc_ref[h] = jnp.concatenate([alpha, alpha], axis=1) * acc_ref[h] + pv
            m_ref[h] = m_news[h]

    lone = jnp.logical_and(i >= 2, i % 2 == 0)

    @pl.when(lone)
    def _():
        attend(0, 1, None)

    first = jnp.where(lone, 1, 0)

    def far_pair(p, carry):
        attend(first + 2 * p, 2, None)
        return carry

    lax.fori_loop(0, (i - 1 - first) // 2, far_pair, 0)

    @pl.when(i >= 1)
    def _():
        attend(i - 1, 2, 0)

    @pl.when(i == 0)
    def _():
        attend(0, 1, sk)

    outs = []
    for h in range(N_HEADS_A):
        acc = acc_ref[h]
        o_lat = acc[:, :KV_RANK] / acc[:, KV_RANK:]
        outs.append(_dot(o_lat.astype(bf16), wuv_ref[h]))
    o_ref[...] = jnp.concatenate(outs, axis=1).astype(bf16)


def _t5_bucket(n):
    max_exact = N_BUCKETS // 2
    nf = jnp.maximum(n, 1).astype(f32)
    large = max_exact + (jnp.log(nf / max_exact) / math.log(MAX_DISTANCE / max_exact)
                         * (N_BUCKETS - max_exact)).astype(jnp.int32)
    large = jnp.minimum(large, N_BUCKETS - 1)
    return jnp.where(n < max_exact, n, large)


def _dsa(qa, qidx, widx, kidx, ckv, wuk_t, wuv_t, rel_bias, batch, seq):
    n = qa.shape[0]
    tq = ATT_BLOCK
    nq = seq // tq
    topk = min(INDEX_TOPK, seq // 4)
    span = 3 * tq - 1
    by_dist = rel_bias[_t5_bucket(jnp.maximum(jnp.arange(span) - tq + 1, 0))].astype(f32).T
    skew = jnp.tile(by_dist, (1, tq + 1))[:, :tq * (span + 1)].reshape(N_HEADS_A, tq, span + 1)
    bias_near = skew[:, :, :2 * tq][:, :, ::-1]
    far_n = np.float32(tq + 1)
    assert 16 + int(np.log(far_n / 16) / math.log(MAX_DISTANCE / 16) * 16) >= N_BUCKETS - 1
    bias_near = (bias_near - rel_bias[N_BUCKETS - 1].astype(f32)[:, None, None]) * LOG2E
    tri = (jnp.arange(tq)[:, None] <= jnp.arange(tq)[None, :]).astype(bf16)

    qrow = lambda b, i: (b * nq + i, 0)
    full = lambda b, i: (b, 0)
    c3 = lambda b, i: (0, 0, 0)
    c2 = lambda b, i: (0, 0)
    kern = functools.partial(_dsa_kernel, topk=topk)
    return pl.pallas_call(
        kern,
        grid=(batch, nq),
        in_specs=[pl.BlockSpec((tq, WIDTH_A), qrow),
                  pl.BlockSpec((tq, N_IDX_HEADS * IDX_DIM), qrow),
                  pl.BlockSpec((tq, LANES), qrow),
                  pl.BlockSpec((seq, IDX_DIM), full),
                  pl.BlockSpec((seq, 2 * KV_RANK), full),
                  pl.BlockSpec((N_HEADS_A, HEAD_DIM, KV_RANK), c3),
                  pl.BlockSpec((N_HEADS_A, KV_RANK, HEAD_DIM), c3),
                  pl.BlockSpec((N_HEADS_A, tq, 2 * tq), c3),
                  pl.BlockSpec((tq, tq), c2)],
        out_specs=pl.BlockSpec((tq, WIDTH_A), qrow),
        out_shape=jax.ShapeDtypeStruct((n, WIDTH_A), bf16),
        scratch_shapes=[pltpu.VMEM((nq + 1, tq, tq), jnp.int32),
                        pltpu.VMEM((N_HEADS_A, tq, KV_RANK), bf16),
                        pltpu.VMEM((N_IDX_HEADS, tq, LANES), f32),
                        pltpu.VMEM((tq, LANES), jnp.int32),
                        pltpu.VMEM((tq, LANES), f32),
                        pltpu.VMEM((tq, LANES), f32),
                        pltpu.VMEM((tq, 2 * tq), f32),
                        pltpu.VMEM((N_HEADS_A, tq, LANES), f32),
                        pltpu.VMEM((N_HEADS_A, tq, 2 * KV_RANK), f32)],
        compiler_params=pltpu.CompilerParams(dimension_semantics=("parallel", "arbitrary"),
                                             vmem_limit_bytes=VMEM_LIMIT),
        name="dsa",
    )(qa, qidx, widx, kidx, ckv, wuk_t, wuv_t, bias_near, tri)


def _sb_kernel(q_ref, k_ref, v_ref, tri_ref, o_ref, carry_ref, acc_ref):
    tq = sk = ATT_BLOCK
    i = pl.program_id(1)
    carry_ref[...] = jnp.zeros(carry_ref.shape, f32)
    acc_ref[...] = jnp.zeros(acc_ref.shape, f32)

    def step(j, diag):
        rows = pl.ds(pl.multiple_of(j * sk, sk), sk)
        if diag:
            strict = lax.broadcasted_iota(jnp.int32, (tq, sk), 1) < lax.broadcasted_iota(jnp.int32, (tq, sk), 0)
        heads = range(N_HEADS_B)
        lanes = [slice(h * HEAD_DIM, (h + 1) * HEAD_DIM) for h in heads]
        zs = [_dot_nt(q_ref[:, lanes[h]] * (HEAD_DIM ** -0.5), k_ref[rows, lanes[h]]) for h in heads]
        lfs = [-(jnp.maximum(z, 0.0) + jnp.log(1.0 + jnp.exp(-jnp.abs(z)))) for z in zs]
        if diag:
            lfs = [jnp.where(strict, lf, 0.0) for lf in lfs]
        splits = [_split_bf16(lf) for lf in lfs]
        laters = [_dot(hi, tri_ref[...]) + _dot(lo, tri_ref[...]) for hi, lo in splits]
        carries = [carry_ref[h] for h in heads]
        probs = [jnp.exp(zs[h] + lfs[h] + laters[h] + jnp.concatenate([carries[h]] * (sk // LANES), axis=1))
                 for h in heads]
        if diag:
            probs = [jnp.where(strict, a, 0.0) for a in probs]
        top = None
        for h in heads:
            acc_ref[h] += _dot(probs[h].astype(bf16), v_ref[rows, lanes[h]])
            carry = carries[h] + jnp.sum(lfs[h], axis=1, keepdims=True)
            carry_ref[h] = carry
            top = carry if top is None else jnp.maximum(top, carry)
        return jnp.max(top)

    def cond(st):
        j, top = st
        return jnp.logical_and(j >= 0, top > SB_SKIP)

    def body(st):
        j, _ = st
        return j - 1, step(j, False)

    lax.while_loop(cond, body, (i - 1, step(i, True)))
    o_ref[...] = jnp.concatenate([acc_ref[h] for h in range(N_HEADS_B)], axis=1).astype(bf16)


def _sb(qb, kb, vb, batch, seq):
    n = qb.shape[0]
    tq = ATT_BLOCK
    nq = seq // tq
    tri = (jnp.arange(tq)[:, None] > jnp.arange(tq)[None, :]).astype(bf16)
    qrow = lambda b, i: (b * nq + i, 0)
    full = lambda b, i: (b, 0)
    return pl.pallas_call(
        _sb_kernel,
        grid=(batch, nq),
        in_specs=[pl.BlockSpec((tq, WIDTH_B), qrow),
                  pl.BlockSpec((seq, WIDTH_B), full),
                  pl.BlockSpec((seq, WIDTH_B), full),
                  pl.BlockSpec((tq, tq), lambda b, i: (0, 0))],
        out_specs=pl.BlockSpec((tq, WIDTH_B), qrow),
        out_shape=jax.ShapeDtypeStruct((n, WIDTH_B), bf16),
        scratch_shapes=[pltpu.VMEM((N_HEADS_B, tq, LANES), f32),
                        pltpu.VMEM((N_HEADS_B, tq, HEAD_DIM), f32)],
        compiler_params=pltpu.CompilerParams(dimension_semantics=("parallel", "arbitrary"),
                                             vmem_limit_bytes=VMEM_LIMIT),
        name="sb",
    )(qb, kb, vb, tri)


def _merge_kernel(oa_ref, ob_ref, ga_ref, gb_ref, x_ref, gate_ref, shift_ref, scale_ref,
                  wa_ref, wb_ref, wo_ref, g_ref, b_ref, wr_hi_ref, wr_lo_ref, br_ref,
                  x1_ref, h2_ref, logit_ref):
    ya = _dot(oa_ref[...], wa_ref[...])
    yb = _dot(ob_ref[...], wb_ref[...])
    merged = jax.nn.sigmoid(ga_ref[...].astype(f32)) * ya + jax.nn.sigmoid(gb_ref[...].astype(f32)) * yb
    y = _dot(merged.astype(bf16), wo_ref[...])
    x1 = _ln(DN_ALPHA * x_ref[...] + gate_ref[0] * y) * g_ref[...] + b_ref[...]
    x1_ref[...] = x1
    h2 = _ln(x1) * (1.0 + scale_ref[0]) + shift_ref[0]
    hi, lo = _split_bf16(h2)
    h2_ref[...] = hi
    logit_ref[...] = (_dot(hi, wr_hi_ref[...]) + _dot(lo, wr_hi_ref[...]) + _dot(hi, wr_lo_ref[...])
                      + br_ref[...])


def _merge(oa, ob, ga, gb, x2, gate1, shift2, scale2, wa, wb, wo, g, b, wr_hi, wr_lo, br, seq):
    n, d = x2.shape
    tm = min(ROW_TILE, seq)
    per_b = seq // tm
    row = lambda i: (i, 0)
    bat = lambda i: (i // per_b, 0, 0)
    const = lambda i: (0, 0)
    return pl.pallas_call(
        _merge_kernel,
        grid=(n // tm,),
        in_specs=[pl.BlockSpec((tm, WIDTH_A), row), pl.BlockSpec((tm, WIDTH_B), row),
                  pl.BlockSpec((tm, d), row), pl.BlockSpec((tm, d), row), pl.BlockSpec((tm, d), row),
                  pl.BlockSpec((1, 1, d), bat), pl.BlockSpec((1, 1, d), bat), pl.BlockSpec((1, 1, d), bat),
                  pl.BlockSpec((WIDTH_A, d), const), pl.BlockSpec((WIDTH_B, d), const), pl.BlockSpec((d, d), const),
                  pl.BlockSpec((1, d), const), pl.BlockSpec((1, d), const),
                  pl.BlockSpec((d, LANES), const), pl.BlockSpec((d, LANES), const), pl.BlockSpec((1, LANES), const)],
        out_specs=[pl.BlockSpec((tm, d), row), pl.BlockSpec((tm, d), row), pl.BlockSpec((tm, LANES), row)],
        out_shape=[jax.ShapeDtypeStruct((n, d), f32), jax.ShapeDtypeStruct((n, d), bf16),
                   jax.ShapeDtypeStruct((n, LANES), f32)],
        compiler_params=pltpu.CompilerParams(dimension_semantics=("parallel",), vmem_limit_bytes=VMEM_LIMIT),
        name="merge",
    )(oa, ob, ga, gb, x2, gate1, shift2, scale2, wa, wb, wo, g, b, wr_hi, wr_lo, br)


def _expert_kernel(be_ref, nb_ref, x_ref, wgu_ref, bgu_ref, wdn_ref, bdn_ref, *rest, block0):
    y_ref, wgu_bf, wdn_bf = rest[-3:]
    i = pl.program_id(0)
    g = i + block0

    @pl.when(jnp.logical_or(i == 0, be_ref[g] != be_ref[jnp.maximum(g - 1, 0)]))
    def _():
        wgu_bf[...] = wgu_ref[0, 0].astype(bf16)
        wdn_bf[...] = wdn_ref[0, 0].astype(bf16)

    @pl.when(g < nb_ref[0])
    def _():
        gu = _dot(x_ref[...], wgu_bf[...]) + bgu_ref[0, 0]
        a = jnp.minimum(gu[:, :D_EXPERT], SWIGLU_LIMIT)
        u = jnp.clip(gu[:, D_EXPERT:], -SWIGLU_LIMIT, SWIGLU_LIMIT)
        act = (u + 1.0) * a * jax.nn.sigmoid(SWIGLU_ALPHA * a)
        y_ref[...] = (_dot(act.astype(bf16), wdn_bf[...]) + bdn_ref[0, 0]).astype(y_ref.dtype)

    @pl.when(g >= nb_ref[0])
    def _():
        y_ref[...] = jnp.zeros(y_ref.shape, y_ref.dtype)


def _experts(h2, row_token, block_expert, n_used, layer, wgu, bgu, wdn, bdn):
    rows, d = row_token.shape[0], h2.shape[1]
    n_blocks = rows // MOE_ROWS
    bounds = [n_blocks * c // MOE_CHUNKS for c in range(MOE_CHUNKS + 1)]
    ys = None
    for b0, b1 in zip(bounds[:-1], bounds[1:]):
        xs = h2[row_token[b0 * MOE_ROWS:b1 * MOE_ROWS]]
        e_idx = lambda i, be, nb, b0=b0: (layer, be[i + b0], 0, 0)
        in_specs = [pl.BlockSpec((MOE_ROWS, d), lambda i, be, nb: (i, 0)),
                    pl.BlockSpec((1, 1, d, 2 * D_EXPERT), e_idx),
                    pl.BlockSpec((1, 1, 1, 2 * D_EXPERT), e_idx),
                    pl.BlockSpec((1, 1, D_EXPERT, d), e_idx),
                    pl.BlockSpec((1, 1, 1, d), e_idx)]
        args = [block_expert, n_used, xs, wgu, bgu, wdn, bdn]
        aliases = {}
        if ys is not None:
            in_specs.append(pl.BlockSpec(memory_space=pl.ANY))
            aliases = {len(args): 0}
            args.append(ys)
        grid_spec = pltpu.PrefetchScalarGridSpec(
            num_scalar_prefetch=2,
            grid=(b1 - b0,),
            in_specs=in_specs,
            out_specs=pl.BlockSpec((MOE_ROWS, d), lambda i, be, nb, b0=b0: (i + b0, 0)),
            scratch_shapes=[pltpu.VMEM((d, 2 * D_EXPERT), bf16), pltpu.VMEM((D_EXPERT, d), bf16)],
        )
        ys = pl.pallas_call(
            functools.partial(_expert_kernel, block0=b0),
            grid_spec=grid_spec,
            out_shape=jax.ShapeDtypeStruct((rows, d), bf16),
            input_output_aliases=aliases,
            compiler_params=pltpu.CompilerParams(dimension_semantics=("arbitrary",), vmem_limit_bytes=VMEM_LIMIT),
            name="experts",
        )(*args)
    return ys


def _combine_kernel(x_ref, y_ref, gk_ref, gate_ref, g_ref, b_ref, *rest):
    o_ref = rest[-1]
    gk = gk_ref[...]
    y = y_ref[0].astype(f32) * gk[:, 0:1]
    for k in range(1, TOP_K):
        y = y + y_ref[k].astype(f32) * gk[:, k:k + 1]
    o_ref[...] = _ln(DN_ALPHA * x_ref[...] + gate_ref[0] * y) * g_ref[...] + b_ref[...]


def _combine(x1, ys, dest, gates, gate2, g, b, seq):
    n, d = x1.shape
    tm = min(ROW_TILE, seq)
    per_b = seq // tm
    tiles = n // tm
    chunks = min(COMBINE_CHUNKS, tiles)
    bounds = [tiles * c // chunks for c in range(chunks + 1)]
    const = lambda i: (0, 0)
    out = None
    for a0, a1 in zip(bounds[:-1], bounds[1:]):
        yk = ys[dest[a0 * tm:a1 * tm].T]
        row = lambda i, a0=a0: (i + a0, 0)
        in_specs = [pl.BlockSpec((tm, d), row),
                    pl.BlockSpec((TOP_K, tm, d), lambda i: (0, i, 0)),
                    pl.BlockSpec((tm, TOP_K), row),
                    pl.BlockSpec((1, 1, d), lambda i, a0=a0: ((i + a0) // per_b, 0, 0)),
                    pl.BlockSpec((1, d), const), pl.BlockSpec((1, d), const)]
        args = [x1, yk, gates, gate2, g, b]
        aliases = {}
        if out is not None:
            in_specs.append(pl.BlockSpec(memory_space=pl.ANY))
            aliases = {len(args): 0}
            args.append(out)
        out = pl.pallas_call(
            _combine_kernel,
            grid=(a1 - a0,),
            in_specs=in_specs,
            out_specs=pl.BlockSpec((tm, d), row),
            out_shape=jax.ShapeDtypeStruct((n, d), f32),
            input_output_aliases=aliases,
            compiler_params=pltpu.CompilerParams(dimension_semantics=("parallel",), vmem_limit_bytes=VMEM_LIMIT),
            name="combine",
        )(*args)
    return out


def _route_kernel(logit_ref, tri_ref, idx_ref, gate_ref, rank_ref, count_ref, carry_ref):
    @pl.when(pl.program_id(0) == 0)
    def _():
        carry_ref[...] = jnp.zeros(carry_ref.shape, f32)

    tm = logit_ref.shape[0]
    lane = lax.broadcasted_iota(jnp.int32, (tm, LANES), 1).astype(f32)
    v = jnp.where(lane < N_EXPERTS, logit_ref[...], -jnp.inf)
    tops, picks = [], []
    for _ in range(TOP_K):
        top = jnp.max(v, axis=1, keepdims=True)
        pick = jnp.min(jnp.where(v == top, lane, float(LANES)), axis=1, keepdims=True)
        tops.append(top)
        picks.append(pick)
        v = jnp.where(lane == pick, -jnp.inf, v)
    exps = [jnp.exp(top - tops[0]) for top in tops]
    denom = functools.reduce(jnp.add, exps)
    hits = [lane == pick for pick in picks]
    onehot = functools.reduce(jnp.add, [jnp.where(hit, 1.0, 0.0) for hit in hits])
    before = _dot(tri_ref[...], onehot.astype(bf16)) + carry_ref[...]
    ranks = [jnp.sum(jnp.where(hit, before, 0.0), axis=1, keepdims=True) for hit in hits]

    def spread(cols):
        return functools.reduce(jnp.add, [jnp.where(lane == float(k), col, 0.0) for k, col in enumerate(cols)])

    idx_ref[...] = spread(picks).astype(jnp.int32)
    gate_ref[...] = spread([e / denom for e in exps])
    rank_ref[...] = spread(ranks).astype(jnp.int32)
    carry_ref[...] += jnp.sum(onehot, axis=0, keepdims=True)
    count_ref[...] = carry_ref[...]


def _route(logits):
    n = logits.shape[0]
    tm = ROW_TILE
    tri = (jnp.arange(tm)[:, None] > jnp.arange(tm)[None, :]).astype(bf16)
    row = lambda i: (i, 0)
    const = lambda i: (0, 0)
    idx, gates, rank, counts = pl.pallas_call(
        _route_kernel,
        grid=(n // tm,),
        in_specs=[pl.BlockSpec((tm, LANES), row), pl.BlockSpec((tm, tm), const)],
        out_specs=[pl.BlockSpec((tm, LANES), row), pl.BlockSpec((tm, LANES), row), pl.BlockSpec((tm, LANES), row),
                   pl.BlockSpec((1, LANES), const)],
        out_shape=[jax.ShapeDtypeStruct((n, LANES), jnp.int32), jax.ShapeDtypeStruct((n, LANES), f32),
                   jax.ShapeDtypeStruct((n, LANES), jnp.int32), jax.ShapeDtypeStruct((1, LANES), f32)],
        scratch_shapes=[pltpu.VMEM((1, LANES), f32)],
        compiler_params=pltpu.CompilerParams(dimension_semantics=("arbitrary",), vmem_limit_bytes=VMEM_LIMIT),
        name="route",
    )(logits, tri)
    top_idx, gates, rank = idx[:, :TOP_K], gates[:, :TOP_K], rank[:, :TOP_K]
    counts = counts[0, :N_EXPERTS].astype(jnp.int32)
    padded = (counts + MOE_ROWS - 1) // MOE_ROWS * MOE_ROWS
    pad_end = jnp.cumsum(padded)
    pad_start = pad_end - padded
    dest = pad_start[top_idx] + rank
    n_blocks = -(-(n * TOP_K + N_EXPERTS * (MOE_ROWS - 1)) // MOE_ROWS)
    block_start = jnp.arange(n_blocks, dtype=jnp.int32) * MOE_ROWS
    block_expert = jnp.minimum(jnp.sum((pad_end[None, :] <= block_start[:, None]).astype(jnp.int32), axis=1),
                               N_EXPERTS - 1)
    n_used = (pad_end[-1:] // MOE_ROWS).astype(jnp.int32)
    return gates, dest, block_expert, n_used, n_blocks


def _split_w_in(w):
    split = sum(COL_SPLITS[:5])
    return jnp.pad(w[:, :split], ((0, 0), (0, IDX_PAD))).astype(bf16), w[:, split:].astype(bf16)


def kernel(x, c, rel_bias, w_ada, b_ada, w_in, g_kv, w_uk, w_uv, w_a_out, w_b_out, w_o, ln1_g, ln1_b,
           w_router, b_router, w_gu, b_gu, w_dn, b_dn, ln2_g, ln2_b):
    B, S, D = x.shape
    N = B * S
    assert D == D_MODEL and S % ATT_BLOCK == 0
    cond = jax.nn.silu(c)
    x2 = x.reshape(N, D)
    for l in range(DEPTH):
        mod = jnp.dot(cond, w_ada[l], precision=lax.Precision.HIGHEST) + b_ada[l]
        shift1, scale1, gate1, shift2, scale2, gate2 = [m[:, None, :] for m in jnp.split(mod, 6, axis=-1)]

        qa, ckv, qidx, kidx, widx, qb, kb, vb, ga, gb = _proj(
            x2, shift1, scale1, *_split_w_in(w_in[l]), g_kv[l][None, :], S)
        wuk_t = jnp.transpose(w_uk[l], (1, 2, 0)).astype(bf16)
        wuv_t = jnp.transpose(w_uv[l], (1, 0, 2)).astype(bf16)
        oa = _dsa(qa, qidx, widx, kidx, ckv, wuk_t, wuv_t, rel_bias, B, S)
        ob = _sb(qb, kb, vb, B, S)

        wr = jnp.pad(w_router[l], ((0, 0), (0, LANES - N_EXPERTS)))
        wr_hi, wr_lo = _split_bf16(wr)
        br = jnp.pad(b_router[l], (0, LANES - N_EXPERTS))[None, :]
        x1, h2, logits = _merge(oa, ob, ga, gb, x2, gate1, shift2, scale2,
                                w_a_out[l].astype(bf16), w_b_out[l].astype(bf16), w_o[l].astype(bf16),
                                ln1_g[l][None, :], ln1_b[l][None, :], wr_hi, wr_lo, br, S)

        gates, dest, block_expert, n_used, n_blocks = _route(logits)
        rows = n_blocks * MOE_ROWS
        row_token = jnp.zeros((rows,), jnp.int32).at[dest.reshape(-1)].set(
            jnp.repeat(jnp.arange(N, dtype=jnp.int32), TOP_K))
        ys = _experts(h2, row_token, block_expert, n_used, l, w_gu, b_gu[:, :, None, :], w_dn, b_dn[:, :, None, :])
        x2 = _combine(x1, ys, dest, gates, gate2, ln2_g[l][None, :], ln2_b[l][None, :], S)
    return x2.reshape(B, S, D)
```

```python
import functools
import math

import numpy as np
import jax
import jax.numpy as jnp
from jax import lax
from jax.experimental import pallas as pl
from jax.experimental.pallas import tpu as pltpu

D_MODEL = 1024
HEAD_DIM = 64
N_HEADS_A = 8
WIDTH_A = N_HEADS_A * HEAD_DIM
KV_RANK = 128
N_IDX_HEADS = 4
IDX_DIM = 64
INDEX_TOPK = 256
N_HEADS_B = 8
WIDTH_B = N_HEADS_B * HEAD_DIM
N_BUCKETS = 32
MAX_DISTANCE = 128
N_EXPERTS = 32
TOP_K = 4
D_EXPERT = 1024
SWIGLU_LIMIT = 7.0
SWIGLU_ALPHA = 1.702
LN_EPS = 1e-5
RMS_EPS = 1e-6
DEPTH = 2
DN_ALPHA = (2 * DEPTH) ** 0.25
IDX_SCALE = (N_IDX_HEADS ** -0.5) * (IDX_DIM ** -0.5)
LOG2E = math.log2(math.e)

COL_SPLITS = (WIDTH_A, KV_RANK, N_IDX_HEADS * IDX_DIM, IDX_DIM, N_IDX_HEADS,
              WIDTH_B, WIDTH_B, WIDTH_B, D_MODEL, D_MODEL)
N_COLS = sum(COL_SPLITS)

LANES = 128
VMEM_LIMIT = 56 * 1024 * 1024

IDX_PAD = LANES - IDX_DIM - N_IDX_HEADS
C_QA = 0
C_CKV = C_QA + WIDTH_A
C_QIDX = C_CKV + KV_RANK
C_KW = C_QIDX + N_IDX_HEADS * IDX_DIM
C_QB = C_KW + LANES
C_KB = C_QB + WIDTH_B
C_VB = C_KB + WIDTH_B
C_GA = C_VB + WIDTH_B
C_GB = C_GA + D_MODEL
N_COLS_PAD = C_GB + D_MODEL

ROW_TILE = 512
ATT_BLOCK = 256
BISECT_ROWS = 128
MOE_ROWS = 512
MOE_CHUNKS = 4
COMBINE_CHUNKS = 1
NEG_MASK = -1e30
M_INIT = -1e29
SB_SKIP = -110.0
INT_MIN = np.int32(-2 ** 31)

f32 = jnp.float32
bf16 = jnp.bfloat16


def _ln(x):
    mu = jnp.mean(x, axis=-1, keepdims=True)
    xc = x - mu
    var = jnp.mean(xc * xc, axis=-1, keepdims=True)
    return xc * lax.rsqrt(var + LN_EPS)


def _dot(a, b):
    return jnp.dot(a, b, preferred_element_type=f32)


def _dot_nt(a, b):
    return lax.dot_general(a, b, (((1,), (1,)), ((), ())), preferred_element_type=f32)


def _split_bf16(x):
    hi = x.astype(bf16)
    lo = (x - hi.astype(f32)).astype(bf16)
    return hi, lo


def _proj_kernel(x_ref, shift_ref, scale_ref, wa_ref, wb_ref, gkv_ref,
                 qa_ref, ckv_ref, qidx_ref, kidx_ref, widx_ref, qb_ref, kb_ref, vb_ref, ga_ref, gb_ref):
    h = _ln(x_ref[...]) * (1.0 + scale_ref[0]) + shift_ref[0]
    hb = h.astype(bf16)

    def mm(c0, width):
        if c0 < C_QB:
            return _dot(hb, wa_ref[:, c0:c0 + width])
        return _dot(hb, wb_ref[:, c0 - C_QB:c0 - C_QB + width])

    qa_ref[...] = mm(C_QA, WIDTH_A).astype(bf16)
    ckv = mm(C_CKV, KV_RANK)
    ckv = ckv * lax.rsqrt(jnp.mean(ckv * ckv, axis=-1, keepdims=True) + RMS_EPS) * gkv_ref[...]
    ckv_ref[...] = jnp.concatenate([ckv, jnp.ones_like(ckv)], axis=1).astype(bf16)
    qidx_ref[...] = mm(C_QIDX, N_IDX_HEADS * IDX_DIM).astype(bf16)
    kw = mm(C_KW, LANES)
    kidx_ref[...] = kw[:, :IDX_DIM].astype(bf16)
    widx_ref[...] = kw * IDX_SCALE
    qb_ref[...] = mm(C_QB, WIDTH_B).astype(bf16)
    kb_ref[...] = mm(C_KB, WIDTH_B).astype(bf16)
    vb_ref[...] = mm(C_VB, WIDTH_B).astype(bf16)
    for c in range(0, D_MODEL, 512):
        ga_ref[:, c:c + 512] = mm(C_GA + c, 512).astype(bf16)
        gb_ref[:, c:c + 512] = mm(C_GB + c, 512).astype(bf16)


def _proj(x2, shift, scale, w_dsa, w_rest, g_kv, seq):
    n, d = x2.shape
    tm = min(ROW_TILE, seq)
    per_b = seq // tm
    row = lambda i: (i, 0)
    bat = lambda i: (i // per_b, 0, 0)
    const = lambda i: (0, 0)
    widths = (WIDTH_A, 2 * KV_RANK, N_IDX_HEADS * IDX_DIM, IDX_DIM, LANES, WIDTH_B, WIDTH_B, WIDTH_B, D_MODEL, D_MODEL)
    dtypes = (bf16, bf16, bf16, bf16, f32, bf16, bf16, bf16, bf16, bf16)
    return pl.pallas_call(
        _proj_kernel,
        grid=(n // tm,),
        in_specs=[pl.BlockSpec((tm, d), row),
                  pl.BlockSpec((1, 1, d), bat),
                  pl.BlockSpec((1, 1, d), bat),
                  pl.BlockSpec((d, C_QB), const),
                  pl.BlockSpec((d, N_COLS_PAD - C_QB), const),
                  pl.BlockSpec((1, KV_RANK), const)],
        out_specs=[pl.BlockSpec((tm, w), row) for w in widths],
        out_shape=[jax.ShapeDtypeStruct((n, w), dt) for w, dt in zip(widths, dtypes)],
        compiler_params=pltpu.CompilerParams(dimension_semantics=("parallel",), vmem_limit_bytes=VMEM_LIMIT),
        name="proj",
    )(x2, shift, scale, w_dsa, w_rest, g_kv)


def _dsa_kernel(qa_ref, qidx_ref, widx_ref, kidx_ref, ckv_ref, wuk_ref, wuv_ref, bias_ref, tri_ref,
                o_ref, key_ref, qlat_ref, wrep_ref, thr_ref, need_ref, ceq_ref, mb_ref, m_ref, acc_ref, *, topk):
    tq = sk = ATT_BLOCK
    i = pl.program_id(1)
    n_blocks = i + 1

    for h in range(N_HEADS_A):
        ql = _dot(qa_ref[:, h * HEAD_DIM:(h + 1) * HEAD_DIM], wuk_ref[h]) * (HEAD_DIM ** -0.5 * LOG2E)
        qlat_ref[h] = ql.astype(bf16)

    rowi = lax.broadcasted_iota(jnp.int32, (tq, sk), 0)
    coli = lax.broadcasted_iota(jnp.int32, (tq, sk), 1)
    w = widx_ref[:, IDX_DIM:IDX_DIM + N_IDX_HEADS]
    for h in range(N_IDX_HEADS):
        wrep_ref[h] = jnp.broadcast_to(w[:, h:h + 1], (tq, LANES))

    def score_pair(p, carry):
        for u in range(2):
            j = 2 * p + u
            kblk = kidx_ref[pl.ds(pl.multiple_of(jnp.minimum(j, i) * sk, sk), sk), :]
            s = None
            for h in range(N_IDX_HEADS):
                d = _dot_nt(qidx_ref[:, h * IDX_DIM:(h + 1) * IDX_DIM], kblk)
                t = jnp.maximum(d, 0.0) * jnp.concatenate([wrep_ref[h]] * (sk // LANES), axis=1)
                s = t if s is None else s + t
            s = jnp.where(s == 0.0, 0.0, s)
            s = jnp.where((j - i) * sk + coli <= rowi, s, -jnp.inf)
            bits = pltpu.bitcast(s, jnp.int32)
            key_ref[j] = bits ^ ((bits >> 31) & np.int32(0x7FFFFFFF))
        return carry

    lax.fori_loop(0, (n_blocks + 1) // 2, score_pair, 0)
    key_ref[n_blocks] = jnp.full((tq, sk), INT_MIN, jnp.int32)

    halves = range(0, tq, BISECT_ROWS)

    def count_ge(cands):
        accs = []
        for r0, cand in zip(halves, cands):
            cb = jnp.broadcast_to(cand, (BISECT_ROWS, LANES))

            def body(p, acc, r0=r0, cb=cb):
                for u in range(2):
                    for c in range(0, sk, LANES):
                        kk = key_ref[2 * p + u, r0:r0 + BISECT_ROWS, c:c + LANES]
                        acc = acc + jnp.where(kk >= cb, 1.0, 0.0)
                return acc

            accs.append(lax.fori_loop(0, (n_blocks + 1) // 2, body, jnp.zeros((BISECT_ROWS, LANES), f32)))
        return [jnp.sum(acc, axis=1, keepdims=True) for acc in accs]

    def bit_step(b, answers):
        cands = [ans + jnp.left_shift(jnp.int32(1), 31 - b) for ans in answers]
        return tuple(jnp.where(cnt >= float(topk), cand, ans)
                     for cnt, cand, ans in zip(count_ge(cands), cands, answers))

    thrs = lax.fori_loop(0, 32, bit_step, tuple(jnp.full((BISECT_ROWS, 1), INT_MIN, jnp.int32) for _ in halves))
    for r0, thr, n_gt in zip(halves, thrs, count_ge([thr + 1 for thr in thrs])):
        thr_ref[r0:r0 + BISECT_ROWS, :] = jnp.broadcast_to(thr, (BISECT_ROWS, LANES))
        need_ref[r0:r0 + BISECT_ROWS, :] = jnp.broadcast_to(float(topk) - n_gt, (BISECT_ROWS, LANES))

    m_ref[...] = jnp.full(m_ref.shape, M_INIT, f32)
    acc_ref[...] = jnp.zeros(acc_ref.shape, f32)
    ceq_ref[...] = jnp.zeros(ceq_ref.shape, f32)

    def key_rows(j):
        return pl.ds(pl.multiple_of(j * sk, sk), sk)

    def mask_block(j, slot, diag):
        thr, need, ceq = thr_ref[...], need_ref[...], ceq_ref[...]
        keys = [key_ref[j, :, c:c + LANES] for c in range(0, sk, LANES)]
        eqs = [k == thr for k in keys]
        eqf = jnp.concatenate([jnp.where(e, 1.0, 0.0) for e in eqs], axis=1).astype(bf16)
        pref = _dot(eqf, tri_ref[...])
        total = jnp.broadcast_to(pref[:, sk - 1:sk], (tq, LANES))
        for n, (k, e) in enumerate(zip(keys, eqs)):
            c = n * LANES
            sel = jnp.logical_or(k > thr, jnp.logical_and(e, pref[:, c:c + LANES] + ceq <= need))
            if diag:
                ri = lax.broadcasted_iota(jnp.int32, (tq, LANES), 0)
                ci = lax.broadcasted_iota(jnp.int32, (tq, LANES), 1)
                sel = jnp.logical_and(sel, ci + c <= ri)
            mb_ref[:, slot * sk + c:slot * sk + c + LANES] = jnp.where(sel, 0.0, NEG_MASK)
        ceq_ref[...] = ceq + total

    def attend(j0, n_blk, near_col):
        for b in range(n_blk):
            mask_block(j0 + b, b, near_col is not None and b == n_blk - 1)
        heads, blks = range(N_HEADS_A), range(n_blk)
        logits = []
        for h in heads:
            q = qlat_ref[h]
            parts = []
            for b in blks:
                lg = _dot_nt(q, ckv_ref[key_rows(j0 + b), :KV_RANK]) + mb_ref[:, b * sk:(b + 1) * sk]
                if near_col is not None:
                    lg = lg + bias_ref[h, :, near_col + b * sk:near_col + (b + 1) * sk]
                parts.append(lg)
            logits.append(parts)
        m_olds = [m_ref[h] for h in heads]
        m_news = []
        for h in heads:
            mx = functools.reduce(jnp.maximum, [p[:, c:c + LANES] for p in logits[h] for c in range(0, sk, LANES)])
            m_news.append(jnp.maximum(m_olds[h], jnp.max(mx, axis=1, keepdims=True)))
        probs = []
        for h in heads:
            m_wide = jnp.concatenate([m_news[h]] * (sk // LANES), axis=1)
            probs.append([jnp.exp2(logits[h][b] - m_wide).astype(bf16) for b in blks])
        for h in heads:
            pv = None
            for b in blks:
                d = _dot(probs[h][b], ckv_ref[key_rows(j0 + b), :])
                pv = d if pv is None else pv + d
            alpha = jnp.exp2(m_olds[h] - m_news[h])
            acc_ref[h] = jnp.concatenate([alpha, alpha], axis=1) * acc_ref[h] + pv
            m_ref[h] = m_news[h]

    lone = jnp.logical_and(i >= 2, i % 2 == 0)

    @pl.when(lone)
    def _():
        attend(0, 1, None)

    first = jnp.where(lone, 1, 0)

    def far_pair(p, carry):
        attend(first + 2 * p, 2, None)
        return carry

    lax.fori_loop(0, (i - 1 - first) // 2, far_pair, 0)

    @pl.when(i >= 1)
    def _():
        attend(i - 1, 2, 0)

    @pl.when(i == 0)
    def _():
        attend(0, 1, sk)

    outs = []
    for h in range(N_HEADS_A):
        acc = acc_ref[h]
        o_lat = acc[:, :KV_RANK] / acc[:, KV_RANK:]
        outs.append(_dot(o_lat.astype(bf16), wuv_ref[h]))
    o_ref[...] = jnp.concatenate(outs, axis=1).astype(bf16)


def _t5_bucket(n):
    max_exact = N_BUCKETS // 2
    nf = jnp.maximum(n, 1).astype(f32)
    large = max_exact + (jnp.log(nf / max_exact) / math.log(MAX_DISTANCE / max_exact)
                         * (N_BUCKETS - max_exact)).astype(jnp.int32)
    large = jnp.minimum(large, N_BUCKETS - 1)
    return jnp.where(n < max_exact, n, large)


def _dsa(qa, qidx, widx, kidx, ckv, wuk_t, wuv_t, rel_bias, batch, seq):
    n = qa.shape[0]
    tq = ATT_BLOCK
    nq = seq // tq
    topk = min(INDEX_TOPK, seq // 4)
    span = 3 * tq - 1
    by_dist = rel_bias[_t5_bucket(jnp.maximum(jnp.arange(span) - tq + 1, 0))].astype(f32).T
    skew = jnp.tile(by_dist, (1, tq + 1))[:, :tq * (span + 1)].reshape(N_HEADS_A, tq, span + 1)
    bias_near = skew[:, :, :2 * tq][:, :, ::-1]
    far_n = np.float32(tq + 1)
    assert 16 + int(np.log(far_n / 16) / math.log(MAX_DISTANCE / 16) * 16) >= N_BUCKETS - 1
    bias_near = (bias_near - rel_bias[N_BUCKETS - 1].astype(f32)[:, None, None]) * LOG2E
    tri = (jnp.arange(tq)[:, None] <= jnp.arange(tq)[None, :]).astype(bf16)

    qrow = lambda b, i: (b * nq + i, 0)
    full = lambda b, i: (b, 0)
    c3 = lambda b, i: (0, 0, 0)
    c2 = lambda b, i: (0, 0)
    kern = functools.partial(_dsa_kernel, topk=topk)
    return pl.pallas_call(
        kern,
        grid=(batch, nq),
        in_specs=[pl.BlockSpec((tq, WIDTH_A), qrow),
                  pl.BlockSpec((tq, N_IDX_HEADS * IDX_DIM), qrow),
                  pl.BlockSpec((tq, LANES), qrow),
                  pl.BlockSpec((seq, IDX_DIM), full),
                  pl.BlockSpec((seq, 2 * KV_RANK), full),
                  pl.BlockSpec((N_HEADS_A, HEAD_DIM, KV_RANK), c3),
                  pl.BlockSpec((N_HEADS_A, KV_RANK, HEAD_DIM), c3),
                  pl.BlockSpec((N_HEADS_A, tq, 2 * tq), c3),
                  pl.BlockSpec((tq, tq), c2)],
        out_specs=pl.BlockSpec((tq, WIDTH_A), qrow),
        out_shape=jax.ShapeDtypeStruct((n, WIDTH_A), bf16),
        scratch_shapes=[pltpu.VMEM((nq + 1, tq, tq), jnp.int32),
                        pltpu.VMEM((N_HEADS_A, tq, KV_RANK), bf16),
                        pltpu.VMEM((N_IDX_HEADS, tq, LANES), f32),
                        pltpu.VMEM((tq, LANES), jnp.int32),
                        pltpu.VMEM((tq, LANES), f32),
                        pltpu.VMEM((tq, LANES), f32),
                        pltpu.VMEM((tq, 2 * tq), f32),
                        pltpu.VMEM((N_HEADS_A, tq, LANES), f32),
                        pltpu.VMEM((N_HEADS_A, tq, 2 * KV_RANK), f32)],
        compiler_params=pltpu.CompilerParams(dimension_semantics=("parallel", "arbitrary"),
                                             vmem_limit_bytes=VMEM_LIMIT),
        name="dsa",
    )(qa, qidx, widx, kidx, ckv, wuk_t, wuv_t, bias_near, tri)


def _sb_kernel(q_ref, k_ref, v_ref, tri_ref, o_ref, carry_ref, acc_ref):
    tq = sk = ATT_BLOCK
    i = pl.program_id(1)
    carry_ref[...] = jnp.zeros(carry_ref.shape, f32)
    acc_ref[...] = jnp.zeros(acc_ref.shape, f32)

    def step(j, diag):
        rows = pl.ds(pl.multiple_of(j * sk, sk), sk)
        if diag:
            strict = lax.broadcasted_iota(jnp.int32, (tq, sk), 1) < lax.broadcasted_iota(jnp.int32, (tq, sk), 0)
        heads = range(N_HEADS_B)
        lanes = [slice(h * HEAD_DIM, (h + 1) * HEAD_DIM) for h in heads]
        zs = [_dot_nt(q_ref[:, lanes[h]] * (HEAD_DIM ** -0.5), k_ref[rows, lanes[h]]) for h in heads]
        lfs = [-(jnp.maximum(z, 0.0) + jnp.log(1.0 + jnp.exp(-jnp.abs(z)))) for z in zs]
        if diag:
            lfs = [jnp.where(strict, lf, 0.0) for lf in lfs]
        splits = [_split_bf16(lf) for lf in lfs]
        laters = [_dot(hi, tri_ref[...]) + _dot(lo, tri_ref[...]) for hi, lo in splits]
        carries = [carry_ref[h] for h in heads]
        probs = [jnp.exp(zs[h] + lfs[h] + laters[h] + jnp.concatenate([carries[h]] * (sk // LANES), axis=1))
                 for h in heads]
        if diag:
            probs = [jnp.where(strict, a, 0.0) for a in probs]
        top = None
        for h in heads:
            acc_ref[h] += _dot(probs[h].astype(bf16), v_ref[rows, lanes[h]])
            carry = carries[h] + jnp.sum(lfs[h], axis=1, keepdims=True)
            carry_ref[h] = carry
            top = carry if top is None else jnp.maximum(top, carry)
        return jnp.max(top)

    def cond(st):
        j, top = st
        return jnp.logical_and(j >= 0, top > SB_SKIP)

    def body(st):
        j, _ = st
        return j - 1, step(j, False)

    lax.while_loop(cond, body, (i - 1, step(i, True)))
    o_ref[...] = jnp.concatenate([acc_ref[h] for h in range(N_HEADS_B)], axis=1).astype(bf16)


def _sb(qb, kb, vb, batch, seq):
    n = qb.shape[0]
    tq = ATT_BLOCK
    nq = seq // tq
    tri = (jnp.arange(tq)[:, None] > jnp.arange(tq)[None, :]).astype(bf16)
    qrow = lambda b, i: (b * nq + i, 0)
    full = lambda b, i: (b, 0)
    return pl.pallas_call(
        _sb_kernel,
        grid=(batch, nq),
        in_specs=[pl.BlockSpec((tq, WIDTH_B), qrow),
                  pl.BlockSpec((seq, WIDTH_B), full),
                  pl.BlockSpec((seq, WIDTH_B), full),
                  pl.BlockSpec((tq, tq), lambda b, i: (0, 0))],
        out_specs=pl.BlockSpec((tq, WIDTH_B), qrow),
        out_shape=jax.ShapeDtypeStruct((n, WIDTH_B), bf16),
        scratch_shapes=[pltpu.VMEM((N_HEADS_B, tq, LANES), f32),
                        pltpu.VMEM((N_HEADS_B, tq, HEAD_DIM), f32)],
        compiler_params=pltpu.CompilerParams(dimension_semantics=("parallel", "arbitrary"),
                                             vmem_limit_bytes=VMEM_LIMIT),
        name="sb",
    )(qb, kb, vb, tri)


def _merge_kernel(oa_ref, ob_ref, ga_ref, gb_ref, x_ref, gate_ref, shift_ref, scale_ref,
                  wa_ref, wb_ref, wo_ref, g_ref, b_ref, wr_hi_ref, wr_lo_ref, br_ref,
                  x1_ref, h2_ref, logit_ref):
    ya = _dot(oa_ref[...], wa_ref[...])
    yb = _dot(ob_ref[...], wb_ref[...])
    merged = jax.nn.sigmoid(ga_ref[...].astype(f32)) * ya + jax.nn.sigmoid(gb_ref[...].astype(f32)) * yb
    y = _dot(merged.astype(bf16), wo_ref[...])
    x1 = _ln(DN_ALPHA * x_ref[...] + gate_ref[0] * y) * g_ref[...] + b_ref[...]
    x1_ref[...] = x1
    h2 = _ln(x1) * (1.0 + scale_ref[0]) + shift_ref[0]
    hi, lo = _split_bf16(h2)
    h2_ref[...] = hi
    logit_ref[...] = (_dot(hi, wr_hi_ref[...]) + _dot(lo, wr_hi_ref[...]) + _dot(hi, wr_lo_ref[...])
                      + br_ref[...])


def _merge(oa, ob, ga, gb, x2, gate1, shift2, scale2, wa, wb, wo, g, b, wr_hi, wr_lo, br, seq):
    n, d = x2.shape
    tm = min(ROW_TILE, seq)
    per_b = seq // tm
    row = lambda i: (i, 0)
    bat = lambda i: (i // per_b, 0, 0)
    const = lambda i: (0, 0)
    return pl.pallas_call(
        _merge_kernel,
        grid=(n // tm,),
        in_specs=[pl.BlockSpec((tm, WIDTH_A), row), pl.BlockSpec((tm, WIDTH_B), row),
                  pl.BlockSpec((tm, d), row), pl.BlockSpec((tm, d), row), pl.BlockSpec((tm, d), row),
                  pl.BlockSpec((1, 1, d), bat), pl.BlockSpec((1, 1, d), bat), pl.BlockSpec((1, 1, d), bat),
                  pl.BlockSpec((WIDTH_A, d), const), pl.BlockSpec((WIDTH_B, d), const), pl.BlockSpec((d, d), const),
                  pl.BlockSpec((1, d), const), pl.BlockSpec((1, d), const),
                  pl.BlockSpec((d, LANES), const), pl.BlockSpec((d, LANES), const), pl.BlockSpec((1, LANES), const)],
        out_specs=[pl.BlockSpec((tm, d), row), pl.BlockSpec((tm, d), row), pl.BlockSpec((tm, LANES), row)],
        out_shape=[jax.ShapeDtypeStruct((n, d), f32), jax.ShapeDtypeStruct((n, d), bf16),
                   jax.ShapeDtypeStruct((n, LANES), f32)],
        compiler_params=pltpu.CompilerParams(dimension_semantics=("parallel",), vmem_limit_bytes=VMEM_LIMIT),
        name="merge",
    )(oa, ob, ga, gb, x2, gate1, shift2, scale2, wa, wb, wo, g, b, wr_hi, wr_lo, br)


def _expert_kernel(be_ref, nb_ref, x_ref, wgu_ref, bgu_ref, wdn_ref, bdn_ref, *rest, block0):
    y_ref, wgu_bf, wdn_bf = rest[-3:]
    i = pl.program_id(0)
    g = i + block0

    @pl.when(jnp.logical_or(i == 0, be_ref[g] != be_ref[jnp.maximum(g - 1, 0)]))
    def _():
        wgu_bf[...] = wgu_ref[0, 0].astype(bf16)
        wdn_bf[...] = wdn_ref[0, 0].astype(bf16)

    @pl.when(g < nb_ref[0])
    def _():
        gu = _dot(x_ref[...], wgu_bf[...]) + bgu_ref[0, 0]
        a = jnp.minimum(gu[:, :D_EXPERT], SWIGLU_LIMIT)
        u = jnp.clip(gu[:, D_EXPERT:], -SWIGLU_LIMIT, SWIGLU_LIMIT)
        act = (u + 1.0) * a * jax.nn.sigmoid(SWIGLU_ALPHA * a)
        y_ref[...] = (_dot(act.astype(bf16), wdn_bf[...]) + bdn_ref[0, 0]).astype(y_ref.dtype)

    @pl.when(g >= nb_ref[0])
    def _():
        y_ref[...] = jnp.zeros(y_ref.shape, y_ref.dtype)


def _experts(h2, row_token, block_expert, n_used, layer, wgu, bgu, wdn, bdn):
    rows, d = row_token.shape[0], h2.shape[1]
    n_blocks = rows // MOE_ROWS
    bounds = [n_blocks * c // MOE_CHUNKS for c in range(MOE_CHUNKS + 1)]
    ys = None
    for b0, b1 in zip(bounds[:-1], bounds[1:]):
        xs = h2[row_token[b0 * MOE_ROWS:b1 * MOE_ROWS]]
        e_idx = lambda i, be, nb, b0=b0: (layer, be[i + b0], 0, 0)
        in_specs = [pl.BlockSpec((MOE_ROWS, d), lambda i, be, nb: (i, 0)),
                    pl.BlockSpec((1, 1, d, 2 * D_EXPERT), e_idx),
                    pl.BlockSpec((1, 1, 1, 2 * D_EXPERT), e_idx),
                    pl.BlockSpec((1, 1, D_EXPERT, d), e_idx),
                    pl.BlockSpec((1, 1, 1, d), e_idx)]
        args = [block_expert, n_used, xs, wgu, bgu, wdn, bdn]
        aliases = {}
        if ys is not None:
            in_specs.append(pl.BlockSpec(memory_space=pl.ANY))
            aliases = {len(args): 0}
            args.append(ys)
        grid_spec = pltpu.PrefetchScalarGridSpec(
            num_scalar_prefetch=2,
            grid=(b1 - b0,),
            in_specs=in_specs,
            out_specs=pl.BlockSpec((MOE_ROWS, d), lambda i, be, nb, b0=b0: (i + b0, 0)),
            scratch_shapes=[pltpu.VMEM((d, 2 * D_EXPERT), bf16), pltpu.VMEM((D_EXPERT, d), bf16)],
        )
        ys = pl.pallas_call(
            functools.partial(_expert_kernel, block0=b0),
            grid_spec=grid_spec,
            out_shape=jax.ShapeDtypeStruct((rows, d), bf16),
            input_output_aliases=aliases,
            compiler_params=pltpu.CompilerParams(dimension_semantics=("arbitrary",), vmem_limit_bytes=VMEM_LIMIT),
            name="experts",
        )(*args)
    return ys


def _combine_kernel(x_ref, y_ref, gk_ref, gate_ref, g_ref, b_ref, *rest):
    o_ref = rest[-1]
    gk = gk_ref[...]
    y = y_ref[0].astype(f32) * gk[:, 0:1]
    for k in range(1, TOP_K):
        y = y + y_ref[k].astype(f32) * gk[:, k:k + 1]
    o_ref[...] = _ln(DN_ALPHA * x_ref[...] + gate_ref[0] * y) * g_ref[...] + b_ref[...]


def _combine(x1, ys, dest, gates, gate2, g, b, seq):
    n, d = x1.shape
    tm = min(ROW_TILE, seq)
    per_b = seq // tm
    tiles = n // tm
    chunks = min(COMBINE_CHUNKS, tiles)
    bounds = [tiles * c // chunks for c in range(chunks + 1)]
    const = lambda i: (0, 0)
    out = None
    for a0, a1 in zip(bounds[:-1], bounds[1:]):
        yk = ys[dest[a0 * tm:a1 * tm].T]
        row = lambda i, a0=a0: (i + a0, 0)
        in_specs = [pl.BlockSpec((tm, d), row),
                    pl.BlockSpec((TOP_K, tm, d), lambda i: (0, i, 0)),
                    pl.BlockSpec((tm, TOP_K), row),
                    pl.BlockSpec((1, 1, d), lambda i, a0=a0: ((i + a0) // per_b, 0, 0)),
                    pl.BlockSpec((1, d), const), pl.BlockSpec((1, d), const)]
        args = [x1, yk, gates, gate2, g, b]
        aliases = {}
        if out is not None:
            in_specs.append(pl.BlockSpec(memory_space=pl.ANY))
            aliases = {len(args): 0}
            args.append(out)
        out = pl.pallas_call(
            _combine_kernel,
            grid=(a1 - a0,),
            in_specs=in_specs,
            out_specs=pl.BlockSpec((tm, d), row),
            out_shape=jax.ShapeDtypeStruct((n, d), f32),
            input_output_aliases=aliases,
            compiler_params=pltpu.CompilerParams(dimension_semantics=("parallel",), vmem_limit_bytes=VMEM_LIMIT),
            name="combine",
        )(*args)
    return out


def _route_kernel(logit_ref, tri_ref, idx_ref, gate_ref, rank_ref, count_ref, carry_ref):
    @pl.when(pl.program_id(0) == 0)
    def _():
        carry_ref[...] = jnp.zeros(carry_ref.shape, f32)

    tm = logit_ref.shape[0]
    lane = lax.broadcasted_iota(jnp.int32, (tm, LANES), 1).astype(f32)
    v = jnp.where(lane < N_EXPERTS, logit_ref[...], -jnp.inf)
    tops, picks = [], []
    for _ in range(TOP_K):
        top = jnp.max(v, axis=1, keepdims=True)
        pick = jnp.min(jnp.where(v == top, lane, float(LANES)), axis=1, keepdims=True)
        tops.append(top)
        picks.append(pick)
        v = jnp.where(lane == pick, -jnp.inf, v)
    exps = [jnp.exp(top - tops[0]) for top in tops]
    denom = functools.reduce(jnp.add, exps)
    hits = [lane == pick for pick in picks]
    onehot = functools.reduce(jnp.add, [jnp.where(hit, 1.0, 0.0) for hit in hits])
    before = _dot(tri_ref[...], onehot.astype(bf16)) + carry_ref[...]
    ranks = [jnp.sum(jnp.where(hit, before, 0.0), axis=1, keepdims=True) for hit in hits]

    def spread(cols):
        return functools.reduce(jnp.add, [jnp.where(lane == float(k), col, 0.0) for k, col in enumerate(cols)])

    idx_ref[...] = spread(picks).astype(jnp.int32)
    gate_ref[...] = spread([e / denom for e in exps])
    rank_ref[...] = spread(ranks).astype(jnp.int32)
    carry_ref[...] += jnp.sum(onehot, axis=0, keepdims=True)
    count_ref[...] = carry_ref[...]


def _route(logits):
    n = logits.shape[0]
    tm = ROW_TILE
    tri = (jnp.arange(tm)[:, None] > jnp.arange(tm)[None, :]).astype(bf16)
    row = lambda i: (i, 0)
    const = lambda i: (0, 0)
    idx, gates, rank, counts = pl.pallas_call(
        _route_kernel,
        grid=(n // tm,),
        in_specs=[pl.BlockSpec((tm, LANES), row), pl.BlockSpec((tm, tm), const)],
        out_specs=[pl.BlockSpec((tm, LANES), row), pl.BlockSpec((tm, LANES), row), pl.BlockSpec((tm, LANES), row),
                   pl.BlockSpec((1, LANES), const)],
        out_shape=[jax.ShapeDtypeStruct((n, LANES), jnp.int32), jax.ShapeDtypeStruct((n, LANES), f32),
                   jax.ShapeDtypeStruct((n, LANES), jnp.int32), jax.ShapeDtypeStruct((1, LANES), f32)],
        scratch_shapes=[pltpu.VMEM((1, LANES), f32)],
        compiler_params=pltpu.CompilerParams(dimension_semantics=("arbitrary",), vmem_limit_bytes=VMEM_LIMIT),
        name="route",
    )(logits, tri)
    top_idx, gates, rank = idx[:, :TOP_K], gates[:, :TOP_K], rank[:, :TOP_K]
    counts = counts[0, :N_EXPERTS].astype(jnp.int32)
    padded = (counts + MOE_ROWS - 1) // MOE_ROWS * MOE_ROWS
    pad_end = jnp.cumsum(padded)
    pad_start = pad_end - padded
    dest = pad_start[top_idx] + rank
    n_blocks = -(-(n * TOP_K + N_EXPERTS * (MOE_ROWS - 1)) // MOE_ROWS)
    block_start = jnp.arange(n_blocks, dtype=jnp.int32) * MOE_ROWS
    block_expert = jnp.minimum(jnp.sum((pad_end[None, :] <= block_start[:, None]).astype(jnp.int32), axis=1),
                               N_EXPERTS - 1)
    n_used = (pad_end[-1:] // MOE_ROWS).astype(jnp.int32)
    nk = n * TOP_K
    order = jnp.sort(top_idx.reshape(-1) * nk + jnp.arange(nk, dtype=jnp.int32))
    sorted_token = (order % nk) // TOP_K
    first = jnp.cumsum(counts) - counts
    offset = (block_start - pad_start[block_expert])[:, None] + jnp.arange(MOE_ROWS, dtype=jnp.int32)[None, :]
    source = jnp.clip(first[block_expert][:, None] + offset, 0, nk - 1)
    row_token = jnp.where(offset < counts[block_expert][:, None], sorted_token[source], 0).reshape(-1)
    return gates, dest, row_token, block_expert, n_used


def _split_w_in(w):
    split = sum(COL_SPLITS[:5])
    return jnp.pad(w[:, :split], ((0, 0), (0, IDX_PAD))).astype(bf16), w[:, split:].astype(bf16)


def kernel(x, c, rel_bias, w_ada, b_ada, w_in, g_kv, w_uk, w_uv, w_a_out, w_b_out, w_o, ln1_g, ln1_b,
           w_router, b_router, w_gu, b_gu, w_dn, b_dn, ln2_g, ln2_b):
    B, S, D = x.shape
    N = B * S
    assert D == D_MODEL and S % ATT_BLOCK == 0
    cond = jax.nn.silu(c)
    x2 = x.reshape(N, D)
    for l in range(DEPTH):
        mod = jnp.dot(cond, w_ada[l], precision=lax.Precision.HIGHEST) + b_ada[l]
        shift1, scale1, gate1, shift2, scale2, gate2 = [m[:, None, :] for m in jnp.split(mod, 6, axis=-1)]

        qa, ckv, qidx, kidx, widx, qb, kb, vb, ga, gb = _proj(
            x2, shift1, scale1, *_split_w_in(w_in[l]), g_kv[l][None, :], S)
        wuk_t = jnp.transpose(w_uk[l], (1, 2, 0)).astype(bf16)
        wuv_t = jnp.transpose(w_uv[l], (1, 0, 2)).astype(bf16)
        oa = _dsa(qa, qidx, widx, kidx, ckv, wuk_t, wuv_t, rel_bias, B, S)
        ob = _sb(qb, kb, vb, B, S)

        wr = jnp.pad(w_router[l], ((0, 0), (0, LANES - N_EXPERTS)))
        wr_hi, wr_lo = _split_bf16(wr)
        br = jnp.pad(b_router[l], (0, LANES - N_EXPERTS))[None, :]
        x1, h2, logits = _merge(oa, ob, ga, gb, x2, gate1, shift2, scale2,
                                w_a_out[l].astype(bf16), w_b_out[l].astype(bf16), w_o[l].astype(bf16),
                                ln1_g[l][None, :], ln1_b[l][None, :], wr_hi, wr_lo, br, S)

        gates, dest, row_token, block_expert, n_used = _route(logits)
        ys = _experts(h2, row_token, block_expert, n_used, l, w_gu, b_gu[:, :, None, :], w_dn, b_dn[:, :, None, :])
        x2 = _combine(x1, ys, dest, gates, gate2, ln2_g[l][None, :], ln2_b[l][None, :], S)
    return x2.reshape(B, S, D)
```

```python
import functools
import math

import numpy as np
import jax
import jax.numpy as jnp
from jax import lax
from jax.experimental import pallas as pl
from jax.experimental.pallas import tpu as pltpu

D_MODEL = 1024
HEAD_DIM = 64
N_HEADS_A = 8
WIDTH_A = N_HEADS_A * HEAD_DIM
KV_RANK = 128
N_IDX_HEADS = 4
IDX_DIM = 64
INDEX_TOPK = 256
N_HEADS_B = 8
WIDTH_B = N_HEADS_B * HEAD_DIM
N_BUCKETS = 32
MAX_DISTANCE = 128
N_EXPERTS = 32
TOP_K = 4
D_EXPERT = 1024
SWIGLU_LIMIT = 7.0
SWIGLU_ALPHA = 1.702
LN_EPS = 1e-5
RMS_EPS = 1e-6
DEPTH = 2
DN_ALPHA = (2 * DEPTH) ** 0.25
IDX_SCALE = (N_IDX_HEADS ** -0.5) * (IDX_DIM ** -0.5)
LOG2E = math.log2(math.e)

COL_SPLITS = (WIDTH_A, KV_RANK, N_IDX_HEADS * IDX_DIM, IDX_DIM, N_IDX_HEADS,
              WIDTH_B, WIDTH_B, WIDTH_B, D_MODEL, D_MODEL)
N_COLS = sum(COL_SPLITS)

LANES = 128
VMEM_LIMIT = 56 * 1024 * 1024

IDX_PAD = LANES - IDX_DIM - N_IDX_HEADS
C_QA = 0
C_CKV = C_QA + WIDTH_A
C_QIDX = C_CKV + KV_RANK
C_KW = C_QIDX + N_IDX_HEADS * IDX_DIM
C_QB = C_KW + LANES
C_KB = C_QB + WIDTH_B
C_VB = C_KB + WIDTH_B
C_GA = C_VB + WIDTH_B
C_GB = C_GA + D_MODEL
N_COLS_PAD = C_GB + D_MODEL

ROW_TILE = 512
ATT_BLOCK = 256
BISECT_ROWS = 128
COUNT_UNROLL = 4
MOE_ROWS = 512
MOE_CHUNKS = 4
COMBINE_CHUNKS = 1
NEG_MASK = -1e30
M_INIT = -1e29
SB_SKIP = -110.0
INT_MIN = np.int32(-2 ** 31)

f32 = jnp.float32
bf16 = jnp.bfloat16


def _ln(x):
    mu = jnp.mean(x, axis=-1, keepdims=True)
    xc = x - mu
    var = jnp.mean(xc * xc, axis=-1, keepdims=True)
    return xc * lax.rsqrt(var + LN_EPS)


def _dot(a, b):
    return jnp.dot(a, b, preferred_element_type=f32)


def _dot_nt(a, b):
    return lax.dot_general(a, b, (((1,), (1,)), ((), ())), preferred_element_type=f32)


def _split_bf16(x):
    hi = x.astype(bf16)
    lo = (x - hi.astype(f32)).astype(bf16)
    return hi, lo


def _pack_halves(y):
    w = y.shape[1] // 2
    lo = pltpu.bitcast(y[:, :w].astype(bf16).astype(f32), jnp.uint32) >> 16
    hi = pltpu.bitcast(y[:, w:].astype(bf16).astype(f32), jnp.uint32) & jnp.uint32(0xFFFF0000)
    return hi | lo


def _unpack_halves(p):
    lo = pltpu.bitcast(p << 16, f32)
    hi = pltpu.bitcast(p & jnp.uint32(0xFFFF0000), f32)
    return jnp.concatenate([lo, hi], axis=1)


def _proj_kernel(x_ref, shift_ref, scale_ref, wa_ref, wb_ref, gkv_ref,
                 qa_ref, ckv_ref, qidx_ref, kidx_ref, widx_ref, qb_ref, kb_ref, vb_ref, ga_ref, gb_ref):
    h = _ln(x_ref[...]) * (1.0 + scale_ref[0]) + shift_ref[0]
    hb = h.astype(bf16)

    def mm(c0, width):
        if c0 < C_QB:
            return _dot(hb, wa_ref[:, c0:c0 + width])
        return _dot(hb, wb_ref[:, c0 - C_QB:c0 - C_QB + width])

    qa_ref[...] = mm(C_QA, WIDTH_A).astype(bf16)
    ckv = mm(C_CKV, KV_RANK)
    ckv = ckv * lax.rsqrt(jnp.mean(ckv * ckv, axis=-1, keepdims=True) + RMS_EPS) * gkv_ref[...]
    ckv_ref[...] = jnp.concatenate([ckv, jnp.ones_like(ckv)], axis=1).astype(bf16)
    qidx_ref[...] = mm(C_QIDX, N_IDX_HEADS * IDX_DIM).astype(bf16)
    kw = mm(C_KW, LANES)
    kidx_ref[...] = kw[:, :IDX_DIM].astype(bf16)
    widx_ref[...] = kw * IDX_SCALE
    qb_ref[...] = mm(C_QB, WIDTH_B).astype(bf16)
    kb_ref[...] = mm(C_KB, WIDTH_B).astype(bf16)
    vb_ref[...] = mm(C_VB, WIDTH_B).astype(bf16)
    for c in range(0, D_MODEL, 512):
        ga_ref[:, c:c + 512] = mm(C_GA + c, 512).astype(bf16)
        gb_ref[:, c:c + 512] = mm(C_GB + c, 512).astype(bf16)


def _proj(x2, shift, scale, w_dsa, w_rest, g_kv, seq):
    n, d = x2.shape
    tm = min(ROW_TILE, seq)
    per_b = seq // tm
    row = lambda i: (i, 0)
    bat = lambda i: (i // per_b, 0, 0)
    const = lambda i: (0, 0)
    widths = (WIDTH_A, 2 * KV_RANK, N_IDX_HEADS * IDX_DIM, IDX_DIM, LANES, WIDTH_B, WIDTH_B, WIDTH_B, D_MODEL, D_MODEL)
    dtypes = (bf16, bf16, bf16, bf16, f32, bf16, bf16, bf16, bf16, bf16)
    return pl.pallas_call(
        _proj_kernel,
        grid=(n // tm,),
        in_specs=[pl.BlockSpec((tm, d), row),
                  pl.BlockSpec((1, 1, d), bat),
                  pl.BlockSpec((1, 1, d), bat),
                  pl.BlockSpec((d, C_QB), const),
                  pl.BlockSpec((d, N_COLS_PAD - C_QB), const),
                  pl.BlockSpec((1, KV_RANK), const)],
        out_specs=[pl.BlockSpec((tm, w), row) for w in widths],
        out_shape=[jax.ShapeDtypeStruct((n, w), dt) for w, dt in zip(widths, dtypes)],
        compiler_params=pltpu.CompilerParams(dimension_semantics=("parallel",), vmem_limit_bytes=VMEM_LIMIT),
        name="proj",
    )(x2, shift, scale, w_dsa, w_rest, g_kv)


def _dsa_kernel(qa_ref, qidx_ref, widx_ref, kidx_ref, ckv_ref, wuk_ref, wuv_ref, bias_ref, tri_ref,
                o_ref, key_ref, qlat_ref, wrep_ref, thr_ref, need_ref, ceq_ref, mb_ref, m_ref, acc_ref, *, topk):
    tq = sk = ATT_BLOCK
    i = pl.program_id(1)
    n_blocks = i + 1

    for h in range(N_HEADS_A):
        ql = _dot(qa_ref[:, h * HEAD_DIM:(h + 1) * HEAD_DIM], wuk_ref[h]) * (HEAD_DIM ** -0.5 * LOG2E)
        qlat_ref[h] = ql.astype(bf16)

    rowi = lax.broadcasted_iota(jnp.int32, (tq, sk), 0)
    coli = lax.broadcasted_iota(jnp.int32, (tq, sk), 1)
    w = widx_ref[:, IDX_DIM:IDX_DIM + N_IDX_HEADS]
    for h in range(N_IDX_HEADS):
        wrep_ref[h] = jnp.broadcast_to(w[:, h:h + 1], (tq, LANES))

    def score_pair(p, carry):
        for u in range(2):
            j = 2 * p + u
            kblk = kidx_ref[pl.ds(pl.multiple_of(jnp.minimum(j, i) * sk, sk), sk), :]
            s = None
            for h in range(N_IDX_HEADS):
                d = _dot_nt(qidx_ref[:, h * IDX_DIM:(h + 1) * IDX_DIM], kblk)
                t = jnp.maximum(d, 0.0) * jnp.concatenate([wrep_ref[h]] * (sk // LANES), axis=1)
                s = t if s is None else s + t
            s = jnp.where(s == 0.0, 0.0, s)
            s = jnp.where((j - i) * sk + coli <= rowi, s, -jnp.inf)
            bits = pltpu.bitcast(s, jnp.int32)
            key_ref[j] = bits ^ ((bits >> 31) & np.int32(0x7FFFFFFF))
        return carry

    lax.fori_loop(0, (n_blocks + 1) // 2, score_pair, 0)
    for u in range(COUNT_UNROLL - 1):
        key_ref[n_blocks + u] = jnp.full((tq, sk), INT_MIN, jnp.int32)

    halves = range(0, tq, BISECT_ROWS)

    def count_ge(cands):
        accs = []
        for r0, cand in zip(halves, cands):
            cb = jnp.broadcast_to(cand, (BISECT_ROWS, LANES))

            def body(p, acc, r0=r0, cb=cb):
                for u in range(COUNT_UNROLL):
                    for c in range(0, sk, LANES):
                        kk = key_ref[COUNT_UNROLL * p + u, r0:r0 + BISECT_ROWS, c:c + LANES]
                        acc = acc + jnp.where(kk >= cb, 1.0, 0.0)
                return acc

            trips = (n_blocks + COUNT_UNROLL - 1) // COUNT_UNROLL
            accs.append(lax.fori_loop(0, trips, body, jnp.zeros((BISECT_ROWS, LANES), f32)))
        return [jnp.sum(acc, axis=1, keepdims=True) for acc in accs]

    def bit_step(b, answers):
        cands = [ans + jnp.left_shift(jnp.int32(1), 31 - b) for ans in answers]
        return tuple(jnp.where(cnt >= float(topk), cand, ans)
                     for cnt, cand, ans in zip(count_ge(cands), cands, answers))

    thrs = lax.fori_loop(0, 32, bit_step, tuple(jnp.full((BISECT_ROWS, 1), INT_MIN, jnp.int32) for _ in halves))
    for r0, thr, n_gt in zip(halves, thrs, count_ge([thr + 1 for thr in thrs])):
        thr_ref[r0:r0 + BISECT_ROWS, :] = jnp.broadcast_to(thr, (BISECT_ROWS, LANES))
        need_ref[r0:r0 + BISECT_ROWS, :] = jnp.broadcast_to(float(topk) - n_gt, (BISECT_ROWS, LANES))

    m_ref[...] = jnp.full(m_ref.shape, M_INIT, f32)
    acc_ref[...] = jnp.zeros(acc_ref.shape, f32)
    ceq_ref[...] = jnp.zeros(ceq_ref.shape, f32)

    def key_rows(j):
        return pl.ds(pl.multiple_of(j * sk, sk), sk)

    def mask_block(j, slot, diag):
        thr, need, ceq = thr_ref[...], need_ref[...], ceq_ref[...]
        keys = [key_ref[j, :, c:c + LANES] for c in range(0, sk, LANES)]
        eqs = [k == thr for k in keys]
        eqf = jnp.concatenate([jnp.where(e, 1.0, 0.0) for e in eqs], axis=1).astype(bf16)
        pref = _dot(eqf, tri_ref[...])
        total = jnp.broadcast_to(pref[:, sk - 1:sk], (tq, LANES))
        for n, (k, e) in enumerate(zip(keys, eqs)):
            c = n * LANES
            sel = jnp.logical_or(k > thr, jnp.logical_and(e, pref[:, c:c + LANES] + ceq <= need))
            if diag:
                ri = lax.broadcasted_iota(jnp.int32, (tq, LANES), 0)
                ci = lax.broadcasted_iota(jnp.int32, (tq, LANES), 1)
                sel = jnp.logical_and(sel, ci + c <= ri)
            mb_ref[:, slot * sk + c:slot * sk + c + LANES] = jnp.where(sel, 0.0, NEG_MASK)
        ceq_ref[...] = ceq + total

    def attend(j0, n_blk, near_col):
        for b in range(n_blk):
            mask_block(j0 + b, b, near_col is not None and b == n_blk - 1)
        heads, blks = range(N_HEADS_A), range(n_blk)
        logits = []
        for h in heads:
            q = qlat_ref[h]
            parts = []
            for b in blks:
                lg = _dot_nt(q, ckv_ref[key_rows(j0 + b), :KV_RANK]) + mb_ref[:, b * sk:(b + 1) * sk]
                if near_col is not None:
                    lg = lg + bias_ref[h, :, near_col + b * sk:near_col + (b + 1) * sk]
                parts.append(lg)
            logits.append(parts)
        m_olds = [m_ref[h] for h in heads]
        m_news = []
        for h in heads:
            mx = functools.reduce(jnp.maximum, [p[:, c:c + LANES] for p in logits[h] for c in range(0, sk, LANES)])
            m_news.append(jnp.maximum(m_olds[h], jnp.max(mx, axis=1, keepdims=True)))
        probs = []
        for h in heads:
            m_wide = jnp.concatenate([m_news[h]] * (sk // LANES), axis=1)
            probs.append([jnp.exp2(logits[h][b] - m_wide).astype(bf16) for b in blks])
        for h in heads:
            pv = None
            for b in blks:
                d = _dot(probs[h][b], ckv_ref[key_rows(j0 + b), :])
                pv = d if pv is None else pv + d
            alpha = jnp.exp2(m_olds[h] - m_news[h])
            acc_ref[h] = jnp.concatenate([alpha, alpha], axis=1) * acc_ref[h] + pv
            m_ref[h] = m_news[h]

    lone = jnp.logical_and(i >= 2, i % 2 == 0)

    @pl.when(lone)
    def _():
        attend(0, 1, None)

    first = jnp.where(lone, 1, 0)

    def far_pair(p, carry):
        attend(first + 2 * p, 2, None)
        return carry

    lax.fori_loop(0, (i - 1 - first) // 2, far_pair, 0)

    @pl.when(i >= 1)
    def _():
        attend(i - 1, 2, 0)

    @pl.when(i == 0)
    def _():
        attend(0, 1, sk)

    outs = []
    for h in range(N_HEADS_A):
        acc = acc_ref[h]
        o_lat = acc[:, :KV_RANK] / acc[:, KV_RANK:]
        outs.append(_dot(o_lat.astype(bf16), wuv_ref[h]))
    o_ref[...] = jnp.concatenate(outs, axis=1).astype(bf16)


def _t5_bucket(n):
    max_exact = N_BUCKETS // 2
    nf = jnp.maximum(n, 1).astype(f32)
    large = max_exact + (jnp.log(nf / max_exact) / math.log(MAX_DISTANCE / max_exact)
                         * (N_BUCKETS - max_exact)).astype(jnp.int32)
    large = jnp.minimum(large, N_BUCKETS - 1)
    return jnp.where(n < max_exact, n, large)


def _dsa(qa, qidx, widx, kidx, ckv, wuk_t, wuv_t, rel_bias, batch, seq):
    n = qa.shape[0]
    tq = ATT_BLOCK
    nq = seq // tq
    topk = min(INDEX_TOPK, seq // 4)
    span = 3 * tq - 1
    by_dist = rel_bias[_t5_bucket(jnp.maximum(jnp.arange(span) - tq + 1, 0))].astype(f32).T
    skew = jnp.tile(by_dist, (1, tq + 1))[:, :tq * (span + 1)].reshape(N_HEADS_A, tq, span + 1)
    bias_near = skew[:, :, :2 * tq][:, :, ::-1]
    far_n = np.float32(tq + 1)
    assert 16 + int(np.log(far_n / 16) / math.log(MAX_DISTANCE / 16) * 16) >= N_BUCKETS - 1
    bias_near = (bias_near - rel_bias[N_BUCKETS - 1].astype(f32)[:, None, None]) * LOG2E
    tri = (jnp.arange(tq)[:, None] <= jnp.arange(tq)[None, :]).astype(bf16)

    qrow = lambda b, i: (b * nq + i, 0)
    full = lambda b, i: (b, 0)
    c3 = lambda b, i: (0, 0, 0)
    c2 = lambda b, i: (0, 0)
    kern = functools.partial(_dsa_kernel, topk=topk)
    return pl.pallas_call(
        kern,
        grid=(batch, nq),
        in_specs=[pl.BlockSpec((tq, WIDTH_A), qrow),
                  pl.BlockSpec((tq, N_IDX_HEADS * IDX_DIM), qrow),
                  pl.BlockSpec((tq, LANES), qrow),
                  pl.BlockSpec((seq, IDX_DIM), full),
                  pl.BlockSpec((seq, 2 * KV_RANK), full),
                  pl.BlockSpec((N_HEADS_A, HEAD_DIM, KV_RANK), c3),
                  pl.BlockSpec((N_HEADS_A, KV_RANK, HEAD_DIM), c3),
                  pl.BlockSpec((N_HEADS_A, tq, 2 * tq), c3),
                  pl.BlockSpec((tq, tq), c2)],
        out_specs=pl.BlockSpec((tq, WIDTH_A), qrow),
        out_shape=jax.ShapeDtypeStruct((n, WIDTH_A), bf16),
        scratch_shapes=[pltpu.VMEM((nq + COUNT_UNROLL - 1, tq, tq), jnp.int32),
                        pltpu.VMEM((N_HEADS_A, tq, KV_RANK), bf16),
                        pltpu.VMEM((N_IDX_HEADS, tq, LANES), f32),
                        pltpu.VMEM((tq, LANES), jnp.int32),
                        pltpu.VMEM((tq, LANES), f32),
                        pltpu.VMEM((tq, LANES), f32),
                        pltpu.VMEM((tq, 2 * tq), f32),
                        pltpu.VMEM((N_HEADS_A, tq, LANES), f32),
                        pltpu.VMEM((N_HEADS_A, tq, 2 * KV_RANK), f32)],
        compiler_params=pltpu.CompilerParams(dimension_semantics=("parallel", "arbitrary"),
                                             vmem_limit_bytes=VMEM_LIMIT),
        name="dsa",
    )(qa, qidx, widx, kidx, ckv, wuk_t, wuv_t, bias_near, tri)


def _sb_kernel(q_ref, k_ref, v_ref, tri_ref, o_ref, carry_ref, acc_ref):
    tq = sk = ATT_BLOCK
    i = pl.program_id(1)
    carry_ref[...] = jnp.zeros(carry_ref.shape, f32)
    acc_ref[...] = jnp.zeros(acc_ref.shape, f32)

    def step(j, diag):
        rows = pl.ds(pl.multiple_of(j * sk, sk), sk)
        if diag:
            strict = lax.broadcasted_iota(jnp.int32, (tq, sk), 1) < lax.broadcasted_iota(jnp.int32, (tq, sk), 0)
        heads = range(N_HEADS_B)
        lanes = [slice(h * HEAD_DIM, (h + 1) * HEAD_DIM) for h in heads]
        zs = [_dot_nt(q_ref[:, lanes[h]] * (HEAD_DIM ** -0.5), k_ref[rows, lanes[h]]) for h in heads]
        lfs = [-(jnp.maximum(z, 0.0) + jnp.log(1.0 + jnp.exp(-jnp.abs(z)))) for z in zs]
        if diag:
            lfs = [jnp.where(strict, lf, 0.0) for lf in lfs]
        splits = [_split_bf16(lf) for lf in lfs]
        laters = [_dot(hi, tri_ref[...]) + _dot(lo, tri_ref[...]) for hi, lo in splits]
        carries = [carry_ref[h] for h in heads]
        probs = [jnp.exp(zs[h] + lfs[h] + laters[h] + jnp.concatenate([carries[h]] * (sk // LANES), axis=1))
                 for h in heads]
        if diag:
            probs = [jnp.where(strict, a, 0.0) for a in probs]
        top = None
        for h in heads:
            acc_ref[h] += _dot(probs[h].astype(bf16), v_ref[rows, lanes[h]])
            carry = carries[h] + jnp.sum(lfs[h], axis=1, keepdims=True)
            carry_ref[h] = carry
            top = carry if top is None else jnp.maximum(top, carry)
        return jnp.max(top)

    def cond(st):
        j, top = st
        return jnp.logical_and(j >= 0, top > SB_SKIP)

    def body(st):
        j, _ = st
        return j - 1, step(j, False)

    lax.while_loop(cond, body, (i - 1, step(i, True)))
    o_ref[...] = jnp.concatenate([acc_ref[h] for h in range(N_HEADS_B)], axis=1).astype(bf16)


def _sb(qb, kb, vb, batch, seq):
    n = qb.shape[0]
    tq = ATT_BLOCK
    nq = seq // tq
    tri = (jnp.arange(tq)[:, None] > jnp.arange(tq)[None, :]).astype(bf16)
    qrow = lambda b, i: (b * nq + i, 0)
    full = lambda b, i: (b, 0)
    return pl.pallas_call(
        _sb_kernel,
        grid=(batch, nq),
        in_specs=[pl.BlockSpec((tq, WIDTH_B), qrow),
                  pl.BlockSpec((seq, WIDTH_B), full),
                  pl.BlockSpec((seq, WIDTH_B), full),
                  pl.BlockSpec((tq, tq), lambda b, i: (0, 0))],
        out_specs=pl.BlockSpec((tq, WIDTH_B), qrow),
        out_shape=jax.ShapeDtypeStruct((n, WIDTH_B), bf16),
        scratch_shapes=[pltpu.VMEM((N_HEADS_B, tq, LANES), f32),
                        pltpu.VMEM((N_HEADS_B, tq, HEAD_DIM), f32)],
        compiler_params=pltpu.CompilerParams(dimension_semantics=("parallel", "arbitrary"),
                                             vmem_limit_bytes=VMEM_LIMIT),
        name="sb",
    )(qb, kb, vb, tri)


def _merge_kernel(oa_ref, ob_ref, ga_ref, gb_ref, x_ref, gate_ref, shift_ref, scale_ref,
                  wa_ref, wb_ref, wo_ref, g_ref, b_ref, wr_hi_ref, wr_lo_ref, br_ref,
                  x1_ref, h2_ref, logit_ref):
    ya = _dot(oa_ref[...], wa_ref[...])
    yb = _dot(ob_ref[...], wb_ref[...])
    merged = jax.nn.sigmoid(ga_ref[...].astype(f32)) * ya + jax.nn.sigmoid(gb_ref[...].astype(f32)) * yb
    y = _dot(merged.astype(bf16), wo_ref[...])
    x1 = _ln(DN_ALPHA * x_ref[...] + gate_ref[0] * y) * g_ref[...] + b_ref[...]
    x1_ref[...] = x1
    h2 = _ln(x1) * (1.0 + scale_ref[0]) + shift_ref[0]
    hi, lo = _split_bf16(h2)
    h2_ref[...] = hi
    logit_ref[...] = (_dot(hi, wr_hi_ref[...]) + _dot(lo, wr_hi_ref[...]) + _dot(hi, wr_lo_ref[...])
                      + br_ref[...])


def _merge(oa, ob, ga, gb, x2, gate1, shift2, scale2, wa, wb, wo, g, b, wr_hi, wr_lo, br, seq):
    n, d = x2.shape
    tm = min(ROW_TILE, seq)
    per_b = seq // tm
    row = lambda i: (i, 0)
    bat = lambda i: (i // per_b, 0, 0)
    const = lambda i: (0, 0)
    return pl.pallas_call(
        _merge_kernel,
        grid=(n // tm,),
        in_specs=[pl.BlockSpec((tm, WIDTH_A), row), pl.BlockSpec((tm, WIDTH_B), row),
                  pl.BlockSpec((tm, d), row), pl.BlockSpec((tm, d), row), pl.BlockSpec((tm, d), row),
                  pl.BlockSpec((1, 1, d), bat), pl.BlockSpec((1, 1, d), bat), pl.BlockSpec((1, 1, d), bat),
                  pl.BlockSpec((WIDTH_A, d), const), pl.BlockSpec((WIDTH_B, d), const), pl.BlockSpec((d, d), const),
                  pl.BlockSpec((1, d), const), pl.BlockSpec((1, d), const),
                  pl.BlockSpec((d, LANES), const), pl.BlockSpec((d, LANES), const), pl.BlockSpec((1, LANES), const)],
        out_specs=[pl.BlockSpec((tm, d), row), pl.BlockSpec((tm, d), row), pl.BlockSpec((tm, LANES), row)],
        out_shape=[jax.ShapeDtypeStruct((n, d), f32), jax.ShapeDtypeStruct((n, d), bf16),
                   jax.ShapeDtypeStruct((n, LANES), f32)],
        compiler_params=pltpu.CompilerParams(dimension_semantics=("parallel",), vmem_limit_bytes=VMEM_LIMIT),
        name="merge",
    )(oa, ob, ga, gb, x2, gate1, shift2, scale2, wa, wb, wo, g, b, wr_hi, wr_lo, br)


def _expert_kernel(be_ref, nb_ref, x_ref, wgu_ref, bgu_ref, wdn_ref, bdn_ref, *rest, block0):
    y_ref, wgu_bf, wdn_bf = rest[-3:]
    i = pl.program_id(0)
    g = i + block0

    @pl.when(jnp.logical_or(i == 0, be_ref[g] != be_ref[jnp.maximum(g - 1, 0)]))
    def _():
        wgu_bf[...] = wgu_ref[0, 0].astype(bf16)
        wdn_bf[...] = wdn_ref[0, 0].astype(bf16)

    @pl.when(g < nb_ref[0])
    def _():
        gu = _dot(x_ref[...], wgu_bf[...]) + bgu_ref[0, 0]
        a = jnp.minimum(gu[:, :D_EXPERT], SWIGLU_LIMIT)
        u = jnp.clip(gu[:, D_EXPERT:], -SWIGLU_LIMIT, SWIGLU_LIMIT)
        act = (u + 1.0) * a * jax.nn.sigmoid(SWIGLU_ALPHA * a)
        y_ref[...] = _pack_halves(_dot(act.astype(bf16), wdn_bf[...]) + bdn_ref[0, 0])

    @pl.when(g >= nb_ref[0])
    def _():
        y_ref[...] = jnp.zeros(y_ref.shape, y_ref.dtype)


def _experts(h2, row_token, block_expert, n_used, layer, wgu, bgu, wdn, bdn):
    rows, d = row_token.shape[0], h2.shape[1]
    n_blocks = rows // MOE_ROWS
    bounds = [n_blocks * c // MOE_CHUNKS for c in range(MOE_CHUNKS + 1)]
    ys = None
    for b0, b1 in zip(bounds[:-1], bounds[1:]):
        xs = h2[row_token[b0 * MOE_ROWS:b1 * MOE_ROWS]]
        e_idx = lambda i, be, nb, b0=b0: (layer, be[i + b0], 0, 0)
        in_specs = [pl.BlockSpec((MOE_ROWS, d), lambda i, be, nb: (i, 0)),
                    pl.BlockSpec((1, 1, d, 2 * D_EXPERT), e_idx),
                    pl.BlockSpec((1, 1, 1, 2 * D_EXPERT), e_idx),
                    pl.BlockSpec((1, 1, D_EXPERT, d), e_idx),
                    pl.BlockSpec((1, 1, 1, d), e_idx)]
        args = [block_expert, n_used, xs, wgu, bgu, wdn, bdn]
        aliases = {}
        if ys is not None:
            in_specs.append(pl.BlockSpec(memory_space=pl.ANY))
            aliases = {len(args): 0}
            args.append(ys)
        grid_spec = pltpu.PrefetchScalarGridSpec(
            num_scalar_prefetch=2,
            grid=(b1 - b0,),
            in_specs=in_specs,
            out_specs=pl.BlockSpec((MOE_ROWS, d // 2), lambda i, be, nb, b0=b0: (i + b0, 0)),
            scratch_shapes=[pltpu.VMEM((d, 2 * D_EXPERT), bf16), pltpu.VMEM((D_EXPERT, d), bf16)],
        )
        ys = pl.pallas_call(
            functools.partial(_expert_kernel, block0=b0),
            grid_spec=grid_spec,
            out_shape=jax.ShapeDtypeStruct((rows, d // 2), jnp.uint32),
            input_output_aliases=aliases,
            compiler_params=pltpu.CompilerParams(dimension_semantics=("arbitrary",), vmem_limit_bytes=VMEM_LIMIT),
            name="experts",
        )(*args)
    return ys


def _combine_kernel(x_ref, y_ref, gk_ref, gate_ref, g_ref, b_ref, *rest):
    o_ref = rest[-1]
    gk = gk_ref[...]
    y = _unpack_halves(y_ref[0]) * gk[:, 0:1]
    for k in range(1, TOP_K):
        y = y + _unpack_halves(y_ref[k]) * gk[:, k:k + 1]
    o_ref[...] = _ln(DN_ALPHA * x_ref[...] + gate_ref[0] * y) * g_ref[...] + b_ref[...]


def _combine(x1, ys, dest, gates, gate2, g, b, seq):
    n, d = x1.shape
    tm = min(ROW_TILE, seq)
    per_b = seq // tm
    tiles = n // tm
    chunks = min(COMBINE_CHUNKS, tiles)
    bounds = [tiles * c // chunks for c in range(chunks + 1)]
    const = lambda i: (0, 0)
    out = None
    for a0, a1 in zip(bounds[:-1], bounds[1:]):
        yk = ys[dest[a0 * tm:a1 * tm].T]
        row = lambda i, a0=a0: (i + a0, 0)
        in_specs = [pl.BlockSpec((tm, d), row),
                    pl.BlockSpec((TOP_K, tm, d // 2), lambda i: (0, i, 0)),
                    pl.BlockSpec((tm, TOP_K), row),
                    pl.BlockSpec((1, 1, d), lambda i, a0=a0: ((i + a0) // per_b, 0, 0)),
                    pl.BlockSpec((1, d), const), pl.BlockSpec((1, d), const)]
        args = [x1, yk, gates, gate2, g, b]
        aliases = {}
        if out is not None:
            in_specs.append(pl.BlockSpec(memory_space=pl.ANY))
            aliases = {len(args): 0}
            args.append(out)
        out = pl.pallas_call(
            _combine_kernel,
            grid=(a1 - a0,),
            in_specs=in_specs,
            out_specs=pl.BlockSpec((tm, d), row),
            out_shape=jax.ShapeDtypeStruct((n, d), f32),
            input_output_aliases=aliases,
            compiler_params=pltpu.CompilerParams(dimension_semantics=("parallel",), vmem_limit_bytes=VMEM_LIMIT),
            name="combine",
        )(*args)
    return out


def _route_kernel(logit_ref, tri_ref, idx_ref, gate_ref, rank_ref, count_ref, carry_ref):
    @pl.when(pl.program_id(0) == 0)
    def _():
        carry_ref[...] = jnp.zeros(carry_ref.shape, f32)

    tm = logit_ref.shape[0]
    lane = lax.broadcasted_iota(jnp.int32, (tm, LANES), 1).astype(f32)
    v = jnp.where(lane < N_EXPERTS, logit_ref[...], -jnp.inf)
    tops, picks = [], []
    for _ in range(TOP_K):
        top = jnp.max(v, axis=1, keepdims=True)
        pick = jnp.min(jnp.where(v == top, lane, float(LANES)), axis=1, keepdims=True)
        tops.append(top)
        picks.append(pick)
        v = jnp.where(lane == pick, -jnp.inf, v)
    exps = [jnp.exp(top - tops[0]) for top in tops]
    denom = functools.reduce(jnp.add, exps)
    hits = [lane == pick for pick in picks]
    onehot = functools.reduce(jnp.add, [jnp.where(hit, 1.0, 0.0) for hit in hits])
    before = _dot(tri_ref[...], onehot.astype(bf16)) + carry_ref[...]
    ranks = [jnp.sum(jnp.where(hit, before, 0.0), axis=1, keepdims=True) for hit in hits]

    def spread(cols):
        return functools.reduce(jnp.add, [jnp.where(lane == float(k), col, 0.0) for k, col in enumerate(cols)])

    idx_ref[...] = spread(picks).astype(jnp.int32)
    gate_ref[...] = spread([e / denom for e in exps])
    rank_ref[...] = spread(ranks).astype(jnp.int32)
    carry_ref[...] += jnp.sum(onehot, axis=0, keepdims=True)
    count_ref[...] = carry_ref[...]


def _route(logits):
    n = logits.shape[0]
    tm = ROW_TILE
    tri = (jnp.arange(tm)[:, None] > jnp.arange(tm)[None, :]).astype(bf16)
    row = lambda i: (i, 0)
    const = lambda i: (0, 0)
    idx, gates, rank, counts = pl.pallas_call(
        _route_kernel,
        grid=(n // tm,),
        in_specs=[pl.BlockSpec((tm, LANES), row), pl.BlockSpec((tm, tm), const)],
        out_specs=[pl.BlockSpec((tm, LANES), row), pl.BlockSpec((tm, LANES), row), pl.BlockSpec((tm, LANES), row),
                   pl.BlockSpec((1, LANES), const)],
        out_shape=[jax.ShapeDtypeStruct((n, LANES), jnp.int32), jax.ShapeDtypeStruct((n, LANES), f32),
                   jax.ShapeDtypeStruct((n, LANES), jnp.int32), jax.ShapeDtypeStruct((1, LANES), f32)],
        scratch_shapes=[pltpu.VMEM((1, LANES), f32)],
        compiler_params=pltpu.CompilerParams(dimension_semantics=("arbitrary",), vmem_limit_bytes=VMEM_LIMIT),
        name="route",
    )(logits, tri)
    top_idx, gates, rank = idx[:, :TOP_K], gates[:, :TOP_K], rank[:, :TOP_K]
    counts = counts[0, :N_EXPERTS].astype(jnp.int32)
    padded = (counts + MOE_ROWS - 1) // MOE_ROWS * MOE_ROWS
    pad_end = jnp.cumsum(padded)
    pad_start = pad_end - padded
    dest = pad_start[top_idx] + rank
    n_blocks = -(-(n * TOP_K + N_EXPERTS * (MOE_ROWS - 1)) // MOE_ROWS)
    block_start = jnp.arange(n_blocks, dtype=jnp.int32) * MOE_ROWS
    block_expert = jnp.minimum(jnp.sum((pad_end[None, :] <= block_start[:, None]).astype(jnp.int32), axis=1),
                               N_EXPERTS - 1)
    n_used = (pad_end[-1:] // MOE_ROWS).astype(jnp.int32)
    nk = n * TOP_K
    order = jnp.sort(top_idx.reshape(-1) * nk + jnp.arange(nk, dtype=jnp.int32))
    sorted_token = (order % nk) // TOP_K
    first = jnp.cumsum(counts) - counts
    offset = (block_start - pad_start[block_expert])[:, None] + jnp.arange(MOE_ROWS, dtype=jnp.int32)[None, :]
    source = jnp.clip(first[block_expert][:, None] + offset, 0, nk - 1)
    row_token = jnp.where(offset < counts[block_expert][:, None], sorted_token[source], 0).reshape(-1)
    return gates, dest, row_token, block_expert, n_used


def _split_w_in(w):
    split = sum(COL_SPLITS[:5])
    return jnp.pad(w[:, :split], ((0, 0), (0, IDX_PAD))).astype(bf16), w[:, split:].astype(bf16)


def kernel(x, c, rel_bias, w_ada, b_ada, w_in, g_kv, w_uk, w_uv, w_a_out, w_b_out, w_o, ln1_g, ln1_b,
           w_router, b_router, w_gu, b_gu, w_dn, b_dn, ln2_g, ln2_b):
    B, S, D = x.shape
    N = B * S
    assert D == D_MODEL and S % ATT_BLOCK == 0
    cond = jax.nn.silu(c)
    x2 = x.reshape(N, D)
    for l in range(DEPTH):
        mod = jnp.dot(cond, w_ada[l], precision=lax.Precision.HIGHEST) + b_ada[l]
        shift1, scale1, gate1, shift2, scale2, gate2 = [m[:, None, :] for m in jnp.split(mod, 6, axis=-1)]

        qa, ckv, qidx, kidx, widx, qb, kb, vb, ga, gb = _proj(
            x2, shift1, scale1, *_split_w_in(w_in[l]), g_kv[l][None, :], S)
        wuk_t = jnp.transpose(w_uk[l], (1, 2, 0)).astype(bf16)
        wuv_t = jnp.transpose(w_uv[l], (1, 0, 2)).astype(bf16)
        oa = _dsa(qa, qidx, widx, kidx, ckv, wuk_t, wuv_t, rel_bias, B, S)
        ob = _sb(qb, kb, vb, B, S)

        wr = jnp.pad(w_router[l], ((0, 0), (0, LANES - N_EXPERTS)))
        wr_hi, wr_lo = _split_bf16(wr)
        br = jnp.pad(b_router[l], (0, LANES - N_EXPERTS))[None, :]
        x1, h2, logits = _merge(oa, ob, ga, gb, x2, gate1, shift2, scale2,
                                w_a_out[l].astype(bf16), w_b_out[l].astype(bf16), w_o[l].astype(bf16),
                                ln1_g[l][None, :], ln1_b[l][None, :], wr_hi, wr_lo, br, S)

        gates, dest, row_token, block_expert, n_used = _route(logits)
        ys = _experts(h2, row_token, block_expert, n_used, l, w_gu, b_gu[:, :, None, :], w_dn, b_dn[:, :, None, :])
        x2 = _combine(x1, ys, dest, gates, gate2, ln2_g[l][None, :], ln2_b[l][None, :], S)
    return x2.reshape(B, S, D)
```

```python
import functools
import math

import numpy as np
import jax
import jax.numpy as jnp
from jax import lax
from jax.experimental import pallas as pl
from jax.experimental.pallas import tpu as pltpu

D_MODEL = 1024
HEAD_DIM = 64
N_HEADS_A = 8
WIDTH_A = N_HEADS_A * HEAD_DIM
KV_RANK = 128
N_IDX_HEADS = 4
IDX_DIM = 64
INDEX_TOPK = 256
N_HEADS_B = 8
WIDTH_B = N_HEADS_B * HEAD_DIM
N_BUCKETS = 32
MAX_DISTANCE = 128
N_EXPERTS = 32
TOP_K = 4
D_EXPERT = 1024
SWIGLU_LIMIT = 7.0
SWIGLU_ALPHA = 1.702
LN_EPS = 1e-5
RMS_EPS = 1e-6
DEPTH = 2
DN_ALPHA = (2 * DEPTH) ** 0.25
IDX_SCALE = (N_IDX_HEADS ** -0.5) * (IDX_DIM ** -0.5)
LOG2E = math.log2(math.e)

COL_SPLITS = (WIDTH_A, KV_RANK, N_IDX_HEADS * IDX_DIM, IDX_DIM, N_IDX_HEADS,
              WIDTH_B, WIDTH_B, WIDTH_B, D_MODEL, D_MODEL)
N_COLS = sum(COL_SPLITS)

LANES = 128
VMEM_LIMIT = 56 * 1024 * 1024

IDX_PAD = LANES - IDX_DIM - N_IDX_HEADS
C_QA = 0
C_CKV = C_QA + WIDTH_A
C_QIDX = C_CKV + KV_RANK
C_KW = C_QIDX + N_IDX_HEADS * IDX_DIM
C_QB = C_KW + LANES
C_KB = C_QB + WIDTH_B
C_VB = C_KB + WIDTH_B
C_GA = C_VB + WIDTH_B
C_GB = C_GA + D_MODEL
N_COLS_PAD = C_GB + D_MODEL

ROW_TILE = 512
ATT_BLOCK = 256
BISECT_ROWS = 128
COUNT_UNROLL = 4
MOE_ROWS = 512
MOE_CHUNKS = 4
COMBINE_CHUNKS = 1
NEG_MASK = -1e30
M_INIT = -1e29
SB_SKIP = -110.0
INT_MIN = np.int32(-2 ** 31)

f32 = jnp.float32
bf16 = jnp.bfloat16


def _ln(x):
    mu = jnp.mean(x, axis=-1, keepdims=True)
    xc = x - mu
    var = jnp.mean(xc * xc, axis=-1, keepdims=True)
    return xc * lax.rsqrt(var + LN_EPS)


def _dot(a, b):
    return jnp.dot(a, b, preferred_element_type=f32)


def _dot_nt(a, b):
    return lax.dot_general(a, b, (((1,), (1,)), ((), ())), preferred_element_type=f32)


def _split_bf16(x):
    hi = x.astype(bf16)
    lo = (x - hi.astype(f32)).astype(bf16)
    return hi, lo


def _pack_halves(y):
    w = y.shape[1] // 2
    lo = pltpu.bitcast(y[:, :w].astype(bf16).astype(f32), jnp.uint32) >> 16
    hi = pltpu.bitcast(y[:, w:].astype(bf16).astype(f32), jnp.uint32) & jnp.uint32(0xFFFF0000)
    return hi | lo


def _unpack_halves(p):
    lo = pltpu.bitcast(p << 16, f32)
    hi = pltpu.bitcast(p & jnp.uint32(0xFFFF0000), f32)
    return jnp.concatenate([lo, hi], axis=1)


def _ada_kernel(c_ref, w_ref, b_ref, o_ref):
    c = c_ref[...]
    c_hi, c_lo = _split_bf16(c * jax.nn.sigmoid(c))
    w_hi, w_lo = _split_bf16(w_ref[0])
    o_ref[...] = _dot(c_hi, w_hi) + _dot(c_lo, w_hi) + _dot(c_hi, w_lo) + b_ref[0]


def _ada(c, layer, w_ada, b_ada):
    batch, d = c.shape
    cols = w_ada.shape[2]
    return pl.pallas_call(
        _ada_kernel,
        grid=(cols // d,),
        in_specs=[pl.BlockSpec((batch, d), lambda j: (0, 0)),
                  pl.BlockSpec((1, d, d), lambda j: (layer, 0, j)),
                  pl.BlockSpec((1, 1, d), lambda j: (layer, 0, j))],
        out_specs=pl.BlockSpec((batch, d), lambda j: (0, j)),
        out_shape=jax.ShapeDtypeStruct((batch, cols), f32),
        compiler_params=pltpu.CompilerParams(dimension_semantics=("parallel",), vmem_limit_bytes=VMEM_LIMIT),
        name="ada",
    )(c, w_ada, b_ada[:, None, :])


def _proj_kernel(x_ref, shift_ref, scale_ref, wa_ref, wb_ref, gkv_ref,
                 qa_ref, ckv_ref, qidx_ref, kidx_ref, widx_ref, qb_ref, kb_ref, vb_ref, ga_ref, gb_ref):
    h = _ln(x_ref[...]) * (1.0 + scale_ref[0]) + shift_ref[0]
    hb = h.astype(bf16)

    def mm(c0, width):
        if c0 < C_QB:
            return _dot(hb, wa_ref[:, c0:c0 + width])
        return _dot(hb, wb_ref[:, c0 - C_QB:c0 - C_QB + width])

    qa_ref[...] = mm(C_QA, WIDTH_A).astype(bf16)
    ckv = mm(C_CKV, KV_RANK)
    ckv = ckv * lax.rsqrt(jnp.mean(ckv * ckv, axis=-1, keepdims=True) + RMS_EPS) * gkv_ref[...]
    ckv_ref[...] = jnp.concatenate([ckv, jnp.ones_like(ckv)], axis=1).astype(bf16)
    qidx_ref[...] = mm(C_QIDX, N_IDX_HEADS * IDX_DIM).astype(bf16)
    kw = mm(C_KW, LANES)
    kidx_ref[...] = kw[:, :IDX_DIM].astype(bf16)
    widx_ref[...] = kw * IDX_SCALE
    qb_ref[...] = mm(C_QB, WIDTH_B).astype(bf16)
    kb_ref[...] = mm(C_KB, WIDTH_B).astype(bf16)
    vb_ref[...] = mm(C_VB, WIDTH_B).astype(bf16)
    for c in range(0, D_MODEL, 512):
        ga_ref[:, c:c + 512] = mm(C_GA + c, 512).astype(bf16)
        gb_ref[:, c:c + 512] = mm(C_GB + c, 512).astype(bf16)


def _proj(x2, shift, scale, w_dsa, w_rest, g_kv, seq):
    n, d = x2.shape
    tm = min(ROW_TILE, seq)
    per_b = seq // tm
    row = lambda i: (i, 0)
    bat = lambda i: (i // per_b, 0, 0)
    const = lambda i: (0, 0)
    widths = (WIDTH_A, 2 * KV_RANK, N_IDX_HEADS * IDX_DIM, IDX_DIM, LANES, WIDTH_B, WIDTH_B, WIDTH_B, D_MODEL, D_MODEL)
    dtypes = (bf16, bf16, bf16, bf16, f32, bf16, bf16, bf16, bf16, bf16)
    return pl.pallas_call(
        _proj_kernel,
        grid=(n // tm,),
        in_specs=[pl.BlockSpec((tm, d), row),
                  pl.BlockSpec((1, 1, d), bat),
                  pl.BlockSpec((1, 1, d), bat),
                  pl.BlockSpec((d, C_QB), const),
                  pl.BlockSpec((d, N_COLS_PAD - C_QB), const),
                  pl.BlockSpec((1, KV_RANK), const)],
        out_specs=[pl.BlockSpec((tm, w), row) for w in widths],
        out_shape=[jax.ShapeDtypeStruct((n, w), dt) for w, dt in zip(widths, dtypes)],
        compiler_params=pltpu.CompilerParams(dimension_semantics=("parallel",), vmem_limit_bytes=VMEM_LIMIT),
        name="proj",
    )(x2, shift, scale, w_dsa, w_rest, g_kv)


def _dsa_kernel(qa_ref, qidx_ref, widx_ref, kidx_ref, ckv_ref, wuk_ref, wuv_ref, bias_ref, tri_ref,
                o_ref, key_ref, qlat_ref, wrep_ref, thr_ref, need_ref, ceq_ref, mb_ref, m_ref, acc_ref, *, topk):
    tq = sk = ATT_BLOCK
    i = pl.program_id(1)
    n_blocks = i + 1

    for h in range(N_HEADS_A):
        ql = _dot(qa_ref[:, h * HEAD_DIM:(h + 1) * HEAD_DIM], wuk_ref[h]) * (HEAD_DIM ** -0.5 * LOG2E)
        qlat_ref[h] = ql.astype(bf16)

    rowi = lax.broadcasted_iota(jnp.int32, (tq, sk), 0)
    coli = lax.broadcasted_iota(jnp.int32, (tq, sk), 1)
    w = widx_ref[:, IDX_DIM:IDX_DIM + N_IDX_HEADS]
    for h in range(N_IDX_HEADS):
        wrep_ref[h] = jnp.broadcast_to(w[:, h:h + 1], (tq, LANES))

    def score_pair(p, carry):
        for u in range(2):
            j = 2 * p + u
            kblk = kidx_ref[pl.ds(pl.multiple_of(jnp.minimum(j, i) * sk, sk), sk), :]
            s = None
            for h in range(N_IDX_HEADS):
                d = _dot_nt(qidx_ref[:, h * IDX_DIM:(h + 1) * IDX_DIM], kblk)
                t = jnp.maximum(d, 0.0) * jnp.concatenate([wrep_ref[h]] * (sk // LANES), axis=1)
                s = t if s is None else s + t
            s = jnp.where(s == 0.0, 0.0, s)
            s = jnp.where((j - i) * sk + coli <= rowi, s, -jnp.inf)
            bits = pltpu.bitcast(s, jnp.int32)
            key_ref[j] = bits ^ ((bits >> 31) & np.int32(0x7FFFFFFF))
        return carry

    lax.fori_loop(0, (n_blocks + 1) // 2, score_pair, 0)
    for u in range(COUNT_UNROLL - 1):
        key_ref[n_blocks + u] = jnp.full((tq, sk), INT_MIN, jnp.int32)

    halves = range(0, tq, BISECT_ROWS)

    def count_ge(cands):
        accs = []
        for r0, cand in zip(halves, cands):
            cb = jnp.broadcast_to(cand, (BISECT_ROWS, LANES))

            def body(p, acc, r0=r0, cb=cb):
                for u in range(COUNT_UNROLL):
                    for c in range(0, sk, LANES):
                        kk = key_ref[COUNT_UNROLL * p + u, r0:r0 + BISECT_ROWS, c:c + LANES]
                        acc = acc + jnp.where(kk >= cb, 1.0, 0.0)
                return acc

            trips = (n_blocks + COUNT_UNROLL - 1) // COUNT_UNROLL
            accs.append(lax.fori_loop(0, trips, body, jnp.zeros((BISECT_ROWS, LANES), f32)))
        return [jnp.sum(acc, axis=1, keepdims=True) for acc in accs]

    def bit_step(b, answers):
        cands = [ans + jnp.left_shift(jnp.int32(1), 31 - b) for ans in answers]
        return tuple(jnp.where(cnt >= float(topk), cand, ans)
                     for cnt, cand, ans in zip(count_ge(cands), cands, answers))

    thrs = lax.fori_loop(0, 32, bit_step, tuple(jnp.full((BISECT_ROWS, 1), INT_MIN, jnp.int32) for _ in halves))
    for r0, thr, n_gt in zip(halves, thrs, count_ge([thr + 1 for thr in thrs])):
        thr_ref[r0:r0 + BISECT_ROWS, :] = jnp.broadcast_to(thr, (BISECT_ROWS, LANES))
        need_ref[r0:r0 + BISECT_ROWS, :] = jnp.broadcast_to(float(topk) - n_gt, (BISECT_ROWS, LANES))

    m_ref[...] = jnp.full(m_ref.shape, M_INIT, f32)
    acc_ref[...] = jnp.zeros(acc_ref.shape, f32)
    ceq_ref[...] = jnp.zeros(ceq_ref.shape, f32)

    def key_rows(j):
        return pl.ds(pl.multiple_of(j * sk, sk), sk)

    def mask_block(j, slot, diag):
        thr, need, ceq = thr_ref[...], need_ref[...], ceq_ref[...]
        keys = [key_ref[j, :, c:c + LANES] for c in range(0, sk, LANES)]
        eqs = [k == thr for k in keys]
        eqf = jnp.concatenate([jnp.where(e, 1.0, 0.0) for e in eqs], axis=1).astype(bf16)
        pref = _dot(eqf, tri_ref[...])
        total = jnp.broadcast_to(pref[:, sk - 1:sk], (tq, LANES))
        for n, (k, e) in enumerate(zip(keys, eqs)):
            c = n * LANES
            sel = jnp.logical_or(k > thr, jnp.logical_and(e, pref[:, c:c + LANES] + ceq <= need))
            if diag:
                ri = lax.broadcasted_iota(jnp.int32, (tq, LANES), 0)
                ci = lax.broadcasted_iota(jnp.int32, (tq, LANES), 1)
                sel = jnp.logical_and(sel, ci + c <= ri)
            mb_ref[:, slot * sk + c:slot * sk + c + LANES] = jnp.where(sel, 0.0, NEG_MASK)
        ceq_ref[...] = ceq + total

    def attend(j0, n_blk, near_col):
        for b in range(n_blk):
            mask_block(j0 + b, b, near_col is not None and b == n_blk - 1)
        heads, blks = range(N_HEADS_A), range(n_blk)
        logits = []
        for h in heads:
            q = qlat_ref[h]
            parts = []
            for b in blks:
                lg = _dot_nt(q, ckv_ref[key_rows(j0 + b), :KV_RANK]) + mb_ref[:, b * sk:(b + 1) * sk]
                if near_col is not None:
                    lg = lg + bias_ref[h, :, near_col + b * sk:near_col + (b + 1) * sk]
                parts.append(lg)
            logits.append(parts)
        m_olds = [m_ref[h] for h in heads]
        m_news = []
        for h in heads:
            mx = functools.reduce(jnp.maximum, [p[:, c:c + LANES] for p in logits[h] for c in range(0, sk, LANES)])
            m_news.append(jnp.maximum(m_olds[h], jnp.max(mx, axis=1, keepdims=True)))
        probs = []
        for h in heads:
            m_wide = jnp.concatenate([m_news[h]] * (sk // LANES), axis=1)
            probs.append([jnp.exp2(logits[h][b] - m_wide).astype(bf16) for b in blks])
        for h in heads:
            pv = None
            for b in blks:
                d = _dot(probs[h][b], ckv_ref[key_rows(j0 + b), :])
                pv = d if pv is None else pv + d
            alpha = jnp.exp2(m_olds[h] - m_news[h])
            acc_ref[h] = jnp.concatenate([alpha, alpha], axis=1) * acc_ref[h] + pv
            m_ref[h] = m_news[h]

    lone = jnp.logical_and(i >= 2, i % 2 == 0)

    @pl.when(lone)
    def _():
        attend(0, 1, None)

    first = jnp.where(lone, 1, 0)

    def far_pair(p, carry):
        attend(first + 2 * p, 2, None)
        return carry

    lax.fori_loop(0, (i - 1 - first) // 2, far_pair, 0)

    @pl.when(i >= 1)
    def _():
        attend(i - 1, 2, 0)

    @pl.when(i == 0)
    def _():
        attend(0, 1, sk)

    outs = []
    for h in range(N_HEADS_A):
        acc = acc_ref[h]
        o_lat = acc[:, :KV_RANK] / acc[:, KV_RANK:]
        outs.append(_dot(o_lat.astype(bf16), wuv_ref[h]))
    o_ref[...] = jnp.concatenate(outs, axis=1).astype(bf16)


def _t5_bucket(n):
    max_exact = N_BUCKETS // 2
    nf = jnp.maximum(n, 1).astype(f32)
    large = max_exact + (jnp.log(nf / max_exact) / math.log(MAX_DISTANCE / max_exact)
                         * (N_BUCKETS - max_exact)).astype(jnp.int32)
    large = jnp.minimum(large, N_BUCKETS - 1)
    return jnp.where(n < max_exact, n, large)


def _dsa(qa, qidx, widx, kidx, ckv, wuk_t, wuv_t, rel_bias, batch, seq):
    n = qa.shape[0]
    tq = ATT_BLOCK
    nq = seq // tq
    topk = min(INDEX_TOPK, seq // 4)
    span = 3 * tq - 1
    by_dist = rel_bias[_t5_bucket(jnp.maximum(jnp.arange(span) - tq + 1, 0))].astype(f32).T
    skew = jnp.tile(by_dist, (1, tq + 1))[:, :tq * (span + 1)].reshape(N_HEADS_A, tq, span + 1)
    bias_near = skew[:, :, :2 * tq][:, :, ::-1]
    far_n = np.float32(tq + 1)
    assert 16 + int(np.log(far_n / 16) / math.log(MAX_DISTANCE / 16) * 16) >= N_BUCKETS - 1
    bias_near = (bias_near - rel_bias[N_BUCKETS - 1].astype(f32)[:, None, None]) * LOG2E
    tri = (jnp.arange(tq)[:, None] <= jnp.arange(tq)[None, :]).astype(bf16)

    qrow = lambda b, i: (b * nq + i, 0)
    full = lambda b, i: (b, 0)
    c3 = lambda b, i: (0, 0, 0)
    c2 = lambda b, i: (0, 0)
    kern = functools.partial(_dsa_kernel, topk=topk)
    return pl.pallas_call(
        kern,
        grid=(batch, nq),
        in_specs=[pl.BlockSpec((tq, WIDTH_A), qrow),
                  pl.BlockSpec((tq, N_IDX_HEADS * IDX_DIM), qrow),
                  pl.BlockSpec((tq, LANES), qrow),
                  pl.BlockSpec((seq, IDX_DIM), full),
                  pl.BlockSpec((seq, 2 * KV_RANK), full),
                  pl.BlockSpec((N_HEADS_A, HEAD_DIM, KV_RANK), c3),
                  pl.BlockSpec((N_HEADS_A, KV_RANK, HEAD_DIM), c3),
                  pl.BlockSpec((N_HEADS_A, tq, 2 * tq), c3),
                  pl.BlockSpec((tq, tq), c2)],
        out_specs=pl.BlockSpec((tq, WIDTH_A), qrow),
        out_shape=jax.ShapeDtypeStruct((n, WIDTH_A), bf16),
        scratch_shapes=[pltpu.VMEM((nq + COUNT_UNROLL - 1, tq, tq), jnp.int32),
                        pltpu.VMEM((N_HEADS_A, tq, KV_RANK), bf16),
                        pltpu.VMEM((N_IDX_HEADS, tq, LANES), f32),
                        pltpu.VMEM((tq, LANES), jnp.int32),
                        pltpu.VMEM((tq, LANES), f32),
                        pltpu.VMEM((tq, LANES), f32),
                        pltpu.VMEM((tq, 2 * tq), f32),
                        pltpu.VMEM((N_HEADS_A, tq, LANES), f32),
                        pltpu.VMEM((N_HEADS_A, tq, 2 * KV_RANK), f32)],
        compiler_params=pltpu.CompilerParams(dimension_semantics=("parallel", "arbitrary"),
                                             vmem_limit_bytes=VMEM_LIMIT),
        name="dsa",
    )(qa, qidx, widx, kidx, ckv, wuk_t, wuv_t, bias_near, tri)


def _sb_kernel(q_ref, k_ref, v_ref, tri_ref, o_ref, carry_ref, acc_ref):
    tq = sk = ATT_BLOCK
    i = pl.program_id(1)
    carry_ref[...] = jnp.zeros(carry_ref.shape, f32)
    acc_ref[...] = jnp.zeros(acc_ref.shape, f32)

    def step(j, diag):
        rows = pl.ds(pl.multiple_of(j * sk, sk), sk)
        if diag:
            strict = lax.broadcasted_iota(jnp.int32, (tq, sk), 1) < lax.broadcasted_iota(jnp.int32, (tq, sk), 0)
        heads = range(N_HEADS_B)
        lanes = [slice(h * HEAD_DIM, (h + 1) * HEAD_DIM) for h in heads]
        zs = [_dot_nt(q_ref[:, lanes[h]] * (HEAD_DIM ** -0.5), k_ref[rows, lanes[h]]) for h in heads]
        lfs = [-(jnp.maximum(z, 0.0) + jnp.log(1.0 + jnp.exp(-jnp.abs(z)))) for z in zs]
        if diag:
            lfs = [jnp.where(strict, lf, 0.0) for lf in lfs]
        splits = [_split_bf16(lf) for lf in lfs]
        laters = [_dot(hi, tri_ref[...]) + _dot(lo, tri_ref[...]) for hi, lo in splits]
        carries = [carry_ref[h] for h in heads]
        probs = [jnp.exp(zs[h] + lfs[h] + laters[h] + jnp.concatenate([carries[h]] * (sk // LANES), axis=1))
                 for h in heads]
        if diag:
            probs = [jnp.where(strict, a, 0.0) for a in probs]
        top = None
        for h in heads:
            acc_ref[h] += _dot(probs[h].astype(bf16), v_ref[rows, lanes[h]])
            carry = carries[h] + jnp.sum(lfs[h], axis=1, keepdims=True)
            carry_ref[h] = carry
            top = carry if top is None else jnp.maximum(top, carry)
        return jnp.max(top)

    def cond(st):
        j, top = st
        return jnp.logical_and(j >= 0, top > SB_SKIP)

    def body(st):
        j, _ = st
        return j - 1, step(j, False)

    lax.while_loop(cond, body, (i - 1, step(i, True)))
    o_ref[...] = jnp.concatenate([acc_ref[h] for h in range(N_HEADS_B)], axis=1).astype(bf16)


def _sb(qb, kb, vb, batch, seq):
    n = qb.shape[0]
    tq = ATT_BLOCK
    nq = seq // tq
    tri = (jnp.arange(tq)[:, None] > jnp.arange(tq)[None, :]).astype(bf16)
    qrow = lambda b, i: (b * nq + i, 0)
    full = lambda b, i: (b, 0)
    return pl.pallas_call(
        _sb_kernel,
        grid=(batch, nq),
        in_specs=[pl.BlockSpec((tq, WIDTH_B), qrow),
                  pl.BlockSpec((seq, WIDTH_B), full),
                  pl.BlockSpec((seq, WIDTH_B), full),
                  pl.BlockSpec((tq, tq), lambda b, i: (0, 0))],
        out_specs=pl.BlockSpec((tq, WIDTH_B), qrow),
        out_shape=jax.ShapeDtypeStruct((n, WIDTH_B), bf16),
        scratch_shapes=[pltpu.VMEM((N_HEADS_B, tq, LANES), f32),
                        pltpu.VMEM((N_HEADS_B, tq, HEAD_DIM), f32)],
        compiler_params=pltpu.CompilerParams(dimension_semantics=("parallel", "arbitrary"),
                                             vmem_limit_bytes=VMEM_LIMIT),
        name="sb",
    )(qb, kb, vb, tri)


def _merge_kernel(oa_ref, ob_ref, ga_ref, gb_ref, x_ref, gate_ref, shift_ref, scale_ref,
                  wa_ref, wb_ref, wo_ref, g_ref, b_ref, wr_hi_ref, wr_lo_ref, br_ref,
                  x1_ref, h2_ref, logit_ref):
    ya = _dot(oa_ref[...], wa_ref[...])
    yb = _dot(ob_ref[...], wb_ref[...])
    merged = jax.nn.sigmoid(ga_ref[...].astype(f32)) * ya + jax.nn.sigmoid(gb_ref[...].astype(f32)) * yb
    y = _dot(merged.astype(bf16), wo_ref[...])
    x1 = _ln(DN_ALPHA * x_ref[...] + gate_ref[0] * y) * g_ref[...] + b_ref[...]
    x1_ref[...] = x1
    h2 = _ln(x1) * (1.0 + scale_ref[0]) + shift_ref[0]
    hi, lo = _split_bf16(h2)
    h2_ref[...] = _pack_halves(hi.astype(f32))
    logit_ref[...] = (_dot(hi, wr_hi_ref[...]) + _dot(lo, wr_hi_ref[...]) + _dot(hi, wr_lo_ref[...])
                      + br_ref[...])


def _merge(oa, ob, ga, gb, x2, gate1, shift2, scale2, wa, wb, wo, g, b, wr_hi, wr_lo, br, seq):
    n, d = x2.shape
    tm = min(ROW_TILE, seq)
    per_b = seq // tm
    row = lambda i: (i, 0)
    bat = lambda i: (i // per_b, 0, 0)
    const = lambda i: (0, 0)
    return pl.pallas_call(
        _merge_kernel,
        grid=(n // tm,),
        in_specs=[pl.BlockSpec((tm, WIDTH_A), row), pl.BlockSpec((tm, WIDTH_B), row),
                  pl.BlockSpec((tm, d), row), pl.BlockSpec((tm, d), row), pl.BlockSpec((tm, d), row),
                  pl.BlockSpec((1, 1, d), bat), pl.BlockSpec((1, 1, d), bat), pl.BlockSpec((1, 1, d), bat),
                  pl.BlockSpec((WIDTH_A, d), const), pl.BlockSpec((WIDTH_B, d), const), pl.BlockSpec((d, d), const),
                  pl.BlockSpec((1, d), const), pl.BlockSpec((1, d), const),
                  pl.BlockSpec((d, LANES), const), pl.BlockSpec((d, LANES), const), pl.BlockSpec((1, LANES), const)],
        out_specs=[pl.BlockSpec((tm, d), row), pl.BlockSpec((tm, d // 2), row), pl.BlockSpec((tm, LANES), row)],
        out_shape=[jax.ShapeDtypeStruct((n, d), f32), jax.ShapeDtypeStruct((n, d // 2), jnp.uint32),
                   jax.ShapeDtypeStruct((n, LANES), f32)],
        compiler_params=pltpu.CompilerParams(dimension_semantics=("parallel",), vmem_limit_bytes=VMEM_LIMIT),
        name="merge",
    )(oa, ob, ga, gb, x2, gate1, shift2, scale2, wa, wb, wo, g, b, wr_hi, wr_lo, br)


def _expert_kernel(be_ref, nb_ref, x_ref, wgu_ref, bgu_ref, wdn_ref, bdn_ref, *rest, block0):
    y_ref, wgu_bf, wdn_bf = rest[-3:]
    i = pl.program_id(0)
    g = i + block0

    @pl.when(jnp.logical_or(i == 0, be_ref[g] != be_ref[jnp.maximum(g - 1, 0)]))
    def _():
        wgu_bf[...] = wgu_ref[0, 0].astype(bf16)
        wdn_bf[...] = wdn_ref[0, 0].astype(bf16)

    @pl.when(g < nb_ref[0])
    def _():
        gu = _dot(_unpack_halves(x_ref[...]).astype(bf16), wgu_bf[...]) + bgu_ref[0, 0]
        a = jnp.minimum(gu[:, :D_EXPERT], SWIGLU_LIMIT)
        u = jnp.clip(gu[:, D_EXPERT:], -SWIGLU_LIMIT, SWIGLU_LIMIT)
        act = (u + 1.0) * a * jax.nn.sigmoid(SWIGLU_ALPHA * a)
        y_ref[...] = _pack_halves(_dot(act.astype(bf16), wdn_bf[...]) + bdn_ref[0, 0])

    @pl.when(g >= nb_ref[0])
    def _():
        y_ref[...] = jnp.zeros(y_ref.shape, y_ref.dtype)


def _experts(h2, row_token, block_expert, n_used, layer, wgu, bgu, wdn, bdn):
    rows, d = row_token.shape[0], 2 * h2.shape[1]
    n_blocks = rows // MOE_ROWS
    bounds = [n_blocks * c // MOE_CHUNKS for c in range(MOE_CHUNKS + 1)]
    ys = None
    for b0, b1 in zip(bounds[:-1], bounds[1:]):
        xs = h2[row_token[b0 * MOE_ROWS:b1 * MOE_ROWS]]
        e_idx = lambda i, be, nb, b0=b0: (layer, be[i + b0], 0, 0)
        in_specs = [pl.BlockSpec((MOE_ROWS, d // 2), lambda i, be, nb: (i, 0)),
                    pl.BlockSpec((1, 1, d, 2 * D_EXPERT), e_idx),
                    pl.BlockSpec((1, 1, 1, 2 * D_EXPERT), e_idx),
                    pl.BlockSpec((1, 1, D_EXPERT, d), e_idx),
                    pl.BlockSpec((1, 1, 1, d), e_idx)]
        args = [block_expert, n_used, xs, wgu, bgu, wdn, bdn]
        aliases = {}
        if ys is not None:
            in_specs.append(pl.BlockSpec(memory_space=pl.ANY))
            aliases = {len(args): 0}
            args.append(ys)
        grid_spec = pltpu.PrefetchScalarGridSpec(
            num_scalar_prefetch=2,
            grid=(b1 - b0,),
            in_specs=in_specs,
            out_specs=pl.BlockSpec((MOE_ROWS, d // 2), lambda i, be, nb, b0=b0: (i + b0, 0)),
            scratch_shapes=[pltpu.VMEM((d, 2 * D_EXPERT), bf16), pltpu.VMEM((D_EXPERT, d), bf16)],
        )
        ys = pl.pallas_call(
            functools.partial(_expert_kernel, block0=b0),
            grid_spec=grid_spec,
            out_shape=jax.ShapeDtypeStruct((rows, d // 2), jnp.uint32),
            input_output_aliases=aliases,
            compiler_params=pltpu.CompilerParams(dimension_semantics=("arbitrary",), vmem_limit_bytes=VMEM_LIMIT),
            name="experts",
        )(*args)
    return ys


def _combine_kernel(x_ref, y_ref, gk_ref, gate_ref, g_ref, b_ref, *rest):
    o_ref = rest[-1]
    gk = gk_ref[...]
    y = _unpack_halves(y_ref[0]) * gk[:, 0:1]
    for k in range(1, TOP_K):
        y = y + _unpack_halves(y_ref[k]) * gk[:, k:k + 1]
    o_ref[...] = _ln(DN_ALPHA * x_ref[...] + gate_ref[0] * y) * g_ref[...] + b_ref[...]


def _combine(x1, ys, dest, gates, gate2, g, b, seq):
    n, d = x1.shape
    tm = min(ROW_TILE, seq)
    per_b = seq // tm
    tiles = n // tm
    chunks = min(COMBINE_CHUNKS, tiles)
    bounds = [tiles * c // chunks for c in range(chunks + 1)]
    const = lambda i: (0, 0)
    out = None
    for a0, a1 in zip(bounds[:-1], bounds[1:]):
        yk = ys[dest[a0 * tm:a1 * tm].T]
        row = lambda i, a0=a0: (i + a0, 0)
        in_specs = [pl.BlockSpec((tm, d), row),
                    pl.BlockSpec((TOP_K, tm, d // 2), lambda i: (0, i, 0)),
                    pl.BlockSpec((tm, TOP_K), row),
                    pl.BlockSpec((1, 1, d), lambda i, a0=a0: ((i + a0) // per_b, 0, 0)),
                    pl.BlockSpec((1, d), const), pl.BlockSpec((1, d), const)]
        args = [x1, yk, gates, gate2, g, b]
        aliases = {}
        if out is not None:
            in_specs.append(pl.BlockSpec(memory_space=pl.ANY))
            aliases = {len(args): 0}
            args.append(out)
        out = pl.pallas_call(
            _combine_kernel,
            grid=(a1 - a0,),
            in_specs=in_specs,
            out_specs=pl.BlockSpec((tm, d), row),
            out_shape=jax.ShapeDtypeStruct((n, d), f32),
            input_output_aliases=aliases,
            compiler_params=pltpu.CompilerParams(dimension_semantics=("parallel",), vmem_limit_bytes=VMEM_LIMIT),
            name="combine",
        )(*args)
    return out


def _route_kernel(logit_ref, tri_ref, idx_ref, gate_ref, rank_ref, count_ref, carry_ref):
    @pl.when(pl.program_id(0) == 0)
    def _():
        carry_ref[...] = jnp.zeros(carry_ref.shape, f32)

    tm = logit_ref.shape[0]
    lane = lax.broadcasted_iota(jnp.int32, (tm, LANES), 1).astype(f32)
    v = jnp.where(lane < N_EXPERTS, logit_ref[...], -jnp.inf)
    tops, picks = [], []
    for _ in range(TOP_K):
        top = jnp.max(v, axis=1, keepdims=True)
        pick = jnp.min(jnp.where(v == top, lane, float(LANES)), axis=1, keepdims=True)
        tops.append(top)
        picks.append(pick)
        v = jnp.where(lane == pick, -jnp.inf, v)
    exps = [jnp.exp(top - tops[0]) for top in tops]
    denom = functools.reduce(jnp.add, exps)
    hits = [lane == pick for pick in picks]
    onehot = functools.reduce(jnp.add, [jnp.where(hit, 1.0, 0.0) for hit in hits])
    before = _dot(tri_ref[...], onehot.astype(bf16)) + carry_ref[...]
    ranks = [jnp.sum(jnp.where(hit, before, 0.0), axis=1, keepdims=True) for hit in hits]

    def spread(cols):
        return functools.reduce(jnp.add, [jnp.where(lane == float(k), col, 0.0) for k, col in enumerate(cols)])

    idx_ref[...] = spread(picks).astype(jnp.int32)
    gate_ref[...] = spread([e / denom for e in exps])
    rank_ref[...] = spread(ranks).astype(jnp.int32)
    carry_ref[...] += jnp.sum(onehot, axis=0, keepdims=True)
    count_ref[...] = carry_ref[...]


def _route(logits):
    n = logits.shape[0]
    tm = ROW_TILE
    tri = (jnp.arange(tm)[:, None] > jnp.arange(tm)[None, :]).astype(bf16)
    row = lambda i: (i, 0)
    const = lambda i: (0, 0)
    idx, gates, rank, counts = pl.pallas_call(
        _route_kernel,
        grid=(n // tm,),
        in_specs=[pl.BlockSpec((tm, LANES), row), pl.BlockSpec((tm, tm), const)],
        out_specs=[pl.BlockSpec((tm, LANES), row), pl.BlockSpec((tm, LANES), row), pl.BlockSpec((tm, LANES), row),
                   pl.BlockSpec((1, LANES), const)],
        out_shape=[jax.ShapeDtypeStruct((n, LANES), jnp.int32), jax.ShapeDtypeStruct((n, LANES), f32),
                   jax.ShapeDtypeStruct((n, LANES), jnp.int32), jax.ShapeDtypeStruct((1, LANES), f32)],
        scratch_shapes=[pltpu.VMEM((1, LANES), f32)],
        compiler_params=pltpu.CompilerParams(dimension_semantics=("arbitrary",), vmem_limit_bytes=VMEM_LIMIT),
        name="route",
    )(logits, tri)
    top_idx, gates, rank = idx[:, :TOP_K], gates[:, :TOP_K], rank[:, :TOP_K]
    counts = counts[0, :N_EXPERTS].astype(jnp.int32)
    padded = (counts + MOE_ROWS - 1) // MOE_ROWS * MOE_ROWS
    pad_end = jnp.cumsum(padded)
    pad_start = pad_end - padded
    dest = pad_start[top_idx] + rank
    n_blocks = -(-(n * TOP_K + N_EXPERTS * (MOE_ROWS - 1)) // MOE_ROWS)
    block_start = jnp.arange(n_blocks, dtype=jnp.int32) * MOE_ROWS
    block_expert = jnp.minimum(jnp.sum((pad_end[None, :] <= block_start[:, None]).astype(jnp.int32), axis=1),
                               N_EXPERTS - 1)
    n_used = (pad_end[-1:] // MOE_ROWS).astype(jnp.int32)
    nk = n * TOP_K
    order = jnp.sort(top_idx.reshape(-1) * nk + jnp.arange(nk, dtype=jnp.int32))
    sorted_token = (order % nk) // TOP_K
    first = jnp.cumsum(counts) - counts
    offset = (block_start - pad_start[block_expert])[:, None] + jnp.arange(MOE_ROWS, dtype=jnp.int32)[None, :]
    source = jnp.clip(first[block_expert][:, None] + offset, 0, nk - 1)
    row_token = jnp.where(offset < counts[block_expert][:, None], sorted_token[source], 0).reshape(-1)
    return gates, dest, row_token, block_expert, n_used


def _split_w_in(w):
    split = sum(COL_SPLITS[:5])
    return jnp.pad(w[:, :split], ((0, 0), (0, IDX_PAD))).astype(bf16), w[:, split:].astype(bf16)


def kernel(x, c, rel_bias, w_ada, b_ada, w_in, g_kv, w_uk, w_uv, w_a_out, w_b_out, w_o, ln1_g, ln1_b,
           w_router, b_router, w_gu, b_gu, w_dn, b_dn, ln2_g, ln2_b):
    B, S, D = x.shape
    N = B * S
    assert D == D_MODEL and S % ATT_BLOCK == 0
    x2 = x.reshape(N, D)
    for l in range(DEPTH):
        mod = _ada(c, l, w_ada, b_ada)
        shift1, scale1, gate1, shift2, scale2, gate2 = [m[:, None, :] for m in jnp.split(mod, 6, axis=-1)]

        qa, ckv, qidx, kidx, widx, qb, kb, vb, ga, gb = _proj(
            x2, shift1, scale1, *_split_w_in(w_in[l]), g_kv[l][None, :], S)
        wuk_t = jnp.transpose(w_uk[l], (1, 2, 0)).astype(bf16)
        wuv_t = jnp.transpose(w_uv[l], (1, 0, 2)).astype(bf16)
        oa = _dsa(qa, qidx, widx, kidx, ckv, wuk_t, wuv_t, rel_bias, B, S)
        ob = _sb(qb, kb, vb, B, S)

        wr = jnp.pad(w_router[l], ((0, 0), (0, LANES - N_EXPERTS)))
        wr_hi, wr_lo = _split_bf16(wr)
        br = jnp.pad(b_router[l], (0, LANES - N_EXPERTS))[None, :]
        x1, h2, logits = _merge(oa, ob, ga, gb, x2, gate1, shift2, scale2,
                                w_a_out[l].astype(bf16), w_b_out[l].astype(bf16), w_o[l].astype(bf16),
                                ln1_g[l][None, :], ln1_b[l][None, :], wr_hi, wr_lo, br, S)

        gates, dest, row_token, block_expert, n_used = _route(logits)
        ys = _experts(h2, row_token, block_expert, n_used, l, w_gu, b_gu[:, :, None, :], w_dn, b_dn[:, :, None, :])
        x2 = _combine(x1, ys, dest, gates, gate2, ln2_g[l][None, :], ln2_b[l][None, :], S)
    return x2.reshape(B, S, D)
```

```python
import functools
import math

import numpy as np
import jax
import jax.numpy as jnp
from jax import lax
from jax.experimental import pallas as pl
from jax.experimental.pallas import tpu as pltpu

D_MODEL = 1024
HEAD_DIM = 64
N_HEADS_A = 8
WIDTH_A = N_HEADS_A * HEAD_DIM
KV_RANK = 128
N_IDX_HEADS = 4
IDX_DIM = 64
INDEX_TOPK = 256
N_HEADS_B = 8
WIDTH_B = N_HEADS_B * HEAD_DIM
N_BUCKETS = 32
MAX_DISTANCE = 128
N_EXPERTS = 32
TOP_K = 4
D_EXPERT = 1024
SWIGLU_LIMIT = 7.0
SWIGLU_ALPHA = 1.702
LN_EPS = 1e-5
RMS_EPS = 1e-6
DEPTH = 2
DN_ALPHA = (2 * DEPTH) ** 0.25
IDX_SCALE = (N_IDX_HEADS ** -0.5) * (IDX_DIM ** -0.5)
LOG2E = math.log2(math.e)

COL_SPLITS = (WIDTH_A, KV_RANK, N_IDX_HEADS * IDX_DIM, IDX_DIM, N_IDX_HEADS,
              WIDTH_B, WIDTH_B, WIDTH_B, D_MODEL, D_MODEL)
N_COLS = sum(COL_SPLITS)

LANES = 128
VMEM_LIMIT = 56 * 1024 * 1024

IDX_PAD = LANES - IDX_DIM - N_IDX_HEADS
C_QA = 0
C_CKV = C_QA + WIDTH_A
C_QIDX = C_CKV + KV_RANK
C_KW = C_QIDX + N_IDX_HEADS * IDX_DIM
C_QB = C_KW + LANES
C_KB = C_QB + WIDTH_B
C_VB = C_KB + WIDTH_B
C_GA = C_VB + WIDTH_B
C_GB = C_GA + D_MODEL
N_COLS_PAD = C_GB + D_MODEL
REST_SHIFT = sum(COL_SPLITS[:5]) % LANES
REST_BASE = sum(COL_SPLITS[:5]) - REST_SHIFT
REST_WIDTH = N_COLS - REST_BASE + IDX_PAD

ROW_TILE = 512
ATT_BLOCK = 256
BISECT_ROWS = 128
COUNT_UNROLL = 4
MOE_ROWS = 512
MOE_CHUNKS = 4
COMBINE_CHUNKS = 1
NEG_MASK = -1e30
M_INIT = -1e29
SB_SKIP = -110.0
INT_MIN = np.int32(-2 ** 31)

f32 = jnp.float32
bf16 = jnp.bfloat16


def _ln(x):
    mu = jnp.mean(x, axis=-1, keepdims=True)
    xc = x - mu
    var = jnp.mean(xc * xc, axis=-1, keepdims=True)
    return xc * lax.rsqrt(var + LN_EPS)


def _dot(a, b):
    return jnp.dot(a, b, preferred_element_type=f32)


def _dot_nt(a, b):
    return lax.dot_general(a, b, (((1,), (1,)), ((), ())), preferred_element_type=f32)


def _split_bf16(x):
    hi = x.astype(bf16)
    lo = (x - hi.astype(f32)).astype(bf16)
    return hi, lo


def _pack_halves(y):
    w = y.shape[1] // 2
    lo = pltpu.bitcast(y[:, :w].astype(bf16).astype(f32), jnp.uint32) >> 16
    hi = pltpu.bitcast(y[:, w:].astype(bf16).astype(f32), jnp.uint32) & jnp.uint32(0xFFFF0000)
    return hi | lo


def _unpack_halves(p):
    lo = pltpu.bitcast(p << 16, f32)
    hi = pltpu.bitcast(p & jnp.uint32(0xFFFF0000), f32)
    return jnp.concatenate([lo, hi], axis=1)


def _ada_kernel(c_ref, w_ref, b_ref, o_ref):
    c = c_ref[...]
    c_hi, c_lo = _split_bf16(c * jax.nn.sigmoid(c))
    w_hi, w_lo = _split_bf16(w_ref[0])
    o_ref[...] = _dot(c_hi, w_hi) + _dot(c_lo, w_hi) + _dot(c_hi, w_lo) + b_ref[0]


def _ada(c, layer, w_ada, b_ada):
    batch, d = c.shape
    cols = w_ada.shape[2]
    return pl.pallas_call(
        _ada_kernel,
        grid=(cols // d,),
        in_specs=[pl.BlockSpec((batch, d), lambda j: (0, 0)),
                  pl.BlockSpec((1, d, d), lambda j: (layer, 0, j)),
                  pl.BlockSpec((1, 1, d), lambda j: (layer, 0, j))],
        out_specs=pl.BlockSpec((batch, d), lambda j: (0, j)),
        out_shape=jax.ShapeDtypeStruct((batch, cols), f32),
        compiler_params=pltpu.CompilerParams(dimension_semantics=("parallel",), vmem_limit_bytes=VMEM_LIMIT),
        name="ada",
    )(c, w_ada, b_ada[:, None, :])


def _proj_kernel(x_ref, shift_ref, scale_ref, wa_ref, wb_ref, gkv_ref,
                 qa_ref, ckv_ref, qidx_ref, kidx_ref, widx_ref, qb_ref, kb_ref, vb_ref, ga_ref, gb_ref):
    h = _ln(x_ref[...]) * (1.0 + scale_ref[0]) + shift_ref[0]
    hb = h.astype(bf16)

    def mm(c0, width):
        return _dot(hb, wa_ref[:, c0:c0 + width])

    qa_ref[...] = mm(C_QA, WIDTH_A).astype(bf16)
    ckv = mm(C_CKV, KV_RANK)
    ckv = ckv * lax.rsqrt(jnp.mean(ckv * ckv, axis=-1, keepdims=True) + RMS_EPS) * gkv_ref[...]
    ckv_ref[...] = jnp.concatenate([ckv, jnp.ones_like(ckv)], axis=1).astype(bf16)
    qidx_ref[...] = mm(C_QIDX, N_IDX_HEADS * IDX_DIM).astype(bf16)
    kw = mm(C_KW, LANES)
    kidx_ref[...] = kw[:, :IDX_DIM].astype(bf16)
    widx_ref[...] = kw * IDX_SCALE
    pieces = [qb_ref, kb_ref, vb_ref, ga_ref.at[:, :512], ga_ref.at[:, 512:], gb_ref.at[:, :512], gb_ref.at[:, 512:]]
    width = wb_ref.shape[1]
    prev = _dot(hb, wb_ref[:, :512])
    for k, out in enumerate(pieces):
        nxt = _dot(hb, wb_ref[:, 512 * (k + 1):min(512 * (k + 2), width)])
        out[...] = jnp.concatenate([prev[:, REST_SHIFT:], nxt[:, :REST_SHIFT]], axis=1).astype(bf16)
        prev = nxt


def _proj(x2, shift, scale, w_dsa, w_rest, g_kv, seq):
    n, d = x2.shape
    tm = min(ROW_TILE, seq)
    per_b = seq // tm
    row = lambda i: (i, 0)
    bat = lambda i: (i // per_b, 0, 0)
    const = lambda i: (0, 0)
    widths = (WIDTH_A, 2 * KV_RANK, N_IDX_HEADS * IDX_DIM, IDX_DIM, LANES, WIDTH_B, WIDTH_B, WIDTH_B, D_MODEL, D_MODEL)
    dtypes = (bf16, bf16, bf16, bf16, f32, bf16, bf16, bf16, bf16, bf16)
    return pl.pallas_call(
        _proj_kernel,
        grid=(n // tm,),
        in_specs=[pl.BlockSpec((tm, d), row),
                  pl.BlockSpec((1, 1, d), bat),
                  pl.BlockSpec((1, 1, d), bat),
                  pl.BlockSpec((d, C_QB), const),
                  pl.BlockSpec((d, REST_WIDTH), const),
                  pl.BlockSpec((1, KV_RANK), const)],
        out_specs=[pl.BlockSpec((tm, w), row) for w in widths],
        out_shape=[jax.ShapeDtypeStruct((n, w), dt) for w, dt in zip(widths, dtypes)],
        compiler_params=pltpu.CompilerParams(dimension_semantics=("parallel",), vmem_limit_bytes=VMEM_LIMIT),
        name="proj",
    )(x2, shift, scale, w_dsa, w_rest, g_kv)


def _dsa_kernel(qa_ref, qidx_ref, widx_ref, kidx_ref, ckv_ref, wuk_ref, wuv_ref, bias_ref, tri_ref,
                o_ref, key_ref, qlat_ref, wrep_ref, thr_ref, need_ref, ceq_ref, mb_ref, m_ref, acc_ref, *, topk):
    tq = sk = ATT_BLOCK
    i = pl.program_id(1)
    n_blocks = i + 1

    for h in range(N_HEADS_A):
        ql = _dot(qa_ref[:, h * HEAD_DIM:(h + 1) * HEAD_DIM], wuk_ref[h]) * (HEAD_DIM ** -0.5 * LOG2E)
        qlat_ref[h] = ql.astype(bf16)

    rowi = lax.broadcasted_iota(jnp.int32, (tq, sk), 0)
    coli = lax.broadcasted_iota(jnp.int32, (tq, sk), 1)
    w = widx_ref[:, IDX_DIM:IDX_DIM + N_IDX_HEADS]
    for h in range(N_IDX_HEADS):
        wrep_ref[h] = jnp.broadcast_to(w[:, h:h + 1], (tq, LANES))

    def score_pair(p, carry):
        for u in range(2):
            j = 2 * p + u
            kblk = kidx_ref[pl.ds(pl.multiple_of(jnp.minimum(j, i) * sk, sk), sk), :]
            s = None
            for h in range(N_IDX_HEADS):
                d = _dot_nt(qidx_ref[:, h * IDX_DIM:(h + 1) * IDX_DIM], kblk)
                t = jnp.maximum(d, 0.0) * jnp.concatenate([wrep_ref[h]] * (sk // LANES), axis=1)
                s = t if s is None else s + t
            s = jnp.where(s == 0.0, 0.0, s)
            s = jnp.where((j - i) * sk + coli <= rowi, s, -jnp.inf)
            bits = pltpu.bitcast(s, jnp.int32)
            key_ref[j] = bits ^ ((bits >> 31) & np.int32(0x7FFFFFFF))
        return carry

    lax.fori_loop(0, (n_blocks + 1) // 2, score_pair, 0)
    for u in range(COUNT_UNROLL - 1):
        key_ref[n_blocks + u] = jnp.full((tq, sk), INT_MIN, jnp.int32)

    halves = range(0, tq, BISECT_ROWS)

    def count_ge(cands):
        accs = []
        for r0, cand in zip(halves, cands):
            cb = jnp.broadcast_to(cand, (BISECT_ROWS, LANES))

            def body(p, acc, r0=r0, cb=cb):
                for u in range(COUNT_UNROLL):
                    for c in range(0, sk, LANES):
                        kk = key_ref[COUNT_UNROLL * p + u, r0:r0 + BISECT_ROWS, c:c + LANES]
                        acc = acc + jnp.where(kk >= cb, 1.0, 0.0)
                return acc

            trips = (n_blocks + COUNT_UNROLL - 1) // COUNT_UNROLL
            accs.append(lax.fori_loop(0, trips, body, jnp.zeros((BISECT_ROWS, LANES), f32)))
        return [jnp.sum(acc, axis=1, keepdims=True) for acc in accs]

    def bit_step(b, answers):
        cands = [ans + jnp.left_shift(jnp.int32(1), 31 - b) for ans in answers]
        return tuple(jnp.where(cnt >= float(topk), cand, ans)
                     for cnt, cand, ans in zip(count_ge(cands), cands, answers))

    thrs = lax.fori_loop(0, 32, bit_step, tuple(jnp.full((BISECT_ROWS, 1), INT_MIN, jnp.int32) for _ in halves))
    for r0, thr, n_gt in zip(halves, thrs, count_ge([thr + 1 for thr in thrs])):
        thr_ref[r0:r0 + BISECT_ROWS, :] = jnp.broadcast_to(thr, (BISECT_ROWS, LANES))
        need_ref[r0:r0 + BISECT_ROWS, :] = jnp.broadcast_to(float(topk) - n_gt, (BISECT_ROWS, LANES))

    m_ref[...] = jnp.full(m_ref.shape, M_INIT, f32)
    acc_ref[...] = jnp.zeros(acc_ref.shape, f32)
    ceq_ref[...] = jnp.zeros(ceq_ref.shape, f32)

    def key_rows(j):
        return pl.ds(pl.multiple_of(j * sk, sk), sk)

    def mask_block(j, slot, diag):
        thr, need, ceq = thr_ref[...], need_ref[...], ceq_ref[...]
        keys = [key_ref[j, :, c:c + LANES] for c in range(0, sk, LANES)]
        eqs = [k == thr for k in keys]
        eqf = jnp.concatenate([jnp.where(e, 1.0, 0.0) for e in eqs], axis=1).astype(bf16)
        pref = _dot(eqf, tri_ref[...])
        total = jnp.broadcast_to(pref[:, sk - 1:sk], (tq, LANES))
        for n, (k, e) in enumerate(zip(keys, eqs)):
            c = n * LANES
            sel = jnp.logical_or(k > thr, jnp.logical_and(e, pref[:, c:c + LANES] + ceq <= need))
            if diag:
                ri = lax.broadcasted_iota(jnp.int32, (tq, LANES), 0)
                ci = lax.broadcasted_iota(jnp.int32, (tq, LANES), 1)
                sel = jnp.logical_and(sel, ci + c <= ri)
            mb_ref[:, slot * sk + c:slot * sk + c + LANES] = jnp.where(sel, 0.0, NEG_MASK)
        ceq_ref[...] = ceq + total

    def attend(j0, n_blk, near_col):
        for b in range(n_blk):
            mask_block(j0 + b, b, near_col is not None and b == n_blk - 1)
        heads, blks = range(N_HEADS_A), range(n_blk)
        logits = []
        for h in heads:
            q = qlat_ref[h]
            parts = []
            for b in blks:
                lg = _dot_nt(q, ckv_ref[key_rows(j0 + b), :KV_RANK]) + mb_ref[:, b * sk:(b + 1) * sk]
                if near_col is not None:
                    lg = lg + bias_ref[h, :, near_col + b * sk:near_col + (b + 1) * sk]
                parts.append(lg)
            logits.append(parts)
        m_olds = [m_ref[h] for h in heads]
        m_news = []
        for h in heads:
            mx = functools.reduce(jnp.maximum, [p[:, c:c + LANES] for p in logits[h] for c in range(0, sk, LANES)])
            m_news.append(jnp.maximum(m_olds[h], jnp.max(mx, axis=1, keepdims=True)))
        probs = []
        for h in heads:
            m_wide = jnp.concatenate([m_news[h]] * (sk // LANES), axis=1)
            probs.append([jnp.exp2(logits[h][b] - m_wide).astype(bf16) for b in blks])
        for h in heads:
            pv = None
            for b in blks:
                d = _dot(probs[h][b], ckv_ref[key_rows(j0 + b), :])
                pv = d if pv is None else pv + d
            alpha = jnp.exp2(m_olds[h] - m_news[h])
            acc_ref[h] = jnp.concatenate([alpha, alpha], axis=1) * acc_ref[h] + pv
            m_ref[h] = m_news[h]

    lone = jnp.logical_and(i >= 2, i % 2 == 0)

    @pl.when(lone)
    def _():
        attend(0, 1, None)

    first = jnp.where(lone, 1, 0)

    def far_pair(p, carry):
        attend(first + 2 * p, 2, None)
        return carry

    lax.fori_loop(0, (i - 1 - first) // 2, far_pair, 0)

    @pl.when(i >= 1)
    def _():
        attend(i - 1, 2, 0)

    @pl.when(i == 0)
    def _():
        attend(0, 1, sk)

    outs = []
    for h in range(N_HEADS_A):
        acc = acc_ref[h]
        o_lat = acc[:, :KV_RANK] / acc[:, KV_RANK:]
        outs.append(_dot(o_lat.astype(bf16), wuv_ref[h]))
    o_ref[...] = jnp.concatenate(outs, axis=1).astype(bf16)


def _t5_bucket(n):
    max_exact = N_BUCKETS // 2
    nf = jnp.maximum(n, 1).astype(f32)
    large = max_exact + (jnp.log(nf / max_exact) / math.log(MAX_DISTANCE / max_exact)
                         * (N_BUCKETS - max_exact)).astype(jnp.int32)
    large = jnp.minimum(large, N_BUCKETS - 1)
    return jnp.where(n < max_exact, n, large)


def _dsa(qa, qidx, widx, kidx, ckv, wuk_t, wuv_t, rel_bias, batch, seq):
    n = qa.shape[0]
    tq = ATT_BLOCK
    nq = seq // tq
    topk = min(INDEX_TOPK, seq // 4)
    span = 3 * tq - 1
    by_dist = rel_bias[_t5_bucket(jnp.maximum(jnp.arange(span) - tq + 1, 0))].astype(f32).T
    skew = jnp.tile(by_dist, (1, tq + 1))[:, :tq * (span + 1)].reshape(N_HEADS_A, tq, span + 1)
    bias_near = skew[:, :, :2 * tq][:, :, ::-1]
    far_n = np.float32(tq + 1)
    assert 16 + int(np.log(far_n / 16) / math.log(MAX_DISTANCE / 16) * 16) >= N_BUCKETS - 1
    bias_near = (bias_near - rel_bias[N_BUCKETS - 1].astype(f32)[:, None, None]) * LOG2E
    tri = (jnp.arange(tq)[:, None] <= jnp.arange(tq)[None, :]).astype(bf16)

    qrow = lambda b, i: (b * nq + i, 0)
    full = lambda b, i: (b, 0)
    c3 = lambda b, i: (0, 0, 0)
    c2 = lambda b, i: (0, 0)
    kern = functools.partial(_dsa_kernel, topk=topk)
    return pl.pallas_call(
        kern,
        grid=(batch, nq),
        in_specs=[pl.BlockSpec((tq, WIDTH_A), qrow),
                  pl.BlockSpec((tq, N_IDX_HEADS * IDX_DIM), qrow),
                  pl.BlockSpec((tq, LANES), qrow),
                  pl.BlockSpec((seq, IDX_DIM), full),
                  pl.BlockSpec((seq, 2 * KV_RANK), full),
                  pl.BlockSpec((N_HEADS_A, HEAD_DIM, KV_RANK), c3),
                  pl.BlockSpec((N_HEADS_A, KV_RANK, HEAD_DIM), c3),
                  pl.BlockSpec((N_HEADS_A, tq, 2 * tq), c3),
                  pl.BlockSpec((tq, tq), c2)],
        out_specs=pl.BlockSpec((tq, WIDTH_A), qrow),
        out_shape=jax.ShapeDtypeStruct((n, WIDTH_A), bf16),
        scratch_shapes=[pltpu.VMEM((nq + COUNT_UNROLL - 1, tq, tq), jnp.int32),
                        pltpu.VMEM((N_HEADS_A, tq, KV_RANK), bf16),
                        pltpu.VMEM((N_IDX_HEADS, tq, LANES), f32),
                        pltpu.VMEM((tq, LANES), jnp.int32),
                        pltpu.VMEM((tq, LANES), f32),
                        pltpu.VMEM((tq, LANES), f32),
                        pltpu.VMEM((tq, 2 * tq), f32),
                        pltpu.VMEM((N_HEADS_A, tq, LANES), f32),
                        pltpu.VMEM((N_HEADS_A, tq, 2 * KV_RANK), f32)],
        compiler_params=pltpu.CompilerParams(dimension_semantics=("parallel", "arbitrary"),
                                             vmem_limit_bytes=VMEM_LIMIT),
        name="dsa",
    )(qa, qidx, widx, kidx, ckv, wuk_t, wuv_t, bias_near, tri)


def _sb_kernel(q_ref, k_ref, v_ref, tri_ref, o_ref, carry_ref, acc_ref):
    tq = sk = ATT_BLOCK
    i = pl.program_id(1)
    carry_ref[...] = jnp.zeros(carry_ref.shape, f32)
    acc_ref[...] = jnp.zeros(acc_ref.shape, f32)

    def step(j, diag):
        rows = pl.ds(pl.multiple_of(j * sk, sk), sk)
        if diag:
            strict = lax.broadcasted_iota(jnp.int32, (tq, sk), 1) < lax.broadcasted_iota(jnp.int32, (tq, sk), 0)
        heads = range(N_HEADS_B)
        lanes = [slice(h * HEAD_DIM, (h + 1) * HEAD_DIM) for h in heads]
        zs = [_dot_nt(q_ref[:, lanes[h]] * (HEAD_DIM ** -0.5), k_ref[rows, lanes[h]]) for h in heads]
        lfs = [-(jnp.maximum(z, 0.0) + jnp.log(1.0 + jnp.exp(-jnp.abs(z)))) for z in zs]
        if diag:
            lfs = [jnp.where(strict, lf, 0.0) for lf in lfs]
        splits = [_split_bf16(lf) for lf in lfs]
        laters = [_dot(hi, tri_ref[...]) + _dot(lo, tri_ref[...]) for hi, lo in splits]
        carries = [carry_ref[h] for h in heads]
        probs = [jnp.exp(zs[h] + lfs[h] + laters[h] + jnp.concatenate([carries[h]] * (sk // LANES), axis=1))
                 for h in heads]
        if diag:
            probs = [jnp.where(strict, a, 0.0) for a in probs]
        top = None
        for h in heads:
            acc_ref[h] += _dot(probs[h].astype(bf16), v_ref[rows, lanes[h]])
            carry = carries[h] + jnp.sum(lfs[h], axis=1, keepdims=True)
            carry_ref[h] = carry
            top = carry if top is None else jnp.maximum(top, carry)
        return jnp.max(top)

    def cond(st):
        j, top = st
        return jnp.logical_and(j >= 0, top > SB_SKIP)

    def body(st):
        j, _ = st
        return j - 1, step(j, False)

    lax.while_loop(cond, body, (i - 1, step(i, True)))
    o_ref[...] = jnp.concatenate([acc_ref[h] for h in range(N_HEADS_B)], axis=1).astype(bf16)


def _sb(qb, kb, vb, batch, seq):
    n = qb.shape[0]
    tq = ATT_BLOCK
    nq = seq // tq
    tri = (jnp.arange(tq)[:, None] > jnp.arange(tq)[None, :]).astype(bf16)
    qrow = lambda b, i: (b * nq + i, 0)
    full = lambda b, i: (b, 0)
    return pl.pallas_call(
        _sb_kernel,
        grid=(batch, nq),
        in_specs=[pl.BlockSpec((tq, WIDTH_B), qrow),
                  pl.BlockSpec((seq, WIDTH_B), full),
                  pl.BlockSpec((seq, WIDTH_B), full),
                  pl.BlockSpec((tq, tq), lambda b, i: (0, 0))],
        out_specs=pl.BlockSpec((tq, WIDTH_B), qrow),
        out_shape=jax.ShapeDtypeStruct((n, WIDTH_B), bf16),
        scratch_shapes=[pltpu.VMEM((N_HEADS_B, tq, LANES), f32),
                        pltpu.VMEM((N_HEADS_B, tq, HEAD_DIM), f32)],
        compiler_params=pltpu.CompilerParams(dimension_semantics=("parallel", "arbitrary"),
                                             vmem_limit_bytes=VMEM_LIMIT),
        name="sb",
    )(qb, kb, vb, tri)


def _merge_kernel(oa_ref, ob_ref, ga_ref, gb_ref, x_ref, gate_ref, shift_ref, scale_ref,
                  wa_ref, wb_ref, wo_ref, g_ref, b_ref, wr_hi_ref, wr_lo_ref, br_ref,
                  x1_ref, h2_ref, logit_ref):
    ya = _dot(oa_ref[...], wa_ref[...])
    yb = _dot(ob_ref[...], wb_ref[...])
    merged = jax.nn.sigmoid(ga_ref[...].astype(f32)) * ya + jax.nn.sigmoid(gb_ref[...].astype(f32)) * yb
    y = _dot(merged.astype(bf16), wo_ref[...])
    x1 = _ln(DN_ALPHA * x_ref[...] + gate_ref[0] * y) * g_ref[...] + b_ref[...]
    x1_ref[...] = x1
    h2 = _ln(x1) * (1.0 + scale_ref[0]) + shift_ref[0]
    hi, lo = _split_bf16(h2)
    h2_ref[...] = _pack_halves(hi.astype(f32))
    logit_ref[...] = (_dot(hi, wr_hi_ref[...]) + _dot(lo, wr_hi_ref[...]) + _dot(hi, wr_lo_ref[...])
                      + br_ref[...])


def _merge(oa, ob, ga, gb, x2, gate1, shift2, scale2, wa, wb, wo, g, b, wr_hi, wr_lo, br, seq):
    n, d = x2.shape
    tm = min(ROW_TILE, seq)
    per_b = seq // tm
    row = lambda i: (i, 0)
    bat = lambda i: (i // per_b, 0, 0)
    const = lambda i: (0, 0)
    return pl.pallas_call(
        _merge_kernel,
        grid=(n // tm,),
        in_specs=[pl.BlockSpec((tm, WIDTH_A), row), pl.BlockSpec((tm, WIDTH_B), row),
                  pl.BlockSpec((tm, d), row), pl.BlockSpec((tm, d), row), pl.BlockSpec((tm, d), row),
                  pl.BlockSpec((1, 1, d), bat), pl.BlockSpec((1, 1, d), bat), pl.BlockSpec((1, 1, d), bat),
                  pl.BlockSpec((WIDTH_A, d), const), pl.BlockSpec((WIDTH_B, d), const), pl.BlockSpec((d, d), const),
                  pl.BlockSpec((1, d), const), pl.BlockSpec((1, d), const),
                  pl.BlockSpec((d, LANES), const), pl.BlockSpec((d, LANES), const), pl.BlockSpec((1, LANES), const)],
        out_specs=[pl.BlockSpec((tm, d), row), pl.BlockSpec((tm, d // 2), row), pl.BlockSpec((tm, LANES), row)],
        out_shape=[jax.ShapeDtypeStruct((n, d), f32), jax.ShapeDtypeStruct((n, d // 2), jnp.uint32),
                   jax.ShapeDtypeStruct((n, LANES), f32)],
        compiler_params=pltpu.CompilerParams(dimension_semantics=("parallel",), vmem_limit_bytes=VMEM_LIMIT),
        name="merge",
    )(oa, ob, ga, gb, x2, gate1, shift2, scale2, wa, wb, wo, g, b, wr_hi, wr_lo, br)


def _expert_kernel(be_ref, nb_ref, x_ref, wgu_ref, bgu_ref, wdn_ref, bdn_ref, *rest, block0):
    y_ref, wgu_bf, wdn_bf = rest[-3:]
    i = pl.program_id(0)
    g = i + block0

    @pl.when(jnp.logical_or(i == 0, be_ref[g] != be_ref[jnp.maximum(g - 1, 0)]))
    def _():
        wgu_bf[...] = wgu_ref[0, 0].astype(bf16)
        wdn_bf[...] = wdn_ref[0, 0].astype(bf16)

    @pl.when(g < nb_ref[0])
    def _():
        gu = _dot(_unpack_halves(x_ref[...]).astype(bf16), wgu_bf[...]) + bgu_ref[0, 0]
        a = jnp.minimum(gu[:, :D_EXPERT], SWIGLU_LIMIT)
        u = jnp.clip(gu[:, D_EXPERT:], -SWIGLU_LIMIT, SWIGLU_LIMIT)
        act = (u + 1.0) * a * jax.nn.sigmoid(SWIGLU_ALPHA * a)
        y_ref[...] = _pack_halves(_dot(act.astype(bf16), wdn_bf[...]) + bdn_ref[0, 0])

    @pl.when(g >= nb_ref[0])
    def _():
        y_ref[...] = jnp.zeros(y_ref.shape, y_ref.dtype)


def _experts(h2, row_token, block_expert, n_used, layer, wgu, bgu, wdn, bdn):
    rows, d = row_token.shape[0], 2 * h2.shape[1]
    n_blocks = rows // MOE_ROWS
    bounds = [n_blocks * c // MOE_CHUNKS for c in range(MOE_CHUNKS + 1)]
    ys = None
    for b0, b1 in zip(bounds[:-1], bounds[1:]):
        xs = h2[row_token[b0 * MOE_ROWS:b1 * MOE_ROWS]]
        e_idx = lambda i, be, nb, b0=b0: (layer, be[i + b0], 0, 0)
        in_specs = [pl.BlockSpec((MOE_ROWS, d // 2), lambda i, be, nb: (i, 0)),
                    pl.BlockSpec((1, 1, d, 2 * D_EXPERT), e_idx),
                    pl.BlockSpec((1, 1, 1, 2 * D_EXPERT), e_idx),
                    pl.BlockSpec((1, 1, D_EXPERT, d), e_idx),
                    pl.BlockSpec((1, 1, 1, d), e_idx)]
        args = [block_expert, n_used, xs, wgu, bgu, wdn, bdn]
        aliases = {}
        if ys is not None:
            in_specs.append(pl.BlockSpec(memory_space=pl.ANY))
            aliases = {len(args): 0}
            args.append(ys)
        grid_spec = pltpu.PrefetchScalarGridSpec(
            num_scalar_prefetch=2,
            grid=(b1 - b0,),
            in_specs=in_specs,
            out_specs=pl.BlockSpec((MOE_ROWS, d // 2), lambda i, be, nb, b0=b0: (i + b0, 0)),
            scratch_shapes=[pltpu.VMEM((d, 2 * D_EXPERT), bf16), pltpu.VMEM((D_EXPERT, d), bf16)],
        )
        ys = pl.pallas_call(
            functools.partial(_expert_kernel, block0=b0),
            grid_spec=grid_spec,
            out_shape=jax.ShapeDtypeStruct((rows, d // 2), jnp.uint32),
            input_output_aliases=aliases,
            compiler_params=pltpu.CompilerParams(dimension_semantics=("arbitrary",), vmem_limit_bytes=VMEM_LIMIT),
            name="experts",
        )(*args)
    return ys


def _combine_kernel(x_ref, y_ref, gk_ref, gate_ref, g_ref, b_ref, *rest):
    o_ref = rest[-1]
    gk = gk_ref[...]
    y = _unpack_halves(y_ref[0]) * gk[:, 0:1]
    for k in range(1, TOP_K):
        y = y + _unpack_halves(y_ref[k]) * gk[:, k:k + 1]
    o_ref[...] = _ln(DN_ALPHA * x_ref[...] + gate_ref[0] * y) * g_ref[...] + b_ref[...]


def _combine(x1, ys, dest, gates, gate2, g, b, seq):
    n, d = x1.shape
    tm = min(ROW_TILE, seq)
    per_b = seq // tm
    tiles = n // tm
    chunks = min(COMBINE_CHUNKS, tiles)
    bounds = [tiles * c // chunks for c in range(chunks + 1)]
    const = lambda i: (0, 0)
    out = None
    for a0, a1 in zip(bounds[:-1], bounds[1:]):
        yk = ys[dest[a0 * tm:a1 * tm].T]
        row = lambda i, a0=a0: (i + a0, 0)
        in_specs = [pl.BlockSpec((tm, d), row),
                    pl.BlockSpec((TOP_K, tm, d // 2), lambda i: (0, i, 0)),
                    pl.BlockSpec((tm, TOP_K), row),
                    pl.BlockSpec((1, 1, d), lambda i, a0=a0: ((i + a0) // per_b, 0, 0)),
                    pl.BlockSpec((1, d), const), pl.BlockSpec((1, d), const)]
        args = [x1, yk, gates, gate2, g, b]
        aliases = {}
        if out is not None:
            in_specs.append(pl.BlockSpec(memory_space=pl.ANY))
            aliases = {len(args): 0}
            args.append(out)
        out = pl.pallas_call(
            _combine_kernel,
            grid=(a1 - a0,),
            in_specs=in_specs,
            out_specs=pl.BlockSpec((tm, d), row),
            out_shape=jax.ShapeDtypeStruct((n, d), f32),
            input_output_aliases=aliases,
            compiler_params=pltpu.CompilerParams(dimension_semantics=("parallel",), vmem_limit_bytes=VMEM_LIMIT),
            name="combine",
        )(*args)
    return out


def _route_kernel(logit_ref, tri_ref, idx_ref, gate_ref, rank_ref, count_ref, carry_ref):
    @pl.when(pl.program_id(0) == 0)
    def _():
        carry_ref[...] = jnp.zeros(carry_ref.shape, f32)

    tm = logit_ref.shape[0]
    lane = lax.broadcasted_iota(jnp.int32, (tm, LANES), 1).astype(f32)
    v = jnp.where(lane < N_EXPERTS, logit_ref[...], -jnp.inf)
    tops, picks = [], []
    for _ in range(TOP_K):
        top = jnp.max(v, axis=1, keepdims=True)
        pick = jnp.min(jnp.where(v == top, lane, float(LANES)), axis=1, keepdims=True)
        tops.append(top)
        picks.append(pick)
        v = jnp.where(lane == pick, -jnp.inf, v)
    exps = [jnp.exp(top - tops[0]) for top in tops]
    denom = functools.reduce(jnp.add, exps)
    hits = [lane == pick for pick in picks]
    onehot = functools.reduce(jnp.add, [jnp.where(hit, 1.0, 0.0) for hit in hits])
    before = _dot(tri_ref[...], onehot.astype(bf16)) + carry_ref[...]
    ranks = [jnp.sum(jnp.where(hit, before, 0.0), axis=1, keepdims=True) for hit in hits]

    def spread(cols):
        return functools.reduce(jnp.add, [jnp.where(lane == float(k), col, 0.0) for k, col in enumerate(cols)])

    idx_ref[...] = spread(picks).astype(jnp.int32)
    gate_ref[...] = spread([e / denom for e in exps])
    rank_ref[...] = spread(ranks).astype(jnp.int32)
    carry_ref[...] += jnp.sum(onehot, axis=0, keepdims=True)
    count_ref[...] = carry_ref[...]


def _route(logits):
    n = logits.shape[0]
    tm = ROW_TILE
    tri = (jnp.arange(tm)[:, None] > jnp.arange(tm)[None, :]).astype(bf16)
    row = lambda i: (i, 0)
    const = lambda i: (0, 0)
    idx, gates, rank, counts = pl.pallas_call(
        _route_kernel,
        grid=(n // tm,),
        in_specs=[pl.BlockSpec((tm, LANES), row), pl.BlockSpec((tm, tm), const)],
        out_specs=[pl.BlockSpec((tm, LANES), row), pl.BlockSpec((tm, LANES), row), pl.BlockSpec((tm, LANES), row),
                   pl.BlockSpec((1, LANES), const)],
        out_shape=[jax.ShapeDtypeStruct((n, LANES), jnp.int32), jax.ShapeDtypeStruct((n, LANES), f32),
                   jax.ShapeDtypeStruct((n, LANES), jnp.int32), jax.ShapeDtypeStruct((1, LANES), f32)],
        scratch_shapes=[pltpu.VMEM((1, LANES), f32)],
        compiler_params=pltpu.CompilerParams(dimension_semantics=("arbitrary",), vmem_limit_bytes=VMEM_LIMIT),
        name="route",
    )(logits, tri)
    top_idx, gates, rank = idx[:, :TOP_K], gates[:, :TOP_K], rank[:, :TOP_K]
    counts = counts[0, :N_EXPERTS].astype(jnp.int32)
    padded = (counts + MOE_ROWS - 1) // MOE_ROWS * MOE_ROWS
    pad_end = jnp.cumsum(padded)
    pad_start = pad_end - padded
    dest = pad_start[top_idx] + rank
    n_blocks = -(-(n * TOP_K + N_EXPERTS * (MOE_ROWS - 1)) // MOE_ROWS)
    block_start = jnp.arange(n_blocks, dtype=jnp.int32) * MOE_ROWS
    block_expert = jnp.minimum(jnp.sum((pad_end[None, :] <= block_start[:, None]).astype(jnp.int32), axis=1),
                               N_EXPERTS - 1)
    n_used = (pad_end[-1:] // MOE_ROWS).astype(jnp.int32)
    nk = n * TOP_K
    order = jnp.sort(top_idx.reshape(-1) * nk + jnp.arange(nk, dtype=jnp.int32))
    sorted_token = (order % nk) // TOP_K
    first = jnp.cumsum(counts) - counts
    offset = (block_start - pad_start[block_expert])[:, None] + jnp.arange(MOE_ROWS, dtype=jnp.int32)[None, :]
    source = jnp.clip(first[block_expert][:, None] + offset, 0, nk - 1)
    row_token = jnp.where(offset < counts[block_expert][:, None], sorted_token[source], 0).reshape(-1)
    return gates, dest, row_token, block_expert, n_used


def _split_w_in(w):
    return w[:, :C_QB].astype(bf16), jnp.pad(w[:, REST_BASE:], ((0, 0), (0, IDX_PAD))).astype(bf16)


def kernel(x, c, rel_bias, w_ada, b_ada, w_in, g_kv, w_uk, w_uv, w_a_out, w_b_out, w_o, ln1_g, ln1_b,
           w_router, b_router, w_gu, b_gu, w_dn, b_dn, ln2_g, ln2_b):
    B, S, D = x.shape
    N = B * S
    assert D == D_MODEL and S % ATT_BLOCK == 0
    x2 = x.reshape(N, D)
    for l in range(DEPTH):
        mod = _ada(c, l, w_ada, b_ada)
        shift1, scale1, gate1, shift2, scale2, gate2 = [m[:, None, :] for m in jnp.split(mod, 6, axis=-1)]

        qa, ckv, qidx, kidx, widx, qb, kb, vb, ga, gb = _proj(
            x2, shift1, scale1, *_split_w_in(w_in[l]), g_kv[l][None, :], S)
        wuk_t = jnp.transpose(w_uk[l], (1, 2, 0)).astype(bf16)
        wuv_t = jnp.transpose(w_uv[l], (1, 0, 2)).astype(bf16)
        oa = _dsa(qa, qidx, widx, kidx, ckv, wuk_t, wuv_t, rel_bias, B, S)
        ob = _sb(qb, kb, vb, B, S)

        wr = jnp.pad(w_router[l], ((0, 0), (0, LANES - N_EXPERTS)))
        wr_hi, wr_lo = _split_bf16(wr)
        br = jnp.pad(b_router[l], (0, LANES - N_EXPERTS))[None, :]
        x1, h2, logits = _merge(oa, ob, ga, gb, x2, gate1, shift2, scale2,
                                w_a_out[l].astype(bf16), w_b_out[l].astype(bf16), w_o[l].astype(bf16),
                                ln1_g[l][None, :], ln1_b[l][None, :], wr_hi, wr_lo, br, S)

        gates, dest, row_token, block_expert, n_used = _route(logits)
        ys = _experts(h2, row_token, block_expert, n_used, l, w_gu, b_gu[:, :, None, :], w_dn, b_dn[:, :, None, :])
        x2 = _combine(x1, ys, dest, gates, gate2, ln2_g[l][None, :], ln2_b[l][None, :], S)
    return x2.reshape(B, S, D)
```

```python
import functools
import math

import numpy as np
import jax
import jax.numpy as jnp
from jax import lax
from jax.experimental import pallas as pl
from jax.experimental.pallas import tpu as pltpu

D_MODEL = 1024
HEAD_DIM = 64
N_HEADS_A = 8
WIDTH_A = N_HEADS_A * HEAD_DIM
KV_RANK = 128
N_IDX_HEADS = 4
IDX_DIM = 64
INDEX_TOPK = 256
N_HEADS_B = 8
WIDTH_B = N_HEADS_B * HEAD_DIM
N_BUCKETS = 32
MAX_DISTANCE = 128
N_EXPERTS = 32
TOP_K = 4
D_EXPERT = 1024
SWIGLU_LIMIT = 7.0
SWIGLU_ALPHA = 1.702
LN_EPS = 1e-5
RMS_EPS = 1e-6
DEPTH = 2
DN_ALPHA = (2 * DEPTH) ** 0.25
IDX_SCALE = (N_IDX_HEADS ** -0.5) * (IDX_DIM ** -0.5)
LOG2E = math.log2(math.e)

COL_SPLITS = (WIDTH_A, KV_RANK, N_IDX_HEADS * IDX_DIM, IDX_DIM, N_IDX_HEADS,
              WIDTH_B, WIDTH_B, WIDTH_B, D_MODEL, D_MODEL)
N_COLS = sum(COL_SPLITS)

LANES = 128
VMEM_LIMIT = 56 * 1024 * 1024

IDX_PAD = LANES - IDX_DIM - N_IDX_HEADS
C_QA = 0
C_CKV = C_QA + WIDTH_A
C_QIDX = C_CKV + KV_RANK
C_KW = C_QIDX + N_IDX_HEADS * IDX_DIM
C_QB = C_KW + LANES
C_KB = C_QB + WIDTH_B
C_VB = C_KB + WIDTH_B
C_GA = C_VB + WIDTH_B
C_GB = C_GA + D_MODEL
N_COLS_PAD = C_GB + D_MODEL
REST_SHIFT = sum(COL_SPLITS[:5]) % LANES
REST_BASE = sum(COL_SPLITS[:5]) - REST_SHIFT
REST_WIDTH = N_COLS - REST_BASE + IDX_PAD

ROW_TILE = 512
ATT_BLOCK = 256
BISECT_ROWS = 128
COUNT_UNROLL = 8
MOE_ROWS = 512
MOE_CHUNKS = 8
COMBINE_CHUNKS = 1
NEG_MASK = -1e30
M_INIT = -1e29
SB_SKIP = -110.0
INT_MIN = np.int32(-2 ** 31)

f32 = jnp.float32
bf16 = jnp.bfloat16


def _ln(x):
    mu = jnp.mean(x, axis=-1, keepdims=True)
    xc = x - mu
    var = jnp.mean(xc * xc, axis=-1, keepdims=True)
    return xc * lax.rsqrt(var + LN_EPS)


def _dot(a, b):
    return jnp.dot(a, b, preferred_element_type=f32)


def _dot_nt(a, b):
    return lax.dot_general(a, b, (((1,), (1,)), ((), ())), preferred_element_type=f32)


def _split_bf16(x):
    hi = x.astype(bf16)
    lo = (x - hi.astype(f32)).astype(bf16)
    return hi, lo


def _pack_halves(y):
    w = y.shape[1] // 2
    lo = pltpu.bitcast(y[:, :w].astype(bf16).astype(f32), jnp.uint32) >> 16
    hi = pltpu.bitcast(y[:, w:].astype(bf16).astype(f32), jnp.uint32) & jnp.uint32(0xFFFF0000)
    return hi | lo


def _unpack_halves(p):
    lo = pltpu.bitcast(p << 16, f32)
    hi = pltpu.bitcast(p & jnp.uint32(0xFFFF0000), f32)
    return jnp.concatenate([lo, hi], axis=1)


def _ada_kernel(c_ref, w_ref, b_ref, o_ref):
    c = c_ref[...]
    c_hi, c_lo = _split_bf16(c * jax.nn.sigmoid(c))
    w_hi, w_lo = _split_bf16(w_ref[0])
    o_ref[...] = _dot(c_hi, w_hi) + _dot(c_lo, w_hi) + _dot(c_hi, w_lo) + b_ref[0]


def _ada(c, layer, w_ada, b_ada):
    batch, d = c.shape
    cols = w_ada.shape[2]
    return pl.pallas_call(
        _ada_kernel,
        grid=(cols // d,),
        in_specs=[pl.BlockSpec((batch, d), lambda j: (0, 0)),
                  pl.BlockSpec((1, d, d), lambda j: (layer, 0, j)),
                  pl.BlockSpec((1, 1, d), lambda j: (layer, 0, j))],
        out_specs=pl.BlockSpec((batch, d), lambda j: (0, j)),
        out_shape=jax.ShapeDtypeStruct((batch, cols), f32),
        compiler_params=pltpu.CompilerParams(dimension_semantics=("parallel",), vmem_limit_bytes=VMEM_LIMIT),
        name="ada",
    )(c, w_ada, b_ada[:, None, :])


def _proj_kernel(x_ref, shift_ref, scale_ref, wa_ref, wb_ref, gkv_ref,
                 qa_ref, ckv_ref, qidx_ref, kidx_ref, widx_ref, qb_ref, kb_ref, vb_ref, ga_ref, gb_ref):
    h = _ln(x_ref[...]) * (1.0 + scale_ref[0]) + shift_ref[0]
    hb = h.astype(bf16)

    def mm(c0, width):
        return _dot(hb, wa_ref[:, c0:c0 + width])

    qa_ref[...] = mm(C_QA, WIDTH_A).astype(bf16)
    ckv = mm(C_CKV, KV_RANK)
    ckv = ckv * lax.rsqrt(jnp.mean(ckv * ckv, axis=-1, keepdims=True) + RMS_EPS) * gkv_ref[...]
    ckv_ref[...] = jnp.concatenate([ckv, jnp.ones_like(ckv)], axis=1).astype(bf16)
    qidx_ref[...] = mm(C_QIDX, N_IDX_HEADS * IDX_DIM).astype(bf16)
    kw = mm(C_KW, LANES)
    kidx_ref[...] = kw[:, :IDX_DIM].astype(bf16)
    widx_ref[...] = kw * IDX_SCALE
    pieces = [qb_ref, kb_ref, vb_ref, ga_ref.at[:, :512], ga_ref.at[:, 512:], gb_ref.at[:, :512], gb_ref.at[:, 512:]]
    width = wb_ref.shape[1]
    prev = _dot(hb, wb_ref[:, :512])
    for k, out in enumerate(pieces):
        nxt = _dot(hb, wb_ref[:, 512 * (k + 1):min(512 * (k + 2), width)])
        out[...] = jnp.concatenate([prev[:, REST_SHIFT:], nxt[:, :REST_SHIFT]], axis=1).astype(bf16)
        prev = nxt


def _proj(x2, shift, scale, w_dsa, w_rest, g_kv, seq):
    n, d = x2.shape
    tm = min(ROW_TILE, seq)
    per_b = seq // tm
    row = lambda i: (i, 0)
    bat = lambda i: (i // per_b, 0, 0)
    const = lambda i: (0, 0)
    widths = (WIDTH_A, 2 * KV_RANK, N_IDX_HEADS * IDX_DIM, IDX_DIM, LANES, WIDTH_B, WIDTH_B, WIDTH_B, D_MODEL, D_MODEL)
    dtypes = (bf16, bf16, bf16, bf16, f32, bf16, bf16, bf16, bf16, bf16)
    return pl.pallas_call(
        _proj_kernel,
        grid=(n // tm,),
        in_specs=[pl.BlockSpec((tm, d), row),
                  pl.BlockSpec((1, 1, d), bat),
                  pl.BlockSpec((1, 1, d), bat),
                  pl.BlockSpec((d, C_QB), const),
                  pl.BlockSpec((d, REST_WIDTH), const),
                  pl.BlockSpec((1, KV_RANK), const)],
        out_specs=[pl.BlockSpec((tm, w), row) for w in widths],
        out_shape=[jax.ShapeDtypeStruct((n, w), dt) for w, dt in zip(widths, dtypes)],
        compiler_params=pltpu.CompilerParams(dimension_semantics=("parallel",), vmem_limit_bytes=VMEM_LIMIT),
        name="proj",
    )(x2, shift, scale, w_dsa, w_rest, g_kv)


def _dsa_kernel(qa_ref, qidx_ref, widx_ref, kidx_ref, ckv_ref, wuk_ref, wuv_ref, bias_ref, tri_ref,
                o_ref, key_ref, qlat_ref, wrep_ref, thr_ref, need_ref, ceq_ref, mb_ref, m_ref, acc_ref, *, topk):
    tq = sk = ATT_BLOCK
    i = pl.program_id(1)
    n_blocks = i + 1

    for h in range(N_HEADS_A):
        ql = _dot(qa_ref[:, h * HEAD_DIM:(h + 1) * HEAD_DIM], wuk_ref[h]) * (HEAD_DIM ** -0.5 * LOG2E)
        qlat_ref[h] = ql.astype(bf16)

    rowi = lax.broadcasted_iota(jnp.int32, (tq, sk), 0)
    coli = lax.broadcasted_iota(jnp.int32, (tq, sk), 1)
    w = widx_ref[:, IDX_DIM:IDX_DIM + N_IDX_HEADS]
    for h in range(N_IDX_HEADS):
        wrep_ref[h] = jnp.broadcast_to(w[:, h:h + 1], (tq, LANES))

    def score_pair(p, carry):
        for u in range(2):
            j = 2 * p + u
            kblk = kidx_ref[pl.ds(pl.multiple_of(jnp.minimum(j, i) * sk, sk), sk), :]
            s = None
            for h in range(N_IDX_HEADS):
                d = _dot_nt(qidx_ref[:, h * IDX_DIM:(h + 1) * IDX_DIM], kblk)
                t = jnp.maximum(d, 0.0) * jnp.concatenate([wrep_ref[h]] * (sk // LANES), axis=1)
                s = t if s is None else s + t
            s = jnp.where(s == 0.0, 0.0, s)
            s = jnp.where((j - i) * sk + coli <= rowi, s, -jnp.inf)
            bits = pltpu.bitcast(s, jnp.int32)
            key_ref[j] = bits ^ ((bits >> 31) & np.int32(0x7FFFFFFF))
        return carry

    lax.fori_loop(0, (n_blocks + 1) // 2, score_pair, 0)
    for u in range(COUNT_UNROLL - 1):
        key_ref[n_blocks + u] = jnp.full((tq, sk), INT_MIN, jnp.int32)

    halves = range(0, tq, BISECT_ROWS)

    def count_ge(cands):
        accs = []
        for r0, cand in zip(halves, cands):
            cb = jnp.broadcast_to(cand, (BISECT_ROWS, LANES))

            def body(p, acc, r0=r0, cb=cb):
                for u in range(COUNT_UNROLL):
                    for c in range(0, sk, LANES):
                        kk = key_ref[COUNT_UNROLL * p + u, r0:r0 + BISECT_ROWS, c:c + LANES]
                        acc = acc + jnp.where(kk >= cb, 1.0, 0.0)
                return acc

            trips = (n_blocks + COUNT_UNROLL - 1) // COUNT_UNROLL
            accs.append(lax.fori_loop(0, trips, body, jnp.zeros((BISECT_ROWS, LANES), f32)))
        return [jnp.sum(acc, axis=1, keepdims=True) for acc in accs]

    def bit_step(b, answers):
        cands = [ans + jnp.left_shift(jnp.int32(1), 31 - b) for ans in answers]
        return tuple(jnp.where(cnt >= float(topk), cand, ans)
                     for cnt, cand, ans in zip(count_ge(cands), cands, answers))

    thrs = lax.fori_loop(0, 32, bit_step, tuple(jnp.full((BISECT_ROWS, 1), INT_MIN, jnp.int32) for _ in halves))
    for r0, thr, n_gt in zip(halves, thrs, count_ge([thr + 1 for thr in thrs])):
        thr_ref[r0:r0 + BISECT_ROWS, :] = jnp.broadcast_to(thr, (BISECT_ROWS, LANES))
        need_ref[r0:r0 + BISECT_ROWS, :] = jnp.broadcast_to(float(topk) - n_gt, (BISECT_ROWS, LANES))

    m_ref[...] = jnp.full(m_ref.shape, M_INIT, f32)
    acc_ref[...] = jnp.zeros(acc_ref.shape, f32)
    ceq_ref[...] = jnp.zeros(ceq_ref.shape, f32)

    def key_rows(j):
        return pl.ds(pl.multiple_of(j * sk, sk), sk)

    def mask_block(j, slot, diag):
        thr, need, ceq = thr_ref[...], need_ref[...], ceq_ref[...]
        keys = [key_ref[j, :, c:c + LANES] for c in range(0, sk, LANES)]
        eqs = [k == thr for k in keys]
        eqf = jnp.concatenate([jnp.where(e, 1.0, 0.0) for e in eqs], axis=1).astype(bf16)
        pref = _dot(eqf, tri_ref[...])
        total = jnp.broadcast_to(pref[:, sk - 1:sk], (tq, LANES))
        for n, (k, e) in enumerate(zip(keys, eqs)):
            c = n * LANES
            sel = jnp.logical_or(k > thr, jnp.logical_and(e, pref[:, c:c + LANES] + ceq <= need))
            if diag:
                ri = lax.broadcasted_iota(jnp.int32, (tq, LANES), 0)
                ci = lax.broadcasted_iota(jnp.int32, (tq, LANES), 1)
                sel = jnp.logical_and(sel, ci + c <= ri)
            mb_ref[:, slot * sk + c:slot * sk + c + LANES] = jnp.where(sel, 0.0, NEG_MASK)
        ceq_ref[...] = ceq + total

    def attend(j0, n_blk, near_col):
        for b in range(n_blk):
            mask_block(j0 + b, b, near_col is not None and b == n_blk - 1)
        heads, blks = range(N_HEADS_A), range(n_blk)
        logits = []
        for h in heads:
            q = qlat_ref[h]
            parts = []
            for b in blks:
                lg = _dot_nt(q, ckv_ref[key_rows(j0 + b), :KV_RANK]) + mb_ref[:, b * sk:(b + 1) * sk]
                if near_col is not None:
                    lg = lg + bias_ref[h, :, near_col + b * sk:near_col + (b + 1) * sk]
                parts.append(lg)
            logits.append(parts)
        m_olds = [m_ref[h] for h in heads]
        m_news = []
        for h in heads:
            mx = functools.reduce(jnp.maximum, [p[:, c:c + LANES] for p in logits[h] for c in range(0, sk, LANES)])
            m_news.append(jnp.maximum(m_olds[h], jnp.max(mx, axis=1, keepdims=True)))
        probs = []
        for h in heads:
            m_wide = jnp.concatenate([m_news[h]] * (sk // LANES), axis=1)
            probs.append([jnp.exp2(logits[h][b] - m_wide).astype(bf16) for b in blks])
        for h in heads:
            pv = None
            for b in blks:
                d = _dot(probs[h][b], ckv_ref[key_rows(j0 + b), :])
                pv = d if pv is None else pv + d
            alpha = jnp.exp2(m_olds[h] - m_news[h])
            acc_ref[h] = jnp.concatenate([alpha, alpha], axis=1) * acc_ref[h] + pv
            m_ref[h] = m_news[h]

    lone = jnp.logical_and(i >= 2, i % 2 == 0)

    @pl.when(lone)
    def _():
        attend(0, 1, None)

    first = jnp.where(lone, 1, 0)

    def far_pair(p, carry):
        attend(first + 2 * p, 2, None)
        return carry

    lax.fori_loop(0, (i - 1 - first) // 2, far_pair, 0)

    @pl.when(i >= 1)
    def _():
        attend(i - 1, 2, 0)

    @pl.when(i == 0)
    def _():
        attend(0, 1, sk)

    outs = []
    for h in range(N_HEADS_A):
        acc = acc_ref[h]
        o_lat = acc[:, :KV_RANK] / acc[:, KV_RANK:]
        outs.append(_dot(o_lat.astype(bf16), wuv_ref[h]))
    o_ref[...] = jnp.concatenate(outs, axis=1).astype(bf16)


def _t5_bucket(n):
    max_exact = N_BUCKETS // 2
    nf = jnp.maximum(n, 1).astype(f32)
    large = max_exact + (jnp.log(nf / max_exact) / math.log(MAX_DISTANCE / max_exact)
                         * (N_BUCKETS - max_exact)).astype(jnp.int32)
    large = jnp.minimum(large, N_BUCKETS - 1)
    return jnp.where(n < max_exact, n, large)


def _dsa(qa, qidx, widx, kidx, ckv, wuk_t, wuv_t, rel_bias, batch, seq):
    n = qa.shape[0]
    tq = ATT_BLOCK
    nq = seq // tq
    topk = min(INDEX_TOPK, seq // 4)
    span = 3 * tq - 1
    by_dist = rel_bias[_t5_bucket(jnp.maximum(jnp.arange(span) - tq + 1, 0))].astype(f32).T
    skew = jnp.tile(by_dist, (1, tq + 1))[:, :tq * (span + 1)].reshape(N_HEADS_A, tq, span + 1)
    bias_near = skew[:, :, :2 * tq][:, :, ::-1]
    far_n = np.float32(tq + 1)
    assert 16 + int(np.log(far_n / 16) / math.log(MAX_DISTANCE / 16) * 16) >= N_BUCKETS - 1
    bias_near = (bias_near - rel_bias[N_BUCKETS - 1].astype(f32)[:, None, None]) * LOG2E
    tri = (jnp.arange(tq)[:, None] <= jnp.arange(tq)[None, :]).astype(bf16)

    qrow = lambda b, i: (b * nq + i, 0)
    full = lambda b, i: (b, 0)
    c3 = lambda b, i: (0, 0, 0)
    c2 = lambda b, i: (0, 0)
    kern = functools.partial(_dsa_kernel, topk=topk)
    return pl.pallas_call(
        kern,
        grid=(batch, nq),
        in_specs=[pl.BlockSpec((tq, WIDTH_A), qrow),
                  pl.BlockSpec((tq, N_IDX_HEADS * IDX_DIM), qrow),
                  pl.BlockSpec((tq, LANES), qrow),
                  pl.BlockSpec((seq, IDX_DIM), full),
                  pl.BlockSpec((seq, 2 * KV_RANK), full),
                  pl.BlockSpec((N_HEADS_A, HEAD_DIM, KV_RANK), c3),
                  pl.BlockSpec((N_HEADS_A, KV_RANK, HEAD_DIM), c3),
                  pl.BlockSpec((N_HEADS_A, tq, 2 * tq), c3),
                  pl.BlockSpec((tq, tq), c2)],
        out_specs=pl.BlockSpec((tq, WIDTH_A), qrow),
        out_shape=jax.ShapeDtypeStruct((n, WIDTH_A), bf16),
        scratch_shapes=[pltpu.VMEM((nq + COUNT_UNROLL - 1, tq, tq), jnp.int32),
                        pltpu.VMEM((N_HEADS_A, tq, KV_RANK), bf16),
                        pltpu.VMEM((N_IDX_HEADS, tq, LANES), f32),
                        pltpu.VMEM((tq, LANES), jnp.int32),
                        pltpu.VMEM((tq, LANES), f32),
                        pltpu.VMEM((tq, LANES), f32),
                        pltpu.VMEM((tq, 2 * tq), f32),
                        pltpu.VMEM((N_HEADS_A, tq, LANES), f32),
                        pltpu.VMEM((N_HEADS_A, tq, 2 * KV_RANK), f32)],
        compiler_params=pltpu.CompilerParams(dimension_semantics=("parallel", "arbitrary"),
                                             vmem_limit_bytes=VMEM_LIMIT),
        name="dsa",
    )(qa, qidx, widx, kidx, ckv, wuk_t, wuv_t, bias_near, tri)


def _sb_kernel(q_ref, k_ref, v_ref, tri_ref, o_ref, carry_ref, acc_ref):
    tq = sk = ATT_BLOCK
    i = pl.program_id(1)
    carry_ref[...] = jnp.zeros(carry_ref.shape, f32)
    acc_ref[...] = jnp.zeros(acc_ref.shape, f32)

    def step(j, diag):
        rows = pl.ds(pl.multiple_of(j * sk, sk), sk)
        if diag:
            strict = lax.broadcasted_iota(jnp.int32, (tq, sk), 1) < lax.broadcasted_iota(jnp.int32, (tq, sk), 0)
        heads = range(N_HEADS_B)
        lanes = [slice(h * HEAD_DIM, (h + 1) * HEAD_DIM) for h in heads]
        zs = [_dot_nt(q_ref[:, lanes[h]] * (HEAD_DIM ** -0.5), k_ref[rows, lanes[h]]) for h in heads]
        lfs = [-(jnp.maximum(z, 0.0) + jnp.log(1.0 + jnp.exp(-jnp.abs(z)))) for z in zs]
        if diag:
            lfs = [jnp.where(strict, lf, 0.0) for lf in lfs]
        splits = [_split_bf16(lf) for lf in lfs]
        laters = [_dot(hi, tri_ref[...]) + _dot(lo, tri_ref[...]) for hi, lo in splits]
        carries = [carry_ref[h] for h in heads]
        probs = [jnp.exp(zs[h] + lfs[h] + laters[h] + jnp.concatenate([carries[h]] * (sk // LANES), axis=1))
                 for h in heads]
        if diag:
            probs = [jnp.where(strict, a, 0.0) for a in probs]
        top = None
        for h in heads:
            acc_ref[h] += _dot(probs[h].astype(bf16), v_ref[rows, lanes[h]])
            carry = carries[h] + jnp.sum(lfs[h], axis=1, keepdims=True)
            carry_ref[h] = carry
            top = carry if top is None else jnp.maximum(top, carry)
        return jnp.max(top)

    def cond(st):
        j, top = st
        return jnp.logical_and(j >= 0, top > SB_SKIP)

    def body(st):
        j, _ = st
        return j - 1, step(j, False)

    lax.while_loop(cond, body, (i - 1, step(i, True)))
    o_ref[...] = jnp.concatenate([acc_ref[h] for h in range(N_HEADS_B)], axis=1).astype(bf16)


def _sb(qb, kb, vb, batch, seq):
    n = qb.shape[0]
    tq = ATT_BLOCK
    nq = seq // tq
    tri = (jnp.arange(tq)[:, None] > jnp.arange(tq)[None, :]).astype(bf16)
    qrow = lambda b, i: (b * nq + i, 0)
    full = lambda b, i: (b, 0)
    return pl.pallas_call(
        _sb_kernel,
        grid=(batch, nq),
        in_specs=[pl.BlockSpec((tq, WIDTH_B), qrow),
                  pl.BlockSpec((seq, WIDTH_B), full),
                  pl.BlockSpec((seq, WIDTH_B), full),
                  pl.BlockSpec((tq, tq), lambda b, i: (0, 0))],
        out_specs=pl.BlockSpec((tq, WIDTH_B), qrow),
        out_shape=jax.ShapeDtypeStruct((n, WIDTH_B), bf16),
        scratch_shapes=[pltpu.VMEM((N_HEADS_B, tq, LANES), f32),
                        pltpu.VMEM((N_HEADS_B, tq, HEAD_DIM), f32)],
        compiler_params=pltpu.CompilerParams(dimension_semantics=("parallel", "arbitrary"),
                                             vmem_limit_bytes=VMEM_LIMIT),
        name="sb",
    )(qb, kb, vb, tri)


def _merge_kernel(oa_ref, ob_ref, ga_ref, gb_ref, x_ref, gate_ref, shift_ref, scale_ref,
                  wa_ref, wb_ref, wo_ref, g_ref, b_ref, wr_hi_ref, wr_lo_ref, br_ref,
                  x1_ref, h2_ref, logit_ref):
    ya = _dot(oa_ref[...], wa_ref[...])
    yb = _dot(ob_ref[...], wb_ref[...])
    merged = jax.nn.sigmoid(ga_ref[...].astype(f32)) * ya + jax.nn.sigmoid(gb_ref[...].astype(f32)) * yb
    y = _dot(merged.astype(bf16), wo_ref[...])
    x1 = _ln(DN_ALPHA * x_ref[...] + gate_ref[0] * y) * g_ref[...] + b_ref[...]
    x1_ref[...] = x1
    h2 = _ln(x1) * (1.0 + scale_ref[0]) + shift_ref[0]
    hi, lo = _split_bf16(h2)
    h2_ref[...] = _pack_halves(hi.astype(f32))
    logit_ref[...] = (_dot(hi, wr_hi_ref[...]) + _dot(lo, wr_hi_ref[...]) + _dot(hi, wr_lo_ref[...])
                      + br_ref[...])


def _merge(oa, ob, ga, gb, x2, gate1, shift2, scale2, wa, wb, wo, g, b, wr_hi, wr_lo, br, seq):
    n, d = x2.shape
    tm = min(ROW_TILE, seq)
    per_b = seq // tm
    row = lambda i: (i, 0)
    bat = lambda i: (i // per_b, 0, 0)
    const = lambda i: (0, 0)
    return pl.pallas_call(
        _merge_kernel,
        grid=(n // tm,),
        in_specs=[pl.BlockSpec((tm, WIDTH_A), row), pl.BlockSpec((tm, WIDTH_B), row),
                  pl.BlockSpec((tm, d), row), pl.BlockSpec((tm, d), row), pl.BlockSpec((tm, d), row),
                  pl.BlockSpec((1, 1, d), bat), pl.BlockSpec((1, 1, d), bat), pl.BlockSpec((1, 1, d), bat),
                  pl.BlockSpec((WIDTH_A, d), const), pl.BlockSpec((WIDTH_B, d), const), pl.BlockSpec((d, d), const),
                  pl.BlockSpec((1, d), const), pl.BlockSpec((1, d), const),
                  pl.BlockSpec((d, LANES), const), pl.BlockSpec((d, LANES), const), pl.BlockSpec((1, LANES), const)],
        out_specs=[pl.BlockSpec((tm, d), row), pl.BlockSpec((tm, d // 2), row), pl.BlockSpec((tm, LANES), row)],
        out_shape=[jax.ShapeDtypeStruct((n, d), f32), jax.ShapeDtypeStruct((n, d // 2), jnp.uint32),
                   jax.ShapeDtypeStruct((n, LANES), f32)],
        compiler_params=pltpu.CompilerParams(dimension_semantics=("parallel",), vmem_limit_bytes=VMEM_LIMIT),
        name="merge",
    )(oa, ob, ga, gb, x2, gate1, shift2, scale2, wa, wb, wo, g, b, wr_hi, wr_lo, br)


def _expert_kernel(be_ref, nb_ref, x_ref, wgu_ref, bgu_ref, wdn_ref, bdn_ref, *rest, block0):
    y_ref, wgu_bf, wdn_bf = rest[-3:]
    i = pl.program_id(0)
    g = i + block0

    @pl.when(jnp.logical_or(i == 0, be_ref[g] != be_ref[jnp.maximum(g - 1, 0)]))
    def _():
        wgu_bf[...] = wgu_ref[0, 0].astype(bf16)
        wdn_bf[...] = wdn_ref[0, 0].astype(bf16)

    @pl.when(g < nb_ref[0])
    def _():
        gu = _dot(_unpack_halves(x_ref[...]).astype(bf16), wgu_bf[...]) + bgu_ref[0, 0]
        a = jnp.minimum(gu[:, :D_EXPERT], SWIGLU_LIMIT)
        u = jnp.clip(gu[:, D_EXPERT:], -SWIGLU_LIMIT, SWIGLU_LIMIT)
        act = (u + 1.0) * a * jax.nn.sigmoid(SWIGLU_ALPHA * a)
        y_ref[...] = _pack_halves(_dot(act.astype(bf16), wdn_bf[...]) + bdn_ref[0, 0])

    @pl.when(g >= nb_ref[0])
    def _():
        y_ref[...] = jnp.zeros(y_ref.shape, y_ref.dtype)


def _experts(h2, row_token, block_expert, n_used, layer, wgu, bgu, wdn, bdn):
    rows, d = row_token.shape[0], 2 * h2.shape[1]
    n_blocks = rows // MOE_ROWS
    bounds = [n_blocks * c // MOE_CHUNKS for c in range(MOE_CHUNKS + 1)]
    ys = None
    for b0, b1 in zip(bounds[:-1], bounds[1:]):
        xs = h2[row_token[b0 * MOE_ROWS:b1 * MOE_ROWS]]
        e_idx = lambda i, be, nb, b0=b0: (layer, be[i + b0], 0, 0)
        in_specs = [pl.BlockSpec((MOE_ROWS, d // 2), lambda i, be, nb: (i, 0)),
                    pl.BlockSpec((1, 1, d, 2 * D_EXPERT), e_idx),
                    pl.BlockSpec((1, 1, 1, 2 * D_EXPERT), e_idx),
                    pl.BlockSpec((1, 1, D_EXPERT, d), e_idx),
                    pl.BlockSpec((1, 1, 1, d), e_idx)]
        args = [block_expert, n_used, xs, wgu, bgu, wdn, bdn]
        aliases = {}
        if ys is not None:
            in_specs.append(pl.BlockSpec(memory_space=pl.ANY))
            aliases = {len(args): 0}
            args.append(ys)
        grid_spec = pltpu.PrefetchScalarGridSpec(
            num_scalar_prefetch=2,
            grid=(b1 - b0,),
            in_specs=in_specs,
            out_specs=pl.BlockSpec((MOE_ROWS, d // 2), lambda i, be, nb, b0=b0: (i + b0, 0)),
            scratch_shapes=[pltpu.VMEM((d, 2 * D_EXPERT), bf16), pltpu.VMEM((D_EXPERT, d), bf16)],
        )
        ys = pl.pallas_call(
            functools.partial(_expert_kernel, block0=b0),
            grid_spec=grid_spec,
            out_shape=jax.ShapeDtypeStruct((rows, d // 2), jnp.uint32),
            input_output_aliases=aliases,
            compiler_params=pltpu.CompilerParams(dimension_semantics=("arbitrary",), vmem_limit_bytes=VMEM_LIMIT),
            name="experts",
        )(*args)
    return ys


def _combine_kernel(x_ref, y_ref, gk_ref, gate_ref, g_ref, b_ref, *rest):
    o_ref = rest[-1]
    gk = gk_ref[...]
    y = _unpack_halves(y_ref[0]) * gk[:, 0:1]
    for k in range(1, TOP_K):
        y = y + _unpack_halves(y_ref[k]) * gk[:, k:k + 1]
    o_ref[...] = _ln(DN_ALPHA * x_ref[...] + gate_ref[0] * y) * g_ref[...] + b_ref[...]


def _combine(x1, ys, dest, gates, gate2, g, b, seq):
    n, d = x1.shape
    tm = min(ROW_TILE, seq)
    per_b = seq // tm
    tiles = n // tm
    chunks = min(COMBINE_CHUNKS, tiles)
    bounds = [tiles * c // chunks for c in range(chunks + 1)]
    const = lambda i: (0, 0)
    out = None
    for a0, a1 in zip(bounds[:-1], bounds[1:]):
        yk = ys[dest[a0 * tm:a1 * tm].T]
        row = lambda i, a0=a0: (i + a0, 0)
        in_specs = [pl.BlockSpec((tm, d), row),
                    pl.BlockSpec((TOP_K, tm, d // 2), lambda i: (0, i, 0)),
                    pl.BlockSpec((tm, TOP_K), row),
                    pl.BlockSpec((1, 1, d), lambda i, a0=a0: ((i + a0) // per_b, 0, 0)),
                    pl.BlockSpec((1, d), const), pl.BlockSpec((1, d), const)]
        args = [x1, yk, gates, gate2, g, b]
        aliases = {}
        if out is not None:
            in_specs.append(pl.BlockSpec(memory_space=pl.ANY))
            aliases = {len(args): 0}
            args.append(out)
        out = pl.pallas_call(
            _combine_kernel,
            grid=(a1 - a0,),
            in_specs=in_specs,
            out_specs=pl.BlockSpec((tm, d), row),
            out_shape=jax.ShapeDtypeStruct((n, d), f32),
            input_output_aliases=aliases,
            compiler_params=pltpu.CompilerParams(dimension_semantics=("parallel",), vmem_limit_bytes=VMEM_LIMIT),
            name="combine",
        )(*args)
    return out


def _route_kernel(logit_ref, tri_ref, idx_ref, gate_ref, rank_ref, count_ref, carry_ref):
    @pl.when(pl.program_id(0) == 0)
    def _():
        carry_ref[...] = jnp.zeros(carry_ref.shape, f32)

    tm = logit_ref.shape[0]
    lane = lax.broadcasted_iota(jnp.int32, (tm, LANES), 1).astype(f32)
    v = jnp.where(lane < N_EXPERTS, logit_ref[...], -jnp.inf)
    tops, picks = [], []
    for _ in range(TOP_K):
        top = jnp.max(v, axis=1, keepdims=True)
        pick = jnp.min(jnp.where(v == top, lane, float(LANES)), axis=1, keepdims=True)
        tops.append(top)
        picks.append(pick)
        v = jnp.where(lane == pick, -jnp.inf, v)
    exps = [jnp.exp(top - tops[0]) for top in tops]
    denom = functools.reduce(jnp.add, exps)
    hits = [lane == pick for pick in picks]
    onehot = functools.reduce(jnp.add, [jnp.where(hit, 1.0, 0.0) for hit in hits])
    before = _dot(tri_ref[...], onehot.astype(bf16)) + carry_ref[...]
    ranks = [jnp.sum(jnp.where(hit, before, 0.0), axis=1, keepdims=True) for hit in hits]

    def spread(cols):
        return functools.reduce(jnp.add, [jnp.where(lane == float(k), col, 0.0) for k, col in enumerate(cols)])

    idx_ref[...] = spread(picks).astype(jnp.int32)
    gate_ref[...] = spread([e / denom for e in exps])
    rank_ref[...] = spread(ranks).astype(jnp.int32)
    carry_ref[...] += jnp.sum(onehot, axis=0, keepdims=True)
    count_ref[...] = carry_ref[...]


def _route(logits):
    n = logits.shape[0]
    tm = ROW_TILE
    tri = (jnp.arange(tm)[:, None] > jnp.arange(tm)[None, :]).astype(bf16)
    row = lambda i: (i, 0)
    const = lambda i: (0, 0)
    idx, gates, rank, counts = pl.pallas_call(
        _route_kernel,
        grid=(n // tm,),
        in_specs=[pl.BlockSpec((tm, LANES), row), pl.BlockSpec((tm, tm), const)],
        out_specs=[pl.BlockSpec((tm, LANES), row), pl.BlockSpec((tm, LANES), row), pl.BlockSpec((tm, LANES), row),
                   pl.BlockSpec((1, LANES), const)],
        out_shape=[jax.ShapeDtypeStruct((n, LANES), jnp.int32), jax.ShapeDtypeStruct((n, LANES), f32),
                   jax.ShapeDtypeStruct((n, LANES), jnp.int32), jax.ShapeDtypeStruct((1, LANES), f32)],
        scratch_shapes=[pltpu.VMEM((1, LANES), f32)],
        compiler_params=pltpu.CompilerParams(dimension_semantics=("arbitrary",), vmem_limit_bytes=VMEM_LIMIT),
        name="route",
    )(logits, tri)
    top_idx, gates, rank = idx[:, :TOP_K], gates[:, :TOP_K], rank[:, :TOP_K]
    counts = counts[0, :N_EXPERTS].astype(jnp.int32)
    padded = (counts + MOE_ROWS - 1) // MOE_ROWS * MOE_ROWS
    pad_end = jnp.cumsum(padded)
    pad_start = pad_end - padded
    dest = pad_start[top_idx] + rank
    n_blocks = -(-(n * TOP_K + N_EXPERTS * (MOE_ROWS - 1)) // MOE_ROWS)
    block_start = jnp.arange(n_blocks, dtype=jnp.int32) * MOE_ROWS
    block_expert = jnp.minimum(jnp.sum((pad_end[None, :] <= block_start[:, None]).astype(jnp.int32), axis=1),
                               N_EXPERTS - 1)
    n_used = (pad_end[-1:] // MOE_ROWS).astype(jnp.int32)
    nk = n * TOP_K
    order = jnp.sort(top_idx.reshape(-1) * nk + jnp.arange(nk, dtype=jnp.int32))
    sorted_token = (order % nk) // TOP_K
    first = jnp.cumsum(counts) - counts
    offset = (block_start - pad_start[block_expert])[:, None] + jnp.arange(MOE_ROWS, dtype=jnp.int32)[None, :]
    source = jnp.clip(first[block_expert][:, None] + offset, 0, nk - 1)
    row_token = jnp.where(offset < counts[block_expert][:, None], sorted_token[source], 0).reshape(-1)
    return gates, dest, row_token, block_expert, n_used


def _split_w_in(w):
    return w[:, :C_QB].astype(bf16), jnp.pad(w[:, REST_BASE:], ((0, 0), (0, IDX_PAD))).astype(bf16)


def kernel(x, c, rel_bias, w_ada, b_ada, w_in, g_kv, w_uk, w_uv, w_a_out, w_b_out, w_o, ln1_g, ln1_b,
           w_router, b_router, w_gu, b_gu, w_dn, b_dn, ln2_g, ln2_b):
    B, S, D = x.shape
    N = B * S
    assert D == D_MODEL and S % ATT_BLOCK == 0
    x2 = x.reshape(N, D)
    for l in range(DEPTH):
        mod = _ada(c, l, w_ada, b_ada)
        shift1, scale1, gate1, shift2, scale2, gate2 = [m[:, None, :] for m in jnp.split(mod, 6, axis=-1)]

        qa, ckv, qidx, kidx, widx, qb, kb, vb, ga, gb = _proj(
            x2, shift1, scale1, *_split_w_in(w_in[l]), g_kv[l][None, :], S)
        wuk_t = jnp.transpose(w_uk[l], (1, 2, 0)).astype(bf16)
        wuv_t = jnp.transpose(w_uv[l], (1, 0, 2)).astype(bf16)
        oa = _dsa(qa, qidx, widx, kidx, ckv, wuk_t, wuv_t, rel_bias, B, S)
        ob = _sb(qb, kb, vb, B, S)

        wr = jnp.pad(w_router[l], ((0, 0), (0, LANES - N_EXPERTS)))
        wr_hi, wr_lo = _split_bf16(wr)
        br = jnp.pad(b_router[l], (0, LANES - N_EXPERTS))[None, :]
        x1, h2, logits = _merge(oa, ob, ga, gb, x2, gate1, shift2, scale2,
                                w_a_out[l].astype(bf16), w_b_out[l].astype(bf16), w_o[l].astype(bf16),
                                ln1_g[l][None, :], ln1_b[l][None, :], wr_hi, wr_lo, br, S)

        gates, dest, row_token, block_expert, n_used = _route(logits)
        ys = _experts(h2, row_token, block_expert, n_used, l, w_gu, b_gu[:, :, None, :], w_dn, b_dn[:, :, None, :])
        x2 = _combine(x1, ys, dest, gates, gate2, ln2_g[l][None, :], ln2_b[l][None, :], S)
    return x2.reshape(B, S, D)
```
